```python
import math
import jax
import jax.numpy as jnp
from jax import lax
import numpy as np

D_MODEL = 1024
BATCH = 4
SEQ = 4096
DEPTH = 1
DEC_BATCH = 128
DEC_SEQ = 4
PAST_LEN = 2048
PAGE_SIZE = 128

CONV_DIM = 768
CONV_WIDTH = 31
N_HEADS = 12
HEAD_DIM = 64
ATTN_DIM = N_HEADS * HEAD_DIM
MOBA_BLOCK = 256
MOBA_TOPK = 3
Q_CHUNK = 128
MEM_LEN = 256
MEM_HEADS = 4
MEM_HEAD_DIM = 128
MEM_DIM = MEM_HEADS * MEM_HEAD_DIM
D_FF = 2816
N_BUCKETS = 32
MAX_DISTANCE = 128
N_BRANCH = 3
NORM_EPS = 1e-6
NEG_INF = -1e30

kernel_name = 'hybrid_conformer_moba_memory_decoder_step'


def rms_norm(x, g):
    xf = x.astype(jnp.float32)
    y = xf * lax.rsqrt(jnp.mean(xf * xf, axis=-1, keepdims=True) + NORM_EPS)
    return (y * g.astype(jnp.float32)).astype(x.dtype)


def layer_norm(x, g, b):
    xf = x.astype(jnp.float32)
    mu = jnp.mean(xf, axis=-1, keepdims=True)
    var = jnp.mean(jnp.square(xf - mu), axis=-1, keepdims=True)
    y = (xf - mu) * lax.rsqrt(var + NORM_EPS)
    return (y * g.astype(jnp.float32) + b.astype(jnp.float32)).astype(x.dtype)


def swiglu(x, w_in, w_out):
    a, b = jnp.split(x @ w_in, 2, axis=-1)
    return (jax.nn.silu(a) * b) @ w_out


def rel_bucket(dist):
    n = jnp.maximum(dist, 0)
    max_exact = N_BUCKETS // 2
    nf = jnp.maximum(n, 1).astype(jnp.float32)
    large = max_exact + (jnp.log(nf / max_exact) / math.log(MAX_DISTANCE / max_exact)
                         * (N_BUCKETS - max_exact)).astype(jnp.int32)
    large = jnp.minimum(large, N_BUCKETS - 1)
    return jnp.where(n < max_exact, n, large)


def moba_attention(q, k, v, q_pos, rel_bias):
    n_q = q.shape[1]
    n_k = k.shape[1]
    l_pad = -(-n_k // MOBA_BLOCK) * MOBA_BLOCK
    n_blk = l_pad // MOBA_BLOCK
    n_sel = min(MOBA_TOPK, n_blk)
    g = math.gcd(n_q, Q_CHUNK)
    n_chunk = n_q // g
    pad = ((0, 0), (0, l_pad - n_k), (0, 0), (0, 0))
    k = jnp.pad(k, pad)
    v = jnp.pad(v, pad)
    scale = HEAD_DIM ** -0.5
    bias_t = rel_bias.astype(jnp.float32).T
    head_ix = jnp.arange(N_HEADS)[:, None, None, None]
    blk_ar = jnp.arange(MOBA_BLOCK, dtype=jnp.int32)
    blk_ids = jnp.arange(n_blk, dtype=jnp.int32)
    rank_ids = jnp.arange(n_sel, dtype=jnp.int32)
    pos_chunks = q_pos.reshape(n_chunk, g)

    def per_sequence(args):
        qs, ks, vs = args
        kb = ks.reshape(n_blk, MOBA_BLOCK, N_HEADS, HEAD_DIM).transpose(2, 0, 1, 3)
        vb = vs.reshape(n_blk, MOBA_BLOCK, N_HEADS, HEAD_DIM).transpose(2, 0, 1, 3)
        k_mean = jnp.mean(kb.astype(jnp.float32), axis=2)
        q_chunks = qs.reshape(n_chunk, g, N_HEADS, HEAD_DIM).transpose(0, 2, 1, 3)

        def per_chunk(cargs):
            qc, pc = cargs
            own = pc // MOBA_BLOCK
            score = jnp.einsum('hqd,hnd->hqn', qc.astype(jnp.float32), k_mean)
            score = jnp.where((blk_ids[None, :] < own[:, None])[None], score, NEG_INF)
            _, idx = lax.top_k(score, n_sel)
            sel_ok = rank_ids[None, :] < own[:, None]
            k_sel = jax.vmap(lambda a, i: a[i])(kb, idx)
            v_sel = jax.vmap(lambda a, i: a[i])(vb, idx)
            kpos_sel = idx[..., None] * MOBA_BLOCK + blk_ar
            lg_sel = (jnp.einsum('hqd,hqnkd->hqnk', qc, k_sel).astype(jnp.float32) * scale
                      + bias_t[head_ix, rel_bucket(pc[None, :, None, None] - kpos_sel)])
            lg_sel = jnp.where(sel_ok[None, :, :, None], lg_sel, NEG_INF)
            ob = pc[0] // MOBA_BLOCK
            k_own = lax.dynamic_index_in_dim(kb, ob, axis=1, keepdims=False)
            v_own = lax.dynamic_index_in_dim(vb, ob, axis=1, keepdims=False)
            dist_own = pc[:, None] - (ob * MOBA_BLOCK + blk_ar)[None, :]
            lg_own = (jnp.einsum('hqd,hkd->hqk', qc, k_own).astype(jnp.float32) * scale
                      + bias_t[:, rel_bucket(dist_own)])
            lg_own = jnp.where(dist_own[None] >= 0, lg_own, NEG_INF)
            logits = jnp.concatenate([lg_sel.reshape(N_HEADS, g, n_sel * MOBA_BLOCK), lg_own], axis=-1)
            prob = jax.nn.softmax(logits, axis=-1).astype(vs.dtype)
            p_sel = prob[..., :n_sel * MOBA_BLOCK].reshape(N_HEADS, g, n_sel, MOBA_BLOCK)
            p_own = prob[..., n_sel * MOBA_BLOCK:]
            return (jnp.einsum('hqnk,hqnkd->qhd', p_sel, v_sel)
                    + jnp.einsum('hqk,hkd->qhd', p_own, v_own))

        out = lax.map(per_chunk, (q_chunks, pos_chunks))
        return out.reshape(n_q, N_HEADS, HEAD_DIM)

    return lax.map(per_sequence, (q, k, v))


def memory_kv(mem, g_mem, w_mem_kv):
    n_b, n_m, _ = mem.shape
    mk, mv = jnp.split(rms_norm(mem, g_mem) @ w_mem_kv, 2, axis=-1)
    return (mk.reshape(n_b, n_m, MEM_HEADS, MEM_HEAD_DIM),
            mv.reshape(n_b, n_m, MEM_HEADS, MEM_HEAD_DIM))


def layer_step(x, conv_prev, k_past, v_past, mem_k, mem_v, rel_bias, p):
    n_b, n_q, _ = x.shape
    q_pos = k_past.shape[1] + jnp.arange(n_q, dtype=jnp.int32)
    h = x + 0.5 * rms_norm(swiglu(rms_norm(x, p['g_ffn1_pre']), p['w_ffn1_in'], p['w_ffn1_out']), p['g_ffn1_post'])
    u = rms_norm(h, p['g_mix_pre'])
    o1 = 2 * CONV_DIM
    o2 = o1 + ATTN_DIM
    o3 = o2 + ATTN_DIM
    o4 = o3 + ATTN_DIM
    o5 = o4 + MEM_DIM
    glu_in, q, k, v, qm, gate_pre = jnp.split(u @ p['w_in'], [o1, o2, o3, o4, o5], axis=-1)
    gates = jax.nn.sigmoid((gate_pre + p['b_gate']).astype(jnp.float32)).astype(x.dtype)
    gates = gates.reshape(n_b, n_q, N_BRANCH, D_MODEL)
    ga, gb = jnp.split(glu_in, 2, axis=-1)
    conv_in = jnp.concatenate([conv_prev, ga * jax.nn.sigmoid(gb)], axis=1)
    c = lax.conv_general_dilated(conv_in, p['w_dwconv'][:, None, :], (1,), 'VALID',
                                 dimension_numbers=('NWC', 'WIO', 'NWC'),
                                 feature_group_count=CONV_DIM) + p['b_dwconv']
    y_conv = jax.nn.silu(layer_norm(c, p['g_conv_ln'], p['b_conv_ln'])) @ p['w_conv_out']
    conv_state = conv_in[:, conv_in.shape[1] - (CONV_WIDTH - 1):]
    q = q.reshape(n_b, n_q, N_HEADS, HEAD_DIM)
    k = k.reshape(n_b, n_q, N_HEADS, HEAD_DIM)
    v = v.reshape(n_b, n_q, N_HEADS, HEAD_DIM)
    o = moba_attention(q, jnp.concatenate([k_past, k], axis=1), jnp.concatenate([v_past, v], axis=1),
                       q_pos, rel_bias)
    y_attn = o.reshape(n_b, n_q, ATTN_DIM) @ p['w_attn_out']
    qm = qm.reshape(n_b, n_q, MEM_HEADS, MEM_HEAD_DIM)
    lm = jnp.einsum('bqhd,bmhd->bhqm', qm, mem_k).astype(jnp.float32) * MEM_HEAD_DIM ** -0.5
    om = jnp.einsum('bhqm,bmhd->bqhd', jax.nn.softmax(lm, axis=-1).astype(x.dtype), mem_v)
    y_mem = om.reshape(n_b, n_q, MEM_DIM) @ p['w_mem_out']
    merged = gates[:, :, 0] * y_conv + gates[:, :, 1] * y_attn + gates[:, :, 2] * y_mem
    h = h + rms_norm(merged @ p['w_out'], p['g_mix_post'])
    y = h + 0.5 * rms_norm(swiglu(rms_norm(h, p['g_ffn2_pre']), p['w_ffn2_in'], p['w_ffn2_out']), p['g_ffn2_post'])
    return y, k, v, conv_state


def setup_inputs(seed: int = 0) -> dict:
    key = jax.random.key(seed)
    ks = iter(jax.random.split(key, 48))

    def nrm(shape, scale):
        return jax.random.normal(next(ks), shape, jnp.float32) * scale

    def gain(shape):
        return 1.0 + nrm(shape, 0.01)

    L = DEPTH
    D = D_MODEL
    n_pages = PAST_LEN // PAGE_SIZE
    n_used = DEC_BATCH * n_pages
    n_pool = (n_used * 5) // 4
    perm = jax.random.permutation(next(ks), n_pool)
    page_table = perm[:n_used].reshape(DEC_BATCH, n_pages).astype(jnp.int32)
    n_in = 2 * CONV_DIM + 3 * ATTN_DIM + MEM_DIM + N_BRANCH * D
    return {
        'x_prompt': nrm((BATCH, SEQ, D), 1.0),
        'x_sample': nrm((DEC_BATCH, DEC_SEQ, D), 1.0),
        'cache_k': nrm((L, n_pool, PAGE_SIZE, N_HEADS, HEAD_DIM), 1.0),
        'cache_v': nrm((L, n_pool, PAGE_SIZE, N_HEADS, HEAD_DIM), 1.0),
        'cache_mem_k': nrm((L, DEC_BATCH, MEM_LEN, MEM_HEADS, MEM_HEAD_DIM), 1.0),
        'cache_mem_v': nrm((L, DEC_BATCH, MEM_LEN, MEM_HEADS, MEM_HEAD_DIM), 1.0),
        'state_conv': nrm((L, DEC_BATCH, CONV_WIDTH - 1, CONV_DIM), 0.5),
        'page_table': page_table,
        'mem_prompt': nrm((BATCH, MEM_LEN, D), 1.0),
        'g_ffn1_pre': gain((L, D)),
        'w_ffn1_in': nrm((L, D, 2 * D_FF), D ** -0.5),
        'w_ffn1_out': nrm((L, D_FF, D), D_FF ** -0.5),
        'g_ffn1_post': gain((L, D)),
        'g_mix_pre': gain((L, D)),
        'w_in': nrm((L, D, n_in), D ** -0.5),
        'b_gate': nrm((L, N_BRANCH * D), 0.02),
        'w_dwconv': nrm((L, CONV_WIDTH, CONV_DIM), CONV_WIDTH ** -0.5),
        'b_dwconv': nrm((L, CONV_DIM), 0.02),
        'g_conv_ln': gain((L, CONV_DIM)),
        'b_conv_ln': nrm((L, CONV_DIM), 0.02),
        'w_conv_out': nrm((L, CONV_DIM, D), CONV_DIM ** -0.5),
        'w_attn_out': nrm((L, ATTN_DIM, D), ATTN_DIM ** -0.5),
        'rel_bias': nrm((N_BUCKETS, N_HEADS), 0.5),
        'g_mem': gain((L, D)),
        'w_mem_kv': nrm((L, D, 2 * MEM_DIM), D ** -0.5),
        'w_mem_out': nrm((L, MEM_DIM, D), MEM_DIM ** -0.5),
        'w_out': nrm((L, D, D), D ** -0.5),
        'g_mix_post': gain((L, D)),
        'g_ffn2_pre': gain((L, D)),
        'w_ffn2_in': nrm((L, D, 2 * D_FF), D ** -0.5),
        'w_ffn2_out': nrm((L, D_FF, D), D_FF ** -0.5),
        'g_ffn2_post': gain((L, D)),
    }


def reference(x_prompt, x_sample, cache_k, cache_v, cache_mem_k, cache_mem_v, state_conv, page_table,
              mem_prompt, g_ffn1_pre, w_ffn1_in, w_ffn1_out, g_ffn1_post, g_mix_pre, w_in, b_gate,
              w_dwconv, b_dwconv, g_conv_ln, b_conv_ln, w_conv_out, w_attn_out, rel_bias, g_mem,
              w_mem_kv, w_mem_out, w_out, g_mix_post, g_ffn2_pre, w_ffn2_in, w_ffn2_out, g_ffn2_post):
    n_dec = x_sample.shape[0]
    past_len = page_table.shape[1] * PAGE_SIZE
    hp = x_prompt
    hs = x_sample
    kp_l, vp_l, mkp_l, mvp_l, cp_l, ks_l, vs_l, cs_l = [], [], [], [], [], [], [], []
    for l in range(DEPTH):
        p = dict(g_ffn1_pre=g_ffn1_pre[l], w_ffn1_in=w_ffn1_in[l], w_ffn1_out=w_ffn1_out[l],
                 g_ffn1_post=g_ffn1_post[l], g_mix_pre=g_mix_pre[l], w_in=w_in[l], b_gate=b_gate[l],
                 w_dwconv=w_dwconv[l], b_dwconv=b_dwconv[l], g_conv_ln=g_conv_ln[l], b_conv_ln=b_conv_ln[l],
                 w_conv_out=w_conv_out[l], w_attn_out=w_attn_out[l], w_mem_out=w_mem_out[l],
                 w_out=w_out[l], g_mix_post=g_mix_post[l], g_ffn2_pre=g_ffn2_pre[l],
                 w_ffn2_in=w_ffn2_in[l], w_ffn2_out=w_ffn2_out[l], g_ffn2_post=g_ffn2_post[l])
        mk_p, mv_p = memory_kv(mem_prompt, g_mem[l], w_mem_kv[l])
        conv0 = jnp.zeros((hp.shape[0], CONV_WIDTH - 1, CONV_DIM), hp.dtype)
        kv0 = jnp.zeros((hp.shape[0], 0, N_HEADS, HEAD_DIM), hp.dtype)
        hp, kp, vp, cp = layer_step(hp, conv0, kv0, kv0, mk_p, mv_p, rel_bias, p)
        k_past = cache_k[l][page_table].reshape(n_dec, past_len, N_HEADS, HEAD_DIM)
        v_past = cache_v[l][page_table].reshape(n_dec, past_len, N_HEADS, HEAD_DIM)
        hs, ks_, vs_, cs = layer_step(hs, state_conv[l], k_past, v_past, cache_mem_k[l], cache_mem_v[l],
                                      rel_bias, p)
        kp_l.append(kp)
        vp_l.append(vp)
        mkp_l.append(mk_p)
        mvp_l.append(mv_p)
        cp_l.append(cp)
        ks_l.append(ks_)
        vs_l.append(vs_)
        cs_l.append(cs)
    return (hp, hs, jnp.stack(kp_l), jnp.stack(vp_l), jnp.stack(mkp_l), jnp.stack(mvp_l), jnp.stack(cp_l),
            jnp.stack(ks_l), jnp.stack(vs_l), jnp.stack(cs_l))
```

```python
import functools
import math

import jax
import jax.numpy as jnp
from jax import lax
from jax.experimental import pallas as pl
from jax.experimental.pallas import tpu as pltpu

_F32 = jnp.float32
_BF16 = jnp.bfloat16

_CONV_WIDTH = 31
_N_HEADS = 12
_HEAD_DIM = 64
_MOBA_BLOCK = 256
_MOBA_TOPK = 3
_Q_CHUNK = 128
_MEM_HEADS = 4
_MEM_HEAD_DIM = 128
_N_BUCKETS = 32
_MAX_DISTANCE = 128
_N_BRANCH = 3
_PAGE_SIZE = 128
_NORM_EPS = 1e-6
_NEG_INF = -1e30

_LANES = 128
_SUBLANES = 8
_VMEM_BYTES_V7X = 64 * 1024 * 1024
_VMEM_LIMIT = _VMEM_BYTES_V7X - 8 * 1024 * 1024

_HALO_ROWS = 32
_HEAD_GROUP = 16
_MEM_HEAD_GROUP = 8


def _nt_dot(a, b):
    return lax.dot_general(a, b, (((1,), (1,)), ((), ())), preferred_element_type=_F32)


def _dot(a, b):
    return jnp.dot(a, b, preferred_element_type=_F32)


def _sigmoid(x):
    return 1.0 / (1.0 + jnp.exp(-x))


def _rms_norm(x, g):
    return x * lax.rsqrt(jnp.mean(x * x, axis=-1, keepdims=True) + _NORM_EPS) * g


def _resident(shape):
    nd = len(shape)
    return pl.BlockSpec(shape, lambda *_: (0,) * nd, pipeline_mode=pl.Buffered(1))


def _params(n_parallel_axes):
    return pltpu.CompilerParams(
        dimension_semantics=("arbitrary",) * n_parallel_axes,
        vmem_limit_bytes=_VMEM_LIMIT)


def _ffn_half_step(x, g_pre, w_in_ref, w_out_ref, g_post, n_chunk):
    d_ff = w_out_ref.shape[0]
    fc = d_ff // n_chunk
    xn = _rms_norm(x, g_pre).astype(_BF16)
    acc = None
    for c in range(n_chunk):
        a = _dot(xn, w_in_ref[:, c * fc:(c + 1) * fc])
        b = _dot(xn, w_in_ref[:, d_ff + c * fc:d_ff + (c + 1) * fc])
        hid = (a * _sigmoid(a) * b).astype(_BF16)
        part = _dot(hid, w_out_ref[c * fc:(c + 1) * fc, :])
        acc = part if acc is None else acc + part
    return x + 0.5 * _rms_norm(acc, g_post)


def _ffn1_kernel(x_ref, gpre_ref, win_ref, wout_ref, gpost_ref, gmix_ref, h_ref, u_ref, *, n_chunk):
    h = _ffn_half_step(x_ref[...], gpre_ref[...], win_ref, wout_ref, gpost_ref[...], n_chunk)
    h_ref[...] = h
    u_ref[...] = _rms_norm(h, gmix_ref[...]).astype(_BF16)


def _ffn_chunks(d_ff):
    for n in (2, 1):
        if d_ff % (n * _LANES) == 0:
            return n
    return 1


def _ffn1(x, g_pre, w_in, w_out, g_post, g_mix, tm):
    n, d = x.shape
    d_ff = w_out.shape[0]
    row = lambda i: (i, 0)
    return pl.pallas_call(
        functools.partial(_ffn1_kernel, n_chunk=_ffn_chunks(d_ff)),
        grid=(n // tm,),
        in_specs=[pl.BlockSpec((tm, d), row), _resident((1, d)), _resident(w_in.shape),
                  _resident(w_out.shape), _resident((1, d)), _resident((1, d))],
        out_specs=[pl.BlockSpec((tm, d), row), pl.BlockSpec((tm, d), row)],
        out_shape=[jax.ShapeDtypeStruct((n, d), _F32), jax.ShapeDtypeStruct((n, d), _BF16)],
        compiler_params=_params(1),
        name="ffn1",
    )(x, g_pre, w_in, w_out, g_post, g_mix)


def _inproj_kernel(u_ref, w_ref, bg_ref, conv_ref, q_ref, k_ref, v_ref, qm_ref, gate_ref, *rest,
                   conv_dim, attn_dim, mem_dim):
    u = u_ref[0]
    o1 = 2 * conv_dim
    o2 = o1 + attn_dim
    o3 = o2 + attn_dim
    o4 = o3 + attn_dim
    o5 = o4 + mem_dim

    def proj(lo, hi):
        return _dot(u, w_ref[:, lo:hi])

    conv_ref[0] = proj(0, conv_dim) * _sigmoid(proj(conv_dim, o1))
    q_ref[0] = (proj(o1, o2) * (_HEAD_DIM ** -0.5)).astype(_BF16)
    k = proj(o2, o3)
    v = proj(o3, o4)
    k_ref[0] = k
    v_ref[0] = v
    qm_ref[0] = proj(o4, o5).astype(_BF16)
    gate_ref[0] = _sigmoid(proj(o5, w_ref.shape[1]) + bg_ref[...]).astype(_BF16)
    if rest:
        kb_ref, vt_ref = rest
        kb_ref[0] = k.astype(_BF16)
        vt_ref[0] = v.T.astype(_BF16)


def _inproj(u, w_in, b_gate, tm, conv_dim, attn_dim, mem_dim, attn_copies):
    nb, s, d = u.shape
    n_gate = b_gate.shape[1]
    tok = lambda b, i: (b, i, 0)
    out_specs = [pl.BlockSpec((1, tm, conv_dim), tok), pl.BlockSpec((1, tm, attn_dim), tok),
                 pl.BlockSpec((1, tm, attn_dim), tok), pl.BlockSpec((1, tm, attn_dim), tok),
                 pl.BlockSpec((1, tm, mem_dim), tok), pl.BlockSpec((1, tm, n_gate), tok)]
    out_shape = [jax.ShapeDtypeStruct((nb, s, conv_dim), _F32), jax.ShapeDtypeStruct((nb, s, attn_dim), _BF16),
                 jax.ShapeDtypeStruct((nb, s, attn_dim), _F32), jax.ShapeDtypeStruct((nb, s, attn_dim), _F32),
                 jax.ShapeDtypeStruct((nb, s, mem_dim), _BF16), jax.ShapeDtypeStruct((nb, s, n_gate), _BF16)]
    if attn_copies:
        out_specs += [pl.BlockSpec((1, tm, attn_dim), tok), pl.BlockSpec((1, attn_dim, tm), lambda b, i: (b, 0, i))]
        out_shape += [jax.ShapeDtypeStruct((nb, s, attn_dim), _BF16), jax.ShapeDtypeStruct((nb, attn_dim, s), _BF16)]
    return pl.pallas_call(
        functools.partial(_inproj_kernel, conv_dim=conv_dim, attn_dim=attn_dim, mem_dim=mem_dim),
        grid=(nb, s // tm),
        in_specs=[pl.BlockSpec((1, tm, d), tok), _resident(w_in.shape), _resident((1, n_gate))],
        out_specs=out_specs, out_shape=out_shape,
        compiler_params=_params(2),
        name="inproj",
    )(u, w_in, b_gate)


def _memkv_kernel(mem_ref, g_ref, w_ref, mk_ref, mv_ref, mkb_ref, mvb_ref):
    kv = _dot(_rms_norm(mem_ref[0], g_ref[...]).astype(_BF16), w_ref[...])
    half = kv.shape[1] // 2
    mk_ref[0] = kv[:, :half]
    mv_ref[0] = kv[:, half:]
    mkb_ref[0] = kv[:, :half].astype(_BF16)
    mvb_ref[0] = kv[:, half:].astype(_BF16)


def _memkv(mem, g_mem, w_mem_kv):
    nb, m, d = mem.shape
    md = w_mem_kv.shape[1] // 2
    blk = lambda b: (b, 0, 0)
    return pl.pallas_call(
        _memkv_kernel,
        grid=(nb,),
        in_specs=[pl.BlockSpec((1, m, d), blk), _resident((1, d)), _resident(w_mem_kv.shape)],
        out_specs=[pl.BlockSpec((1, m, md), blk)] * 4,
        out_shape=[jax.ShapeDtypeStruct((nb, m, md), _F32)] * 2 + [jax.ShapeDtypeStruct((nb, m, md), _BF16)] * 2,
        compiler_params=_params(1),
        name="memkv",
    )(mem, g_mem, w_mem_kv)


def _ln_swish(c, g, b):
    mu = jnp.mean(c, axis=-1, keepdims=True)
    cc = c - mu
    var = jnp.mean(cc * cc, axis=-1, keepdims=True)
    y = cc * lax.rsqrt(var + _NORM_EPS) * g + b
    return y * _sigmoid(y)


def _conv_kernel(main_ref, halo_ref, wd_ref, bd_ref, g_ref, b_ref, out_ref, ext_ref, *, rows):
    tt = main_ref.shape[1]
    ch = main_ref.shape[2]
    ext_ref[0:_HALO_ROWS, :] = jnp.where(pl.program_id(1) > 0, halo_ref[0], 0.0)
    ext_ref[_HALO_ROWS:, :] = main_ref[0]
    lead = _HALO_ROWS - (_CONV_WIDTH - 1)

    def chunk(i, carry):
        r0 = pl.multiple_of(i * rows, rows)
        cols = []
        for lb in range(ch // _LANES):
            ls = slice(lb * _LANES, (lb + 1) * _LANES)
            win = ext_ref[pl.ds(r0, rows + _HALO_ROWS), ls]
            acc = jnp.broadcast_to(bd_ref[:, ls], (rows, _LANES))
            for r in range(_SUBLANES):
                n_a = (_CONV_WIDTH - r + _SUBLANES - 1) // _SUBLANES
                sh = win[lead + r:lead + r + rows + _SUBLANES * (n_a - 1)]
                for a in range(n_a):
                    w = _SUBLANES * a + r
                    acc = acc + sh[_SUBLANES * a:_SUBLANES * a + rows] * wd_ref[w:w + 1, ls]
            cols.append(acc)
        c = jnp.concatenate(cols, axis=-1)
        out_ref[0, pl.ds(r0, rows), :] = _ln_swish(c, g_ref[...], b_ref[...]).astype(_BF16)
        return carry

    lax.fori_loop(0, tt // rows, chunk, 0)


def _conv_prompt(glu, w_dw, b_dw, g_ln, b_ln, tt):
    nb, s, ch = glu.shape
    per = tt // _HALO_ROWS
    return pl.pallas_call(
        functools.partial(_conv_kernel, rows=32),
        grid=(nb, s // tt),
        in_specs=[pl.BlockSpec((1, tt, ch), lambda b, t: (b, t, 0)),
                  pl.BlockSpec((1, _HALO_ROWS, ch), lambda b, t: (b, jnp.maximum(t * per - 1, 0), 0)),
                  _resident(w_dw.shape), _resident((1, ch)), _resident((1, ch)), _resident((1, ch))],
        out_specs=pl.BlockSpec((1, tt, ch), lambda b, t: (b, t, 0)),
        out_shape=jax.ShapeDtypeStruct((nb, s, ch), _BF16),
        scratch_shapes=[pltpu.VMEM((tt + _HALO_ROWS, ch), _F32)],
        compiler_params=_params(2),
        name="conv_prompt",
    )(glu, glu, w_dw, b_dw, g_ln, b_ln)


def _conv_step_kernel(ext_ref, wd_ref, bd_ref, g_ref, b_ref, out_ref):
    n_q = out_ref.shape[0]
    for t in range(n_q):
        acc = jnp.broadcast_to(bd_ref[...], ext_ref.shape[1:])
        for w in range(_CONV_WIDTH):
            acc = acc + ext_ref[t + w] * wd_ref[w:w + 1, :]
        out_ref[t] = _ln_swish(acc, g_ref[...], b_ref[...]).astype(_BF16)


def _conv_step(ext_tm, w_dw, b_dw, g_ln, b_ln):
    n_rows, n_seq, ch = ext_tm.shape
    n_q = n_rows - (_CONV_WIDTH - 1)
    return pl.pallas_call(
        _conv_step_kernel,
        grid=(1,),
        in_specs=[_resident(ext_tm.shape), _resident(w_dw.shape), _resident((1, ch)), _resident((1, ch)),
                  _resident((1, ch))],
        out_specs=pl.BlockSpec((n_q, n_seq, ch), lambda i: (0, 0, 0)),
        out_shape=jax.ShapeDtypeStruct((n_q, n_seq, ch), _BF16),
        compiler_params=_params(1),
        name="conv_step",
    )(ext_tm, w_dw, b_dw, g_ln, b_ln)


def _rel_bucket(dist):
    n = jnp.maximum(dist, 0)
    max_exact = _N_BUCKETS // 2
    nf = jnp.maximum(n, 1).astype(_F32)
    large = max_exact + (jnp.log(nf / max_exact) / math.log(_MAX_DISTANCE / max_exact)
                         * (_N_BUCKETS - max_exact)).astype(jnp.int32)
    large = jnp.minimum(large, _N_BUCKETS - 1)
    return jnp.where(n < max_exact, n, large)


def _bias_of_dist(rel_bias, dist):
    tab = jnp.take(rel_bias.astype(_F32).T, _rel_bucket(dist), axis=1)
    return jnp.where(dist >= 0, tab, _NEG_INF)


def _moba_prompt_kernel(far_ref, q_ref, k_ref, vt_ref, bias_ref, o_ref, kmean_ref, pen_ref, pen0_ref, *, n_blk):
    hp = pl.program_id(1)
    c = pl.program_id(2)
    own = c // 2
    par = c % 2
    half = _HEAD_DIM

    @pl.when(c == 0)
    def _():
        for j in range(n_blk):
            kj = k_ref[0, j * _MOBA_BLOCK:(j + 1) * _MOBA_BLOCK, :].astype(_F32)
            kmean_ref[j:j + 1, :] = jnp.mean(kj, axis=0, keepdims=True)

    qf = q_ref[0].astype(_F32)
    lane = lax.broadcasted_iota(jnp.int32, qf.shape, 1)
    blk = lax.broadcasted_iota(jnp.int32, (n_blk, _Q_CHUNK), 0)
    kmean = kmean_ref[...].astype(_BF16)
    prev = jnp.maximum(own - 1, 0)
    outs = []
    for hd in range(2):
        qh = jnp.where((lane >= hd * half) & (lane < (hd + 1) * half), qf, 0.0).astype(_BF16)
        score = _nt_dot(kmean, qh)
        rank = jnp.zeros(score.shape, jnp.int32)
        for m in range(n_blk):
            sm = score[m:m + 1, :]
            counts = jnp.where(m < own, 1, 0)
            rank = rank + jnp.where(sm > score, counts, 0) + jnp.where(sm == score, jnp.where(m < blk, counts, 0), 0)
        sel = (blk < own) & (rank < _MOBA_TOPK)
        far = far_ref[hp * 2 + hd]
        pen_ref[hd] = jnp.where(sel, far, _NEG_INF)
        pen0_ref[hd] = jnp.where(sel, 0.0, _NEG_INF)

        def block(j, add, carry, hd=hd, qh=qh):
            m, l, acc = carry
            start = pl.multiple_of(j * _MOBA_BLOCK, _MOBA_BLOCK)
            s = _nt_dot(k_ref[0, pl.ds(start, _MOBA_BLOCK), :], qh) + add
            m_new = jnp.maximum(m, jnp.max(s, axis=0, keepdims=True))
            alpha = jnp.exp(m - m_new)
            p = jnp.exp(s - m_new)
            l = alpha * l + jnp.sum(p, axis=0, keepdims=True)
            vj = vt_ref[0, hd * half:(hd + 1) * half, pl.ds(start, _MOBA_BLOCK)]
            acc = alpha * acc + _dot(vj, p.astype(_BF16))
            return m_new, l, acc

        carry = (jnp.full((1, _Q_CHUNK), _NEG_INF, _F32), jnp.zeros((1, _Q_CHUNK), _F32),
                 jnp.zeros((half, _Q_CHUNK), _F32))
        carry = block(own, bias_ref[hd, par], carry)
        carry = block(prev, bias_ref[hd, 2 + par] + pen0_ref[hd, pl.ds(prev, 1), :], carry)
        carry = lax.fori_loop(
            0, prev, lambda j, cr, hd=hd, block=block: block(j, pen_ref[hd, pl.ds(j, 1), :], cr), carry)
        _, l, acc = carry
        outs.append(acc * (1.0 / l))
    o_ref[0] = jnp.concatenate(outs, axis=0).T.astype(_BF16)


def _moba_prompt(q, kb, vt, rel_bias):
    nb, s, attn_dim = q.shape
    n_blk = s // _MOBA_BLOCK
    n_pair = attn_dim // _LANES
    kk = jnp.arange(_MOBA_BLOCK, dtype=jnp.int32)[:, None]
    qq = jnp.arange(_Q_CHUNK, dtype=jnp.int32)[None, :]
    dist = jnp.stack([qq - kk, _Q_CHUNK + qq - kk,
                      _MOBA_BLOCK + qq - kk, _MOBA_BLOCK + _Q_CHUNK + qq - kk])
    tiles = _bias_of_dist(rel_bias, dist)
    far = _bias_of_dist(rel_bias, jnp.full((1,), _MOBA_BLOCK + 1, jnp.int32))[:, 0]
    return pl.pallas_call(
        functools.partial(_moba_prompt_kernel, n_blk=n_blk),
        grid=(nb, n_pair, s // _Q_CHUNK),
        in_specs=[pl.BlockSpec(memory_space=pltpu.SMEM),
                  pl.BlockSpec((1, _Q_CHUNK, _LANES), lambda b, h, c: (b, c, h)),
                  pl.BlockSpec((1, s, _LANES), lambda b, h, c: (b, 0, h)),
                  pl.BlockSpec((1, _LANES, s), lambda b, h, c: (b, h, 0)),
                  pl.BlockSpec((2, 4, _MOBA_BLOCK, _Q_CHUNK), lambda b, h, c: (h, 0, 0, 0))],
        out_specs=pl.BlockSpec((1, _Q_CHUNK, _LANES), lambda b, h, c: (b, c, h)),
        out_shape=jax.ShapeDtypeStruct((nb, s, attn_dim), _BF16),
        scratch_shapes=[pltpu.VMEM((n_blk, _LANES), _F32), pltpu.VMEM((2, n_blk, _Q_CHUNK), _F32),
                        pltpu.VMEM((2, n_blk, _Q_CHUNK), _F32)],
        compiler_params=_params(3),
        name="moba_prompt",
    )(far, q, kb, vt, tiles)


def _moba_step_kernel(pt_ref, qbd_ref, knew_ref, vnew_ref, btile_ref, blkmap_ref, *rest, n_pages, n_q):
    del pt_ref
    k_pages = rest[:n_pages]
    v_pages = rest[n_pages:2 * n_pages]
    o_ref, kbuf, vbuf, kmean_ref = rest[2 * n_pages:]
    past = n_pages * _PAGE_SIZE
    pages_per_blk = _MOBA_BLOCK // _PAGE_SIZE
    n_blk = past // _MOBA_BLOCK
    feat = kbuf.shape[1]

    kmean_ref[...] = jnp.zeros(kmean_ref.shape, _F32)
    for j in range(n_blk):
        tot = None
        for p in range(j * pages_per_blk, (j + 1) * pages_per_blk):
            kp = k_pages[p][0]
            kbuf[p * _PAGE_SIZE:(p + 1) * _PAGE_SIZE, :] = kp.astype(_BF16)
            vbuf[p * _PAGE_SIZE:(p + 1) * _PAGE_SIZE, :] = v_pages[p][0].astype(_BF16)
            ps = jnp.sum(kp, axis=0, keepdims=True)
            tot = ps if tot is None else tot + ps
        kmean_ref[j:j + 1, :] = tot * (1.0 / _MOBA_BLOCK)
    n_new = knew_ref.shape[1]
    pad = jnp.zeros((_LANES - n_new, feat), _BF16)
    kbuf[past:, :] = jnp.concatenate([knew_ref[0].astype(_BF16), pad], axis=0)
    vbuf[past:, :] = jnp.concatenate([vnew_ref[0].astype(_BF16), pad], axis=0)

    qbd = qbd_ref[0]
    rows = qbd.shape[0]
    n_col = kmean_ref.shape[0]
    score = _nt_dot(qbd, kmean_ref[...].astype(_BF16))
    col = lax.broadcasted_iota(jnp.int32, score.shape, 1)
    rank = jnp.zeros(score.shape, jnp.int32)
    for m in range(n_blk):
        sm = score[:, m:m + 1]
        rank = rank + jnp.where(sm > score, 1, 0) + jnp.where(sm == score, jnp.where(m < col, 1, 0), 0)
    keep = ((col < n_blk) & (rank < _MOBA_TOPK)) | (col == n_blk)
    keep_cols = _dot(jnp.where(keep, 1.0, 0.0).astype(_BF16), blkmap_ref[...])
    logits = _nt_dot(qbd, kbuf[...]) + btile_ref[...]
    logits = jnp.where(keep_cols > 0.5, logits, _NEG_INF)
    m_row = jnp.max(logits, axis=1, keepdims=True)
    p = jnp.exp(logits - m_row)
    p = p * (1.0 / jnp.sum(p, axis=1, keepdims=True))
    out = _dot(p.astype(_BF16), vbuf[...])
    r_head = lax.broadcasted_iota(jnp.int32, out.shape, 0) % _HEAD_GROUP
    c_head = lax.broadcasted_iota(jnp.int32, out.shape, 1) // _HEAD_DIM
    out = jnp.where(r_head == c_head, out, 0.0)
    o_ref[0] = jnp.sum(out.reshape(n_q, _HEAD_GROUP, feat), axis=1)
    del rows, n_col


def _block_diag_queries(q, n_heads, head_dim, group):
    n, n_q, feat = q.shape
    r_head = jnp.arange(group, dtype=jnp.int32)[:, None]
    c_head = jnp.arange(feat, dtype=jnp.int32)[None, :] // head_dim
    mask = (r_head == c_head) & (r_head < n_heads)
    return jnp.where(mask[None, None], q[:, :, None, :], jnp.zeros((), q.dtype)).reshape(n, n_q * group, feat)


def _moba_step(page_table, q, k_new, v_new, cache_k, cache_v, rel_bias):
    n_seq, n_q, feat = q.shape
    n_pages = page_table.shape[1]
    past = n_pages * _PAGE_SIZE
    assert past % _MOBA_BLOCK == 0 and n_q <= _Q_CHUNK and n_q <= 16
    n_blk = past // _MOBA_BLOCK
    n_keys = past + _LANES
    rows = n_q * _HEAD_GROUP
    new_rows = 16
    qbd = _block_diag_queries(q, _N_HEADS, _HEAD_DIM, _HEAD_GROUP)
    padn = ((0, 0), (0, new_rows - n_q), (0, 0))
    k_new = jnp.pad(k_new, padn)
    v_new = jnp.pad(v_new, padn)
    qi = jnp.arange(rows, dtype=jnp.int32)[:, None] // _HEAD_GROUP
    hh = jnp.arange(rows, dtype=jnp.int32)[:, None] % _HEAD_GROUP
    kpos = jnp.arange(n_keys, dtype=jnp.int32)[None, :]
    dist = jnp.where(kpos < past + n_q, past + qi - kpos, -1)
    btile = jnp.where(dist >= 0, rel_bias.astype(_F32)[_rel_bucket(dist), jnp.minimum(hh, _N_HEADS - 1)],
                      _NEG_INF)
    n_col = 16
    blk_of_key = jnp.minimum(kpos // _MOBA_BLOCK, n_blk)
    blkmap = (jnp.arange(n_col, dtype=jnp.int32)[:, None] == blk_of_key).astype(_BF16)

    page = lambda p: pl.BlockSpec((1, _PAGE_SIZE, feat), lambda i, pt, p=p: (pt[i, p], 0, 0))
    seq3 = lambda r: pl.BlockSpec((1, r, feat), lambda i, pt: (i, 0, 0))
    const2 = lambda shape: pl.BlockSpec(shape, lambda i, pt: (0, 0), pipeline_mode=pl.Buffered(1))
    grid_spec = pltpu.PrefetchScalarGridSpec(
        num_scalar_prefetch=1,
        grid=(n_seq,),
        in_specs=[seq3(rows), seq3(new_rows), seq3(new_rows), const2((rows, n_keys)), const2((n_col, n_keys))]
                 + [page(p) for p in range(n_pages)] * 2,
        out_specs=pl.BlockSpec((1, n_q, feat), lambda i, pt: (i, 0, 0)),
        scratch_shapes=[pltpu.VMEM((n_keys, feat), _BF16), pltpu.VMEM((n_keys, feat), _BF16),
                        pltpu.VMEM((n_col, feat), _F32)])
    return pl.pallas_call(
        functools.partial(_moba_step_kernel, n_pages=n_pages, n_q=n_q),
        grid_spec=grid_spec,
        out_shape=jax.ShapeDtypeStruct((n_seq, n_q, feat), _F32),
        compiler_params=_params(1),
        name="moba_step",
    )(page_table, qbd, k_new, v_new, btile, blkmap, *([cache_k] * n_pages), *([cache_v] * n_pages))


def _softmax_rows(s):
    p = jnp.exp(s - jnp.max(s, axis=1, keepdims=True))
    return p * (1.0 / jnp.sum(p, axis=1, keepdims=True))


def _memattn_kernel(qm_ref, mk_ref, mv_ref, o_ref):
    scale = _MEM_HEAD_DIM ** -0.5
    for h in range(_MEM_HEADS):
        hs = slice(h * _MEM_HEAD_DIM, (h + 1) * _MEM_HEAD_DIM)
        p = _softmax_rows(_nt_dot(qm_ref[0, :, hs], mk_ref[0, :, hs]) * scale)
        o_ref[0, :, hs] = _dot(p.astype(_BF16), mv_ref[0, :, hs]).astype(_BF16)


def _memattn(qm, mkb, mvb, tq):
    nb, s, md = qm.shape
    m = mkb.shape[1]
    return pl.pallas_call(
        _memattn_kernel,
        grid=(nb, s // tq),
        in_specs=[pl.BlockSpec((1, tq, md), lambda b, i: (b, i, 0)),
                  pl.BlockSpec((1, m, md), lambda b, i: (b, 0, 0)),
                  pl.BlockSpec((1, m, md), lambda b, i: (b, 0, 0))],
        out_specs=pl.BlockSpec((1, tq, md), lambda b, i: (b, i, 0)),
        out_shape=jax.ShapeDtypeStruct((nb, s, md), _BF16),
        compiler_params=_params(2),
        name="memattn",
    )(qm, mkb, mvb)


def _memattn_step_kernel(qbd_ref, mk_ref, mv_ref, o_ref, *, n_q):
    scale = _MEM_HEAD_DIM ** -0.5
    md = mk_ref.shape[2]
    for i in range(mk_ref.shape[0]):
        qbd = qbd_ref[i]
        p = _softmax_rows(_nt_dot(qbd, mk_ref[i].astype(_BF16)) * scale)
        out = _dot(p.astype(_BF16), mv_ref[i].astype(_BF16))
        r_head = lax.broadcasted_iota(jnp.int32, out.shape, 0) % _MEM_HEAD_GROUP
        c_head = lax.broadcasted_iota(jnp.int32, out.shape, 1) // _MEM_HEAD_DIM
        out = jnp.where(r_head == c_head, out, 0.0)
        o_ref[i] = jnp.sum(out.reshape(n_q, _MEM_HEAD_GROUP, md), axis=1)


def _memattn_step(qm, mem_k, mem_v, seqs_per_step):
    n_seq, n_q, md = qm.shape
    m = mem_k.shape[1]
    qbd = _block_diag_queries(qm, _MEM_HEADS, _MEM_HEAD_DIM, _MEM_HEAD_GROUP)
    rows = n_q * _MEM_HEAD_GROUP
    blk = lambda i: (i, 0, 0)
    return pl.pallas_call(
        functools.partial(_memattn_step_kernel, n_q=n_q),
        grid=(n_seq // seqs_per_step,),
        in_specs=[pl.BlockSpec((seqs_per_step, rows, md), blk), pl.BlockSpec((seqs_per_step, m, md), blk),
                  pl.BlockSpec((seqs_per_step, m, md), blk)],
        out_specs=pl.BlockSpec((seqs_per_step, n_q, md), blk),
        out_shape=jax.ShapeDtypeStruct((n_seq, n_q, md), _F32),
        compiler_params=_params(1),
        name="memattn_step",
    )(qbd, mem_k, mem_v)


def _merge_kernel(h_ref, c_ref, o_ref, om_ref, gate_ref, wc_ref, wa_ref, wm_ref, wo_ref, gpost_ref,
                  g2pre_ref, w2in_ref, w2out_ref, g2post_ref, y_ref, *, n_chunk):
    d = h_ref.shape[1]
    merged = (gate_ref[:, 0:d].astype(_F32) * _dot(c_ref[...], wc_ref[...])
              + gate_ref[:, d:2 * d].astype(_F32) * _dot(o_ref[...], wa_ref[...])
              + gate_ref[:, 2 * d:3 * d].astype(_F32) * _dot(om_ref[...], wm_ref[...]))
    h = h_ref[...] + _rms_norm(_dot(merged.astype(_BF16), wo_ref[...]), gpost_ref[...])
    y_ref[...] = _ffn_half_step(h, g2pre_ref[...], w2in_ref, w2out_ref, g2post_ref[...], n_chunk)


def _merge(h, c_act, o, om, gates, w_conv_out, w_attn_out, w_mem_out, w_out, g_post, g2_pre, w2_in, w2_out,
           g2_post, tm):
    n, d = h.shape
    row = lambda i: (i, 0)
    tile = lambda a: pl.BlockSpec((tm, a.shape[1]), row)
    return pl.pallas_call(
        functools.partial(_merge_kernel, n_chunk=_ffn_chunks(w2_out.shape[0])),
        grid=(n // tm,),
        in_specs=[tile(h), tile(c_act), tile(o), tile(om), tile(gates),
                  _resident(w_conv_out.shape), _resident(w_attn_out.shape), _resident(w_mem_out.shape),
                  _resident(w_out.shape), _resident((1, d)), _resident((1, d)), _resident(w2_in.shape),
                  _resident(w2_out.shape), _resident((1, d))],
        out_specs=pl.BlockSpec((tm, d), row),
        out_shape=jax.ShapeDtypeStruct((n, d), _F32),
        compiler_params=_params(1),
        name="merge_ffn2",
    )(h, c_act, o, om, gates, w_conv_out, w_attn_out, w_mem_out, w_out, g_post, g2_pre, w2_in, w2_out, g2_post)


def _tile(n, want):
    t = min(n, want)
    assert n % t == 0
    return t


def _token_front(x, p):
    nb, s, d = x.shape
    n = nb * s
    h, u = _ffn1(x.reshape(n, d), p['g_ffn1_pre'], p['w_ffn1_in'], p['w_ffn1_out'], p['g_ffn1_post'],
                 p['g_mix_pre'], _tile(n, 512))
    return h, u.reshape(nb, s, d)


def _token_back(h, c_act, o, om, gates, p):
    n = h.shape[0]
    flat = lambda a: a.reshape(n, a.shape[-1])
    return _merge(h, flat(c_act), flat(o), flat(om), flat(gates), p['w_conv_out'], p['w_attn_out'],
                  p['w_mem_out'], p['w_out'], p['g_mix_post'], p['g_ffn2_pre'], p['w_ffn2_in'], p['w_ffn2_out'],
                  p['g_ffn2_post'], _tile(n, 256))


def _dims(p):
    conv_dim = p['w_conv_out'].shape[0]
    attn_dim = p['w_attn_out'].shape[0]
    mem_dim = p['w_mem_out'].shape[0]
    return conv_dim, attn_dim, mem_dim


def _prompt_layer(x, mem, rel_bias, p):
    nb, s, d = x.shape
    conv_dim, attn_dim, mem_dim = _dims(p)
    h, u = _token_front(x, p)
    glu, q, k, v, qm, gates, kb, vt = _inproj(u, p['w_in'], p['b_gate'], _tile(s, 512), conv_dim, attn_dim,
                                              mem_dim, True)
    mk, mv, mkb, mvb = _memkv(mem, p['g_mem'], p['w_mem_kv'])
    c_act = _conv_prompt(glu, p['w_dwconv'], p['b_dwconv'], p['g_conv_ln'], p['b_conv_ln'], _tile(s, 512))
    o = _moba_prompt(q, kb, vt, rel_bias)
    om = _memattn(qm, mkb, mvb, _tile(s, 512))
    y = _token_back(h, c_act, o, om, gates, p).reshape(nb, s, d)
    conv_state = glu[:, s - (_CONV_WIDTH - 1):]
    heads = lambda a: a.reshape(nb, s, _N_HEADS, _HEAD_DIM)
    mheads = lambda a: a.reshape(nb, a.shape[1], _MEM_HEADS, _MEM_HEAD_DIM)
    return y, heads(k), heads(v), mheads(mk), mheads(mv), conv_state


def _sample_layer(x, conv_prev, cache_k, cache_v, page_table, mem_k, mem_v, rel_bias, p):
    n_seq, n_q, d = x.shape
    conv_dim, attn_dim, mem_dim = _dims(p)
    n = n_seq * n_q
    h, u = _token_front(x.reshape(1, n, d), p)
    glu, q, k, v, qm, gates = _inproj(u, p['w_in'], p['b_gate'], _tile(n, 512), conv_dim, attn_dim, mem_dim,
                                      False)
    per_seq = lambda a: a.reshape(n_seq, n_q, a.shape[-1])
    glu = per_seq(glu)
    conv_in = jnp.concatenate([conv_prev, glu], axis=1)
    c_act = _conv_step(conv_in.transpose(1, 0, 2), p['w_dwconv'], p['b_dwconv'], p['g_conv_ln'], p['b_conv_ln'])
    c_act = c_act.transpose(1, 0, 2)
    flat_pages = lambda c: c.reshape(c.shape[0], c.shape[1], attn_dim)
    o = _moba_step(page_table, per_seq(q), per_seq(k), per_seq(v), flat_pages(cache_k), flat_pages(cache_v),
                   rel_bias)
    om = _memattn_step(per_seq(qm), mem_k.reshape(n_seq, -1, mem_dim), mem_v.reshape(n_seq, -1, mem_dim), 8)
    y = _token_back(h, c_act, o.astype(_BF16), om.astype(_BF16), gates, p).reshape(n_seq, n_q, d)
    conv_state = conv_in[:, n_q:]
    heads = lambda a: a.reshape(n_seq, n_q, _N_HEADS, _HEAD_DIM)
    return y, heads(k), heads(v), conv_state


def _layer_params(l, g_ffn1_pre, w_ffn1_in, w_ffn1_out, g_ffn1_post, g_mix_pre, w_in, b_gate, w_dwconv, b_dwconv,
                  g_conv_ln, b_conv_ln, w_conv_out, w_attn_out, g_mem, w_mem_kv, w_mem_out, w_out, g_mix_post,
                  g_ffn2_pre, w_ffn2_in, w_ffn2_out, g_ffn2_post):
    vec = lambda a: a[l].astype(_F32).reshape(1, -1)
    mat = lambda a: a[l].astype(_BF16)
    return dict(
        g_ffn1_pre=vec(g_ffn1_pre), w_ffn1_in=mat(w_ffn1_in), w_ffn1_out=mat(w_ffn1_out),
        g_ffn1_post=vec(g_ffn1_post), g_mix_pre=vec(g_mix_pre), w_in=mat(w_in), b_gate=vec(b_gate),
        w_dwconv=w_dwconv[l].astype(_F32), b_dwconv=vec(b_dwconv), g_conv_ln=vec(g_conv_ln),
        b_conv_ln=vec(b_conv_ln), w_conv_out=mat(w_conv_out), w_attn_out=mat(w_attn_out), g_mem=vec(g_mem),
        w_mem_kv=mat(w_mem_kv), w_mem_out=mat(w_mem_out), w_out=mat(w_out), g_mix_post=vec(g_mix_post),
        g_ffn2_pre=vec(g_ffn2_pre), w_ffn2_in=mat(w_ffn2_in), w_ffn2_out=mat(w_ffn2_out),
        g_ffn2_post=vec(g_ffn2_post))


def kernel(x_prompt, x_sample, cache_k, cache_v, cache_mem_k, cache_mem_v, state_conv, page_table, mem_prompt, g_ffn1_pre, w_ffn1_in, w_ffn1_out, g_ffn1_post, g_mix_pre, w_in, b_gate, w_dwconv, b_dwconv, g_conv_ln, b_conv_ln, w_conv_out, w_attn_out, rel_bias, g_mem, w_mem_kv, w_mem_out, w_out, g_mix_post, g_ffn2_pre, w_ffn2_in, w_ffn2_out, g_ffn2_post):
    depth = w_in.shape[0]
    hp, hs = x_prompt, x_sample
    outs = [[] for _ in range(8)]
    for l in range(depth):
        p = _layer_params(l, g_ffn1_pre, w_ffn1_in, w_ffn1_out, g_ffn1_post, g_mix_pre, w_in, b_gate, w_dwconv,
                          b_dwconv, g_conv_ln, b_conv_ln, w_conv_out, w_attn_out, g_mem, w_mem_kv, w_mem_out,
                          w_out, g_mix_post, g_ffn2_pre, w_ffn2_in, w_ffn2_out, g_ffn2_post)
        hp, kp, vp, mkp, mvp, cp = _prompt_layer(hp, mem_prompt, rel_bias, p)
        hs, ks, vs, cs = _sample_layer(hs, state_conv[l], cache_k[l], cache_v[l], page_table, cache_mem_k[l],
                                       cache_mem_v[l], rel_bias, p)
        for acc, val in zip(outs, (kp, vp, mkp, mvp, cp, ks, vs, cs)):
            acc.append(val)
    return (hp, hs) + tuple(jnp.stack(a) for a in outs)
```

```python
import functools
import math

import jax
import jax.numpy as jnp
from jax import lax
from jax.experimental import pallas as pl
from jax.experimental.pallas import tpu as pltpu

_F32 = jnp.float32
_BF16 = jnp.bfloat16

_CONV_WIDTH = 31
_N_HEADS = 12
_HEAD_DIM = 64
_MOBA_BLOCK = 256
_MOBA_TOPK = 3
_MEM_HEADS = 4
_MEM_HEAD_DIM = 128
_N_BUCKETS = 32
_MAX_DISTANCE = 128
_N_BRANCH = 3
_PAGE_SIZE = 128
_NORM_EPS = 1e-6
_NEG_INF = -1e30

_LANES = 128
_SUBLANES = 8
_VMEM_BYTES_V7X = 64 * 1024 * 1024
_VMEM_LIMIT = _VMEM_BYTES_V7X - 8 * 1024 * 1024

_HALO_ROWS = 32
_HEAD_GROUP = 16
_NEW_ROWS = 16


def _nt_dot(a, b):
    return lax.dot_general(a, b, (((1,), (1,)), ((), ())), preferred_element_type=_F32)


def _dot(a, b):
    return jnp.dot(a, b, preferred_element_type=_F32)


def _sigmoid(x):
    return 1.0 / (1.0 + jnp.exp(-x))


def _rms_norm(x, g):
    return x * lax.rsqrt(jnp.mean(x * x, axis=-1, keepdims=True) + _NORM_EPS) * g


def _resident(shape):
    nd = len(shape)
    return pl.BlockSpec(shape, lambda *_: (0,) * nd, pipeline_mode=pl.Buffered(1))


def _params(n_parallel_axes):
    return pltpu.CompilerParams(
        dimension_semantics=("arbitrary",) * n_parallel_axes,
        vmem_limit_bytes=_VMEM_LIMIT)


def _ffn_half_step(x, g_pre, w_in_ref, w_out_ref, g_post, n_chunk):
    d_ff = w_out_ref.shape[0]
    fc = d_ff // n_chunk
    xn = _rms_norm(x, g_pre).astype(_BF16)
    acc = None
    for c in range(n_chunk):
        a = _dot(xn, w_in_ref[:, c * fc:(c + 1) * fc])
        b = _dot(xn, w_in_ref[:, d_ff + c * fc:d_ff + (c + 1) * fc])
        hid = (a * _sigmoid(a) * b).astype(_BF16)
        part = _dot(hid, w_out_ref[c * fc:(c + 1) * fc, :])
        acc = part if acc is None else acc + part
    return x + 0.5 * _rms_norm(acc, g_post)


def _ffn1_kernel(x_ref, gpre_ref, win_ref, wout_ref, gpost_ref, gmix_ref, h_ref, u_ref, *, n_chunk):
    h = _ffn_half_step(x_ref[...], gpre_ref[...], win_ref, wout_ref, gpost_ref[...], n_chunk)
    h_ref[...] = h
    u_ref[...] = _rms_norm(h, gmix_ref[...]).astype(_BF16)


def _ffn_chunks(d_ff):
    for n in (2, 1):
        if d_ff % (n * _LANES) == 0:
            return n
    return 1


def _ffn1(x, g_pre, w_in, w_out, g_post, g_mix, tm):
    n, d = x.shape
    d_ff = w_out.shape[0]
    row = lambda i: (i, 0)
    return pl.pallas_call(
        functools.partial(_ffn1_kernel, n_chunk=_ffn_chunks(d_ff)),
        grid=(n // tm,),
        in_specs=[pl.BlockSpec((tm, d), row), _resident((1, d)), _resident(w_in.shape),
                  _resident(w_out.shape), _resident((1, d)), _resident((1, d))],
        out_specs=[pl.BlockSpec((tm, d), row), pl.BlockSpec((tm, d), row)],
        out_shape=[jax.ShapeDtypeStruct((n, d), _F32), jax.ShapeDtypeStruct((n, d), _BF16)],
        compiler_params=_params(1),
        name="ffn1",
    )(x, g_pre, w_in, w_out, g_post, g_mix)


def _inproj_kernel(u_ref, w_ref, bg_ref, conv_ref, q_ref, k_ref, v_ref, qm_ref, gate_ref, *rest,
                   conv_dim, attn_dim, mem_dim):
    u = u_ref[0]
    o1 = 2 * conv_dim
    o2 = o1 + attn_dim
    o3 = o2 + attn_dim
    o4 = o3 + attn_dim
    o5 = o4 + mem_dim

    def proj(lo, hi):
        return _dot(u, w_ref[:, lo:hi])

    conv_ref[0] = proj(0, conv_dim) * _sigmoid(proj(conv_dim, o1))
    q_ref[0] = (proj(o1, o2) * (_HEAD_DIM ** -0.5)).astype(_BF16)
    k = proj(o2, o3)
    v = proj(o3, o4)
    k_ref[0] = k
    v_ref[0] = v
    qm_ref[0] = proj(o4, o5).astype(_BF16)
    gate_ref[0] = _sigmoid(proj(o5, w_ref.shape[1]) + bg_ref[...]).astype(_BF16)
    if rest:
        kb_ref, vt_ref = rest
        kb_ref[0] = k.astype(_BF16)
        vt_ref[0] = v.T.astype(_BF16)


def _inproj(u, w_in, b_gate, tm, conv_dim, attn_dim, mem_dim, attn_copies):
    nb, s, d = u.shape
    n_gate = b_gate.shape[1]
    tok = lambda b, i: (b, i, 0)
    out_specs = [pl.BlockSpec((1, tm, conv_dim), tok), pl.BlockSpec((1, tm, attn_dim), tok),
                 pl.BlockSpec((1, tm, attn_dim), tok), pl.BlockSpec((1, tm, attn_dim), tok),
                 pl.BlockSpec((1, tm, mem_dim), tok), pl.BlockSpec((1, tm, n_gate), tok)]
    out_shape = [jax.ShapeDtypeStruct((nb, s, conv_dim), _F32), jax.ShapeDtypeStruct((nb, s, attn_dim), _BF16),
                 jax.ShapeDtypeStruct((nb, s, attn_dim), _F32), jax.ShapeDtypeStruct((nb, s, attn_dim), _F32),
                 jax.ShapeDtypeStruct((nb, s, mem_dim), _BF16), jax.ShapeDtypeStruct((nb, s, n_gate), _BF16)]
    if attn_copies:
        out_specs += [pl.BlockSpec((1, tm, attn_dim), tok), pl.BlockSpec((1, attn_dim, tm), lambda b, i: (b, 0, i))]
        out_shape += [jax.ShapeDtypeStruct((nb, s, attn_dim), _BF16), jax.ShapeDtypeStruct((nb, attn_dim, s), _BF16)]
    return pl.pallas_call(
        functools.partial(_inproj_kernel, conv_dim=conv_dim, attn_dim=attn_dim, mem_dim=mem_dim),
        grid=(nb, s // tm),
        in_specs=[pl.BlockSpec((1, tm, d), tok), _resident(w_in.shape), _resident((1, n_gate))],
        out_specs=out_specs, out_shape=out_shape,
        compiler_params=_params(2),
        name="inproj",
    )(u, w_in, b_gate)


def _memkv_kernel(mem_ref, g_ref, w_ref, mk_ref, mv_ref, mkb_ref, mvb_ref):
    kv = _dot(_rms_norm(mem_ref[0], g_ref[...]).astype(_BF16), w_ref[...])
    half = kv.shape[1] // 2
    mk_ref[0] = kv[:, :half]
    mv_ref[0] = kv[:, half:]
    mkb_ref[0] = kv[:, :half].astype(_BF16)
    mvb_ref[0] = kv[:, half:].astype(_BF16)


def _memkv(mem, g_mem, w_mem_kv):
    nb, m, d = mem.shape
    md = w_mem_kv.shape[1] // 2
    blk = lambda b: (b, 0, 0)
    return pl.pallas_call(
        _memkv_kernel,
        grid=(nb,),
        in_specs=[pl.BlockSpec((1, m, d), blk), _resident((1, d)), _resident(w_mem_kv.shape)],
        out_specs=[pl.BlockSpec((1, m, md), blk)] * 4,
        out_shape=[jax.ShapeDtypeStruct((nb, m, md), _F32)] * 2 + [jax.ShapeDtypeStruct((nb, m, md), _BF16)] * 2,
        compiler_params=_params(1),
        name="memkv",
    )(mem, g_mem, w_mem_kv)


def _ln_swish(c, g, b):
    mu = jnp.mean(c, axis=-1, keepdims=True)
    cc = c - mu
    var = jnp.mean(cc * cc, axis=-1, keepdims=True)
    y = cc * lax.rsqrt(var + _NORM_EPS) * g + b
    return y * _sigmoid(y)


def _conv_kernel(main_ref, halo_ref, wd_ref, bd_ref, g_ref, b_ref, out_ref, ext_ref, *, rows):
    tt = main_ref.shape[1]
    ch = main_ref.shape[2]
    ext_ref[0:_HALO_ROWS, :] = jnp.where(pl.program_id(1) > 0, halo_ref[0], 0.0)
    ext_ref[_HALO_ROWS:, :] = main_ref[0]
    lead = _HALO_ROWS - (_CONV_WIDTH - 1)

    def chunk(i, carry):
        r0 = pl.multiple_of(i * rows, rows)
        cols = []
        for lb in range(ch // _LANES):
            ls = slice(lb * _LANES, (lb + 1) * _LANES)
            win = ext_ref[pl.ds(r0, rows + _HALO_ROWS), ls]
            acc = jnp.broadcast_to(bd_ref[:, ls], (rows, _LANES))
            for r in range(_SUBLANES):
                n_a = (_CONV_WIDTH - r + _SUBLANES - 1) // _SUBLANES
                sh = win[lead + r:lead + r + rows + _SUBLANES * (n_a - 1)]
                for a in range(n_a):
                    w = _SUBLANES * a + r
                    acc = acc + sh[_SUBLANES * a:_SUBLANES * a + rows] * wd_ref[w:w + 1, ls]
            cols.append(acc)
        c = jnp.concatenate(cols, axis=-1)
        out_ref[0, pl.ds(r0, rows), :] = _ln_swish(c, g_ref[...], b_ref[...]).astype(_BF16)
        return carry

    lax.fori_loop(0, tt // rows, chunk, 0)


def _conv_prompt(glu, w_dw, b_dw, g_ln, b_ln, tt):
    nb, s, ch = glu.shape
    per = tt // _HALO_ROWS
    return pl.pallas_call(
        functools.partial(_conv_kernel, rows=32),
        grid=(nb, s // tt),
        in_specs=[pl.BlockSpec((1, tt, ch), lambda b, t: (b, t, 0)),
                  pl.BlockSpec((1, _HALO_ROWS, ch), lambda b, t: (b, jnp.maximum(t * per - 1, 0), 0)),
                  _resident(w_dw.shape), _resident((1, ch)), _resident((1, ch)), _resident((1, ch))],
        out_specs=pl.BlockSpec((1, tt, ch), lambda b, t: (b, t, 0)),
        out_shape=jax.ShapeDtypeStruct((nb, s, ch), _BF16),
        scratch_shapes=[pltpu.VMEM((tt + _HALO_ROWS, ch), _F32)],
        compiler_params=_params(2),
        name="conv_prompt",
    )(glu, glu, w_dw, b_dw, g_ln, b_ln)


def _conv_step_kernel(ext_ref, wd_ref, bd_ref, g_ref, b_ref, out_ref):
    n_q = out_ref.shape[0]
    for t in range(n_q):
        acc = jnp.broadcast_to(bd_ref[...], ext_ref.shape[1:])
        for w in range(_CONV_WIDTH):
            acc = acc + ext_ref[t + w] * wd_ref[w:w + 1, :]
        out_ref[t] = _ln_swish(acc, g_ref[...], b_ref[...]).astype(_BF16)


def _conv_step(ext_tm, w_dw, b_dw, g_ln, b_ln):
    n_rows, n_seq, ch = ext_tm.shape
    n_q = n_rows - (_CONV_WIDTH - 1)
    return pl.pallas_call(
        _conv_step_kernel,
        grid=(1,),
        in_specs=[_resident(ext_tm.shape), _resident(w_dw.shape), _resident((1, ch)), _resident((1, ch)),
                  _resident((1, ch))],
        out_specs=pl.BlockSpec((n_q, n_seq, ch), lambda i: (0, 0, 0)),
        out_shape=jax.ShapeDtypeStruct((n_q, n_seq, ch), _BF16),
        compiler_params=_params(1),
        name="conv_step",
    )(ext_tm, w_dw, b_dw, g_ln, b_ln)


def _rel_bucket(dist):
    n = jnp.maximum(dist, 0)
    max_exact = _N_BUCKETS // 2
    nf = jnp.maximum(n, 1).astype(_F32)
    large = max_exact + (jnp.log(nf / max_exact) / math.log(_MAX_DISTANCE / max_exact)
                         * (_N_BUCKETS - max_exact)).astype(jnp.int32)
    large = jnp.minimum(large, _N_BUCKETS - 1)
    return jnp.where(n < max_exact, n, large)


def _bias_of_dist(table, dist, per_row=False):
    n_col = table.shape[1]
    bucket = jnp.where(dist >= 0, _rel_bucket(dist), -1)
    if per_row:
        col_shape = (n_col,) + (1,) * (dist.ndim - 1)
    else:
        bucket = bucket[None]
        col_shape = (n_col,) + (1,) * dist.ndim
    out = jnp.full(jnp.broadcast_shapes(bucket.shape, col_shape), _NEG_INF, _F32)
    for b in range(_N_BUCKETS):
        out = jnp.where(bucket == b, table[b].reshape(col_shape), out)
    return out


def _topk_block_mask(score, blk, n_valid):
    rank = jnp.zeros(score.shape, jnp.int32)
    for m in range(score.shape[0]):
        sm = score[m:m + 1, :]
        counts = jnp.where(m < n_valid, 1, 0)
        rank = rank + jnp.where(sm > score, counts, 0) + jnp.where(sm == score, jnp.where(m < blk, counts, 0), 0)
    return (blk < n_valid) & (rank < _MOBA_TOPK)


def _moba_prompt_kernel(far_ref, q_ref, k_ref, vt_ref, bias_ref, o_ref, kmean_ref, pen_ref, *, n_blk):
    blk_len = _MOBA_BLOCK
    hp = pl.program_id(1)
    own = pl.program_id(2)
    half = _HEAD_DIM

    @pl.when(own == 0)
    def _():
        for j in range(n_blk):
            kj = k_ref[0, j * blk_len:(j + 1) * blk_len, :].astype(_F32)
            kmean_ref[j:j + 1, :] = jnp.mean(kj, axis=0, keepdims=True)

    qf = q_ref[0].astype(_F32)
    lane = lax.broadcasted_iota(jnp.int32, qf.shape, 1)
    blk = lax.broadcasted_iota(jnp.int32, (n_blk, blk_len), 0)
    kmean = kmean_ref[...].astype(_BF16)
    prev = jnp.maximum(own - 1, 0)

    qhs, adj_pens = [], []
    for hd in range(2):
        qh = jnp.where((lane >= hd * half) & (lane < (hd + 1) * half), qf, 0.0).astype(_BF16)
        sel = _topk_block_mask(_nt_dot(kmean, qh), blk, own)
        pen_ref[hd] = jnp.where(sel & (blk < prev), far_ref[hp * 2 + hd], _NEG_INF)
        sel_adj = jnp.sum(jnp.where(sel & (blk == prev), 1.0, 0.0), axis=0, keepdims=True)
        adj_pens.append(jnp.where(own == 0, 0.0, jnp.where(sel_adj > 0.5, 0.0, _NEG_INF)))
        qhs.append(qh)

    def attend(start_a, start_b, adds, carry):
        new = []
        for hd in range(2):
            add_a, add_b = adds[hd]
            s_a = _nt_dot(k_ref[0, pl.ds(start_a, blk_len), :], qhs[hd]) + add_a
            s_b = _nt_dot(k_ref[0, pl.ds(start_b, blk_len), :], qhs[hd]) + add_b
            m_new = jnp.maximum(jnp.max(s_a, axis=0, keepdims=True), jnp.max(s_b, axis=0, keepdims=True))
            if carry is not None:
                m_old, l_old, acc_old = carry[hd]
                m_new = jnp.maximum(m_new, m_old)
            p_a = jnp.exp(s_a - m_new)
            p_b = jnp.exp(s_b - m_new)
            l = jnp.sum(p_a, axis=0, keepdims=True) + jnp.sum(p_b, axis=0, keepdims=True)
            rows = slice(hd * half, (hd + 1) * half)
            acc = (_dot(vt_ref[0, rows, pl.ds(start_a, blk_len)], p_a.astype(_BF16))
                   + _dot(vt_ref[0, rows, pl.ds(start_b, blk_len)], p_b.astype(_BF16)))
            if carry is not None:
                alpha = jnp.exp(m_old - m_new)
                l = alpha * l_old + l
                acc = alpha * acc_old + acc
            new.append((m_new, l, acc))
        return tuple(new)

    kind_a = jnp.where(own == 0, 0, 1)
    mask_b = jnp.where(own == 0, _NEG_INF, 0.0)
    adds = [(bias_ref[hd, kind_a] + adj_pens[hd], bias_ref[hd, 0] + mask_b) for hd in range(2)]
    carry = attend(pl.multiple_of(prev * blk_len, blk_len),
                   pl.multiple_of(jnp.maximum(own, 1) * blk_len, blk_len), adds, None)

    def far_pair(t, carry):
        adds = [(pen_ref[hd, pl.ds(2 * t, 1), :], pen_ref[hd, pl.ds(2 * t + 1, 1), :]) for hd in range(2)]
        start = pl.multiple_of(2 * t * blk_len, 2 * blk_len)
        return attend(start, start + blk_len, adds, carry)

    carry = lax.fori_loop(0, (prev + 1) // 2, far_pair, carry)
    outs = [acc * (1.0 / l) for (_, l, acc) in carry]
    o_ref[0] = jnp.concatenate(outs, axis=0).T.astype(_BF16)


def _moba_prompt(q, kb, vt, rel_bias):
    nb, s, attn_dim = q.shape
    n_blk = s // _MOBA_BLOCK
    assert n_blk >= 2
    n_pair = attn_dim // _LANES
    kk = jnp.arange(_MOBA_BLOCK, dtype=jnp.int32)[:, None]
    qq = jnp.arange(_MOBA_BLOCK, dtype=jnp.int32)[None, :]
    table = rel_bias.astype(_F32)
    tiles = _bias_of_dist(table, jnp.stack([qq - kk, _MOBA_BLOCK + qq - kk]))
    far = _bias_of_dist(table, jnp.full((1,), _MOBA_BLOCK + 1, jnp.int32))[:, 0]
    return pl.pallas_call(
        functools.partial(_moba_prompt_kernel, n_blk=n_blk),
        grid=(nb, n_pair, n_blk),
        in_specs=[pl.BlockSpec(memory_space=pltpu.SMEM),
                  pl.BlockSpec((1, _MOBA_BLOCK, _LANES), lambda b, h, c: (b, c, h)),
                  pl.BlockSpec((1, s, _LANES), lambda b, h, c: (b, 0, h)),
                  pl.BlockSpec((1, _LANES, s), lambda b, h, c: (b, h, 0)),
                  pl.BlockSpec((2, 2, _MOBA_BLOCK, _MOBA_BLOCK), lambda b, h, c: (h, 0, 0, 0))],
        out_specs=pl.BlockSpec((1, _MOBA_BLOCK, _LANES), lambda b, h, c: (b, c, h)),
        out_shape=jax.ShapeDtypeStruct((nb, s, attn_dim), _BF16),
        scratch_shapes=[pltpu.VMEM((n_blk, _LANES), _F32), pltpu.VMEM((2, n_blk, _MOBA_BLOCK), _F32)],
        compiler_params=_params(3),
        name="moba_prompt",
    )(far, q, kb, vt, tiles)


def _moba_step_kernel(pt_ref, qbd_ref, knew_ref, vnew_ref, badj_ref, bfar_ref, bown_ref, *rest, n_pages, n_q):
    del pt_ref
    kt_pages = rest[:n_pages]
    vt_pages = rest[n_pages:2 * n_pages]
    o_ref = rest[2 * n_pages]
    ppb = _MOBA_BLOCK // _PAGE_SIZE
    n_blk = n_pages // ppb
    qbd = qbd_ref[0]
    feat = qbd.shape[1]

    def block_of(pages, j):
        return jnp.concatenate([pages[p][0] for p in range(j * ppb, (j + 1) * ppb)], axis=1).astype(_BF16)

    s_blk = [_dot(qbd, block_of(kt_pages, j)) for j in range(n_blk)]
    score = [jnp.sum(s, axis=1, keepdims=True) for s in s_blk]
    logits = []
    for n in range(n_blk):
        rank = jnp.zeros(score[n].shape, jnp.int32)
        for m in range(n_blk):
            if m < n:
                rank = rank + jnp.where(score[m] >= score[n], 1, 0)
            elif m > n:
                rank = rank + jnp.where(score[m] > score[n], 1, 0)
        bias = badj_ref[...] if n == n_blk - 1 else bfar_ref[:, 0:1]
        logits.append(jnp.where(rank < _MOBA_TOPK, s_blk[n] + bias, _NEG_INF))
    s_own = _nt_dot(qbd, knew_ref[0].astype(_BF16)) + bown_ref[:, 0:_NEW_ROWS]
    m_row = jnp.max(s_own, axis=1, keepdims=True)
    for lg in logits:
        m_row = jnp.maximum(m_row, jnp.max(lg, axis=1, keepdims=True))
    p_own = jnp.exp(s_own - m_row)
    probs = [jnp.exp(lg - m_row) for lg in logits]
    denom = jnp.sum(p_own, axis=1, keepdims=True)
    for p in probs:
        denom = denom + jnp.sum(p, axis=1, keepdims=True)
    inv = 1.0 / denom
    out = _dot((p_own * inv).astype(_BF16), vnew_ref[0].astype(_BF16))
    for j in range(n_blk):
        out = out + _nt_dot((probs[j] * inv).astype(_BF16), block_of(vt_pages, j))
    r_head = lax.broadcasted_iota(jnp.int32, out.shape, 0) % _HEAD_GROUP
    c_head = lax.broadcasted_iota(jnp.int32, out.shape, 1) // _HEAD_DIM
    out = jnp.where(r_head == c_head, out, 0.0)
    o_ref[0] = jnp.sum(out.reshape(n_q, _HEAD_GROUP, feat), axis=1)


def _block_diag_queries(q, n_heads, head_dim, group):
    n, n_q, feat = q.shape
    r_head = jnp.arange(group, dtype=jnp.int32)[:, None]
    c_head = jnp.arange(feat, dtype=jnp.int32)[None, :] // head_dim
    mask = (r_head == c_head) & (r_head < n_heads)
    return jnp.where(mask[None, None], q[:, :, None, :], jnp.zeros((), q.dtype)).reshape(n, n_q * group, feat)


def _moba_step(page_table, q, k_new, v_new, cache_kt, cache_vt, rel_bias):
    n_seq, n_q, feat = q.shape
    n_pages = page_table.shape[1]
    past = n_pages * _PAGE_SIZE
    assert past % _MOBA_BLOCK == 0 and n_q <= _NEW_ROWS
    rows = n_q * _HEAD_GROUP
    qbd = _block_diag_queries(q, _N_HEADS, _HEAD_DIM, _HEAD_GROUP)
    padn = ((0, 0), (0, _NEW_ROWS - n_q), (0, 0))
    k_new = jnp.pad(k_new, padn)
    v_new = jnp.pad(v_new, padn)
    table = jnp.tile(jnp.pad(rel_bias.astype(_F32), ((0, 0), (0, _HEAD_GROUP - _N_HEADS))), (1, n_q))
    qi = jnp.arange(rows, dtype=jnp.int32)[:, None] // _HEAD_GROUP
    kk = jnp.arange(_MOBA_BLOCK, dtype=jnp.int32)[None, :]
    badj = _bias_of_dist(table, _MOBA_BLOCK + qi - kk, per_row=True)
    bfar = _bias_of_dist(table, jnp.full((rows, _LANES), _MOBA_BLOCK + 1, jnp.int32), per_row=True)
    kj = jnp.arange(_LANES, dtype=jnp.int32)[None, :]
    bown = _bias_of_dist(table, jnp.where(kj < n_q, qi - kj, -1), per_row=True)

    page = lambda p: pl.BlockSpec((1, feat, _PAGE_SIZE), lambda i, pt, p=p: (pt[i, p], 0, 0))
    seq3 = lambda r: pl.BlockSpec((1, r, feat), lambda i, pt: (i, 0, 0))
    const2 = lambda shape: pl.BlockSpec(shape, lambda i, pt: (0, 0), pipeline_mode=pl.Buffered(1))
    grid_spec = pltpu.PrefetchScalarGridSpec(
        num_scalar_prefetch=1,
        grid=(n_seq,),
        in_specs=[seq3(rows), seq3(_NEW_ROWS), seq3(_NEW_ROWS), const2((rows, _MOBA_BLOCK)),
                  const2((rows, _LANES)), const2((rows, _LANES))] + [page(p) for p in range(n_pages)] * 2,
        out_specs=pl.BlockSpec((1, n_q, feat), lambda i, pt: (i, 0, 0)))
    return pl.pallas_call(
        functools.partial(_moba_step_kernel, n_pages=n_pages, n_q=n_q),
        grid_spec=grid_spec,
        out_shape=jax.ShapeDtypeStruct((n_seq, n_q, feat), _F32),
        compiler_params=_params(1),
        name="moba_step",
    )(page_table, qbd, k_new, v_new, badj, bfar, bown, *([cache_kt] * n_pages), *([cache_vt] * n_pages))


def _softmax_rows(s):
    p = jnp.exp(s - jnp.max(s, axis=1, keepdims=True))
    return p * (1.0 / jnp.sum(p, axis=1, keepdims=True))


def _memattn_kernel(qm_ref, mk_ref, mv_ref, o_ref):
    scale = _MEM_HEAD_DIM ** -0.5
    for h in range(_MEM_HEADS):
        hs = slice(h * _MEM_HEAD_DIM, (h + 1) * _MEM_HEAD_DIM)
        p = _softmax_rows(_nt_dot(qm_ref[0, :, hs], mk_ref[0, :, hs]) * scale)
        o_ref[0, :, hs] = _dot(p.astype(_BF16), mv_ref[0, :, hs]).astype(_BF16)


def _memattn(qm, mkb, mvb, tq):
    nb, s, md = qm.shape
    m = mkb.shape[1]
    return pl.pallas_call(
        _memattn_kernel,
        grid=(nb, s // tq),
        in_specs=[pl.BlockSpec((1, tq, md), lambda b, i: (b, i, 0)),
                  pl.BlockSpec((1, m, md), lambda b, i: (b, 0, 0)),
                  pl.BlockSpec((1, m, md), lambda b, i: (b, 0, 0))],
        out_specs=pl.BlockSpec((1, tq, md), lambda b, i: (b, i, 0)),
        out_shape=jax.ShapeDtypeStruct((nb, s, md), _BF16),
        compiler_params=_params(2),
        name="memattn",
    )(qm, mkb, mvb)


def _memattn_step_kernel(q_ref, mk_ref, mv_ref, o_ref):
    scale = _MEM_HEAD_DIM ** -0.5
    for i in range(mk_ref.shape[0]):
        s = _nt_dot(q_ref[i], mk_ref[i].astype(_BF16)) * scale
        r_head = lax.broadcasted_iota(jnp.int32, s.shape, 0) % _MEM_HEADS
        c_head = lax.broadcasted_iota(jnp.int32, s.shape, 1) % _MEM_HEADS
        p = _softmax_rows(jnp.where(r_head == c_head, s, _NEG_INF))
        o_ref[i] = _dot(p.astype(_BF16), mv_ref[i].astype(_BF16))


def _memattn_step(qm, mem_k, mem_v, seqs_per_step):
    n_seq, rows, dh = qm.shape
    mh = mem_k.shape[1]
    blk = lambda i: (i, 0, 0)
    return pl.pallas_call(
        _memattn_step_kernel,
        grid=(n_seq // seqs_per_step,),
        in_specs=[pl.BlockSpec((seqs_per_step, rows, dh), blk), pl.BlockSpec((seqs_per_step, mh, dh), blk),
                  pl.BlockSpec((seqs_per_step, mh, dh), blk)],
        out_specs=pl.BlockSpec((seqs_per_step, rows, dh), blk),
        out_shape=jax.ShapeDtypeStruct((n_seq, rows, dh), _F32),
        compiler_params=_params(1),
        name="memattn_step",
    )(qm, mem_k, mem_v)


def _merge_kernel(h_ref, c_ref, o_ref, om_ref, gate_ref, wc_ref, wa_ref, wm_ref, wo_ref, gpost_ref,
                  g2pre_ref, w2in_ref, w2out_ref, g2post_ref, y_ref, *, n_chunk):
    d = h_ref.shape[1]
    merged = (gate_ref[:, 0:d].astype(_F32) * _dot(c_ref[...], wc_ref[...])
              + gate_ref[:, d:2 * d].astype(_F32) * _dot(o_ref[...], wa_ref[...])
              + gate_ref[:, 2 * d:3 * d].astype(_F32) * _dot(om_ref[...], wm_ref[...]))
    h = h_ref[...] + _rms_norm(_dot(merged.astype(_BF16), wo_ref[...]), gpost_ref[...])
    y_ref[...] = _ffn_half_step(h, g2pre_ref[...], w2in_ref, w2out_ref, g2post_ref[...], n_chunk)


def _merge(h, c_act, o, om, gates, w_conv_out, w_attn_out, w_mem_out, w_out, g_post, g2_pre, w2_in, w2_out,
           g2_post, tm):
    n, d = h.shape
    row = lambda i: (i, 0)
    tile = lambda a: pl.BlockSpec((tm, a.shape[1]), row)
    return pl.pallas_call(
        functools.partial(_merge_kernel, n_chunk=_ffn_chunks(w2_out.shape[0])),
        grid=(n // tm,),
        in_specs=[tile(h), tile(c_act), tile(o), tile(om), tile(gates),
                  _resident(w_conv_out.shape), _resident(w_attn_out.shape), _resident(w_mem_out.shape),
                  _resident(w_out.shape), _resident((1, d)), _resident((1, d)), _resident(w2_in.shape),
                  _resident(w2_out.shape), _resident((1, d))],
        out_specs=pl.BlockSpec((tm, d), row),
        out_shape=jax.ShapeDtypeStruct((n, d), _F32),
        compiler_params=_params(1),
        name="merge_ffn2",
    )(h, c_act, o, om, gates, w_conv_out, w_attn_out, w_mem_out, w_out, g_post, g2_pre, w2_in, w2_out, g2_post)


def _tile(n, want):
    t = min(n, want)
    assert n % t == 0
    return t


def _token_front(x, p):
    nb, s, d = x.shape
    n = nb * s
    h, u = _ffn1(x.reshape(n, d), p['g_ffn1_pre'], p['w_ffn1_in'], p['w_ffn1_out'], p['g_ffn1_post'],
                 p['g_mix_pre'], _tile(n, 512))
    return h, u.reshape(nb, s, d)


def _token_back(h, c_act, o, om, gates, p):
    n = h.shape[0]
    flat = lambda a: a.reshape(n, a.shape[-1])
    return _merge(h, flat(c_act), flat(o), flat(om), flat(gates), p['w_conv_out'], p['w_attn_out'],
                  p['w_mem_out'], p['w_out'], p['g_mix_post'], p['g_ffn2_pre'], p['w_ffn2_in'], p['w_ffn2_out'],
                  p['g_ffn2_post'], _tile(n, 256))


def _dims(p):
    conv_dim = p['w_conv_out'].shape[0]
    attn_dim = p['w_attn_out'].shape[0]
    mem_dim = p['w_mem_out'].shape[0]
    return conv_dim, attn_dim, mem_dim


def _prompt_layer(x, mem, rel_bias, p):
    nb, s, d = x.shape
    conv_dim, attn_dim, mem_dim = _dims(p)
    h, u = _token_front(x, p)
    glu, q, k, v, qm, gates, kb, vt = _inproj(u, p['w_in'], p['b_gate'], _tile(s, 512), conv_dim, attn_dim,
                                              mem_dim, True)
    mk, mv, mkb, mvb = _memkv(mem, p['g_mem'], p['w_mem_kv'])
    c_act = _conv_prompt(glu, p['w_dwconv'], p['b_dwconv'], p['g_conv_ln'], p['b_conv_ln'], _tile(s, 512))
    o = _moba_prompt(q, kb, vt, rel_bias)
    om = _memattn(qm, mkb, mvb, _tile(s, 512))
    y = _token_back(h, c_act, o, om, gates, p).reshape(nb, s, d)
    conv_state = glu[:, s - (_CONV_WIDTH - 1):]
    heads = lambda a: a.reshape(nb, s, _N_HEADS, _HEAD_DIM)
    mheads = lambda a: a.reshape(nb, a.shape[1], _MEM_HEADS, _MEM_HEAD_DIM)
    return y, heads(k), heads(v), mheads(mk), mheads(mv), conv_state


def _sample_layer(x, conv_prev, cache_k, cache_v, page_table, mem_k, mem_v, rel_bias, p):
    n_seq, n_q, d = x.shape
    conv_dim, attn_dim, mem_dim = _dims(p)
    n = n_seq * n_q
    h, u = _token_front(x.reshape(1, n, d), p)
    glu, q, k, v, qm, gates = _inproj(u, p['w_in'], p['b_gate'], _tile(n, 512), conv_dim, attn_dim, mem_dim,
                                      False)
    per_seq = lambda a: a.reshape(n_seq, n_q, a.shape[-1])
    glu = per_seq(glu)
    conv_in = jnp.concatenate([conv_prev, glu], axis=1)
    c_act = _conv_step(conv_in.transpose(1, 0, 2), p['w_dwconv'], p['b_dwconv'], p['g_conv_ln'], p['b_conv_ln'])
    c_act = c_act.transpose(1, 0, 2)
    page_t = lambda c: c.transpose(0, 2, 3, 1).reshape(c.shape[0], attn_dim, c.shape[1])
    o = _moba_step(page_table, per_seq(q), per_seq(k), per_seq(v), page_t(cache_k), page_t(cache_v), rel_bias)
    head_rows = lambda a: a.reshape(n_seq, -1, _MEM_HEAD_DIM)
    om = _memattn_step(head_rows(qm), head_rows(mem_k), head_rows(mem_v), 8)
    y = _token_back(h, c_act, o.astype(_BF16), om.reshape(n, mem_dim).astype(_BF16), gates, p)
    conv_state = conv_in[:, n_q:]
    heads = lambda a: a.reshape(n_seq, n_q, _N_HEADS, _HEAD_DIM)
    return y.reshape(n_seq, n_q, d), heads(k), heads(v), conv_state


def _layer_params(l, g_ffn1_pre, w_ffn1_in, w_ffn1_out, g_ffn1_post, g_mix_pre, w_in, b_gate, w_dwconv, b_dwconv,
                  g_conv_ln, b_conv_ln, w_conv_out, w_attn_out, g_mem, w_mem_kv, w_mem_out, w_out, g_mix_post,
                  g_ffn2_pre, w_ffn2_in, w_ffn2_out, g_ffn2_post):
    vec = lambda a: a[l].astype(_F32).reshape(1, -1)
    mat = lambda a: a[l].astype(_BF16)
    return dict(
        g_ffn1_pre=vec(g_ffn1_pre), w_ffn1_in=mat(w_ffn1_in), w_ffn1_out=mat(w_ffn1_out),
        g_ffn1_post=vec(g_ffn1_post), g_mix_pre=vec(g_mix_pre), w_in=mat(w_in), b_gate=vec(b_gate),
        w_dwconv=w_dwconv[l].astype(_F32), b_dwconv=vec(b_dwconv), g_conv_ln=vec(g_conv_ln),
        b_conv_ln=vec(b_conv_ln), w_conv_out=mat(w_conv_out), w_attn_out=mat(w_attn_out), g_mem=vec(g_mem),
        w_mem_kv=mat(w_mem_kv), w_mem_out=mat(w_mem_out), w_out=mat(w_out), g_mix_post=vec(g_mix_post),
        g_ffn2_pre=vec(g_ffn2_pre), w_ffn2_in=mat(w_ffn2_in), w_ffn2_out=mat(w_ffn2_out),
        g_ffn2_post=vec(g_ffn2_post))


def kernel(x_prompt, x_sample, cache_k, cache_v, cache_mem_k, cache_mem_v, state_conv, page_table, mem_prompt, g_ffn1_pre, w_ffn1_in, w_ffn1_out, g_ffn1_post, g_mix_pre, w_in, b_gate, w_dwconv, b_dwconv, g_conv_ln, b_conv_ln, w_conv_out, w_attn_out, rel_bias, g_mem, w_mem_kv, w_mem_out, w_out, g_mix_post, g_ffn2_pre, w_ffn2_in, w_ffn2_out, g_ffn2_post):
    depth = w_in.shape[0]
    hp, hs = x_prompt, x_sample
    outs = [[] for _ in range(8)]
    for l in range(depth):
        p = _layer_params(l, g_ffn1_pre, w_ffn1_in, w_ffn1_out, g_ffn1_post, g_mix_pre, w_in, b_gate, w_dwconv,
                          b_dwconv, g_conv_ln, b_conv_ln, w_conv_out, w_attn_out, g_mem, w_mem_kv, w_mem_out,
                          w_out, g_mix_post, g_ffn2_pre, w_ffn2_in, w_ffn2_out, g_ffn2_post)
        hp, kp, vp, mkp, mvp, cp = _prompt_layer(hp, mem_prompt, rel_bias, p)
        hs, ks, vs, cs = _sample_layer(hs, state_conv[l], cache_k[l], cache_v[l], page_table, cache_mem_k[l],
                                       cache_mem_v[l], rel_bias, p)
        for acc, val in zip(outs, (kp, vp, mkp, mvp, cp, ks, vs, cs)):
            acc.append(val)
    return (hp, hs) + tuple(jnp.stack(a) for a in outs)
```

```python
import functools
import math

import jax
import jax.numpy as jnp
from jax import lax
from jax.experimental import pallas as pl
from jax.experimental.pallas import tpu as pltpu

_F32 = jnp.float32
_BF16 = jnp.bfloat16

_CONV_WIDTH = 31
_N_HEADS = 12
_HEAD_DIM = 64
_MOBA_BLOCK = 256
_MOBA_TOPK = 3
_MEM_HEADS = 4
_MEM_HEAD_DIM = 128
_N_BUCKETS = 32
_MAX_DISTANCE = 128
_N_BRANCH = 3
_PAGE_SIZE = 128
_NORM_EPS = 1e-6
_NEG_INF = -1e30

_LANES = 128
_SUBLANES = 8
_VMEM_BYTES_V7X = 64 * 1024 * 1024
_VMEM_LIMIT = _VMEM_BYTES_V7X - 8 * 1024 * 1024

_HALO_ROWS = 32
_HEAD_GROUP = 16
_NEW_ROWS = 16


def _nt_dot(a, b):
    return lax.dot_general(a, b, (((1,), (1,)), ((), ())), preferred_element_type=_F32)


def _dot(a, b):
    return jnp.dot(a, b, preferred_element_type=_F32)


def _sigmoid(x):
    return 1.0 / (1.0 + jnp.exp(-x))


def _rms_norm(x, g):
    return x * lax.rsqrt(jnp.mean(x * x, axis=-1, keepdims=True) + _NORM_EPS) * g


def _resident(shape):
    nd = len(shape)
    return pl.BlockSpec(shape, lambda *_: (0,) * nd, pipeline_mode=pl.Buffered(1))


def _params(n_parallel_axes):
    return pltpu.CompilerParams(
        dimension_semantics=("arbitrary",) * n_parallel_axes,
        vmem_limit_bytes=_VMEM_LIMIT)


def _ffn_half_step(x, g_pre, w_in_ref, w_out_ref, g_post, n_chunk):
    d_ff = w_out_ref.shape[0]
    fc = d_ff // n_chunk
    xn = _rms_norm(x, g_pre).astype(_BF16)
    acc = None
    for c in range(n_chunk):
        a = _dot(xn, w_in_ref[:, c * fc:(c + 1) * fc])
        b = _dot(xn, w_in_ref[:, d_ff + c * fc:d_ff + (c + 1) * fc])
        hid = (a * _sigmoid(a) * b).astype(_BF16)
        part = _dot(hid, w_out_ref[c * fc:(c + 1) * fc, :])
        acc = part if acc is None else acc + part
    return x + 0.5 * _rms_norm(acc, g_post)


def _ffn1_kernel(x_ref, gpre_ref, win_ref, wout_ref, gpost_ref, gmix_ref, h_ref, u_ref, *, n_chunk):
    h = _ffn_half_step(x_ref[...], gpre_ref[...], win_ref, wout_ref, gpost_ref[...], n_chunk)
    h_ref[...] = h
    u_ref[...] = _rms_norm(h, gmix_ref[...]).astype(_BF16)


def _ffn_chunks(d_ff):
    for n in (2, 1):
        if d_ff % (n * _LANES) == 0:
            return n
    return 1


def _ffn1(x, g_pre, w_in, w_out, g_post, g_mix, tm):
    n, d = x.shape
    d_ff = w_out.shape[0]
    row = lambda i: (i, 0)
    return pl.pallas_call(
        functools.partial(_ffn1_kernel, n_chunk=_ffn_chunks(d_ff)),
        grid=(n // tm,),
        in_specs=[pl.BlockSpec((tm, d), row), _resident((1, d)), _resident(w_in.shape),
                  _resident(w_out.shape), _resident((1, d)), _resident((1, d))],
        out_specs=[pl.BlockSpec((tm, d), row), pl.BlockSpec((tm, d), row)],
        out_shape=[jax.ShapeDtypeStruct((n, d), _F32), jax.ShapeDtypeStruct((n, d), _BF16)],
        compiler_params=_params(1),
        name="ffn1",
    )(x, g_pre, w_in, w_out, g_post, g_mix)


def _inproj_kernel(u_ref, w_ref, bg_ref, conv_ref, q_ref, k_ref, v_ref, qm_ref, gate_ref, *rest,
                   conv_dim, attn_dim, mem_dim):
    u = u_ref[0]
    o1 = 2 * conv_dim
    o2 = o1 + attn_dim
    o3 = o2 + attn_dim
    o4 = o3 + attn_dim
    o5 = o4 + mem_dim

    def proj(lo, hi):
        return _dot(u, w_ref[:, lo:hi])

    conv_ref[0] = proj(0, conv_dim) * _sigmoid(proj(conv_dim, o1))
    q_ref[0] = (proj(o1, o2) * (_HEAD_DIM ** -0.5)).astype(_BF16)
    k = proj(o2, o3)
    v = proj(o3, o4)
    k_ref[0] = k
    v_ref[0] = v
    qm_ref[0] = proj(o4, o5).astype(_BF16)
    gate_ref[0] = _sigmoid(proj(o5, w_ref.shape[1]) + bg_ref[...]).astype(_BF16)
    if rest:
        kb_ref, vt_ref = rest
        kb_ref[0] = k.astype(_BF16)
        vt_ref[0] = v.T.astype(_BF16)


def _inproj(u, w_in, b_gate, tm, conv_dim, attn_dim, mem_dim, attn_copies):
    nb, s, d = u.shape
    n_gate = b_gate.shape[1]
    tok = lambda b, i: (b, i, 0)
    out_specs = [pl.BlockSpec((1, tm, conv_dim), tok), pl.BlockSpec((1, tm, attn_dim), tok),
                 pl.BlockSpec((1, tm, attn_dim), tok), pl.BlockSpec((1, tm, attn_dim), tok),
                 pl.BlockSpec((1, tm, mem_dim), tok), pl.BlockSpec((1, tm, n_gate), tok)]
    out_shape = [jax.ShapeDtypeStruct((nb, s, conv_dim), _F32), jax.ShapeDtypeStruct((nb, s, attn_dim), _BF16),
                 jax.ShapeDtypeStruct((nb, s, attn_dim), _F32), jax.ShapeDtypeStruct((nb, s, attn_dim), _F32),
                 jax.ShapeDtypeStruct((nb, s, mem_dim), _BF16), jax.ShapeDtypeStruct((nb, s, n_gate), _BF16)]
    if attn_copies:
        out_specs += [pl.BlockSpec((1, tm, attn_dim), tok), pl.BlockSpec((1, attn_dim, tm), lambda b, i: (b, 0, i))]
        out_shape += [jax.ShapeDtypeStruct((nb, s, attn_dim), _BF16), jax.ShapeDtypeStruct((nb, attn_dim, s), _BF16)]
    return pl.pallas_call(
        functools.partial(_inproj_kernel, conv_dim=conv_dim, attn_dim=attn_dim, mem_dim=mem_dim),
        grid=(nb, s // tm),
        in_specs=[pl.BlockSpec((1, tm, d), tok), _resident(w_in.shape), _resident((1, n_gate))],
        out_specs=out_specs, out_shape=out_shape,
        compiler_params=_params(2),
        name="inproj",
    )(u, w_in, b_gate)


def _memkv_kernel(mem_ref, g_ref, w_ref, mk_ref, mv_ref, mkb_ref, mvb_ref):
    kv = _dot(_rms_norm(mem_ref[0], g_ref[...]).astype(_BF16), w_ref[...])
    half = kv.shape[1] // 2
    mk_ref[0] = kv[:, :half]
    mv_ref[0] = kv[:, half:]
    mkb_ref[0] = kv[:, :half].astype(_BF16)
    mvb_ref[0] = kv[:, half:].astype(_BF16)


def _memkv(mem, g_mem, w_mem_kv):
    nb, m, d = mem.shape
    md = w_mem_kv.shape[1] // 2
    blk = lambda b: (b, 0, 0)
    return pl.pallas_call(
        _memkv_kernel,
        grid=(nb,),
        in_specs=[pl.BlockSpec((1, m, d), blk), _resident((1, d)), _resident(w_mem_kv.shape)],
        out_specs=[pl.BlockSpec((1, m, md), blk)] * 4,
        out_shape=[jax.ShapeDtypeStruct((nb, m, md), _F32)] * 2 + [jax.ShapeDtypeStruct((nb, m, md), _BF16)] * 2,
        compiler_params=_params(1),
        name="memkv",
    )(mem, g_mem, w_mem_kv)


def _ln_swish(c, g, b):
    mu = jnp.mean(c, axis=-1, keepdims=True)
    cc = c - mu
    var = jnp.mean(cc * cc, axis=-1, keepdims=True)
    y = cc * lax.rsqrt(var + _NORM_EPS) * g + b
    return y * _sigmoid(y)


def _conv_kernel(main_ref, halo_ref, wd_ref, bd_ref, g_ref, b_ref, out_ref, ext_ref, *, rows):
    tt = main_ref.shape[1]
    ch = main_ref.shape[2]
    ext_ref[0:_HALO_ROWS, :] = jnp.where(pl.program_id(1) > 0, halo_ref[0], 0.0)
    ext_ref[_HALO_ROWS:, :] = main_ref[0]
    lead = _HALO_ROWS - (_CONV_WIDTH - 1)

    n_vreg = rows // _SUBLANES
    n_win = n_vreg + _HALO_ROWS // _SUBLANES
    sub = lax.broadcasted_iota(jnp.int32, (1, _SUBLANES, _LANES), 1)

    def chunk(i, carry):
        r0 = pl.multiple_of(i * rows, rows)
        cols = []
        for lb in range(ch // _LANES):
            ls = slice(lb * _LANES, (lb + 1) * _LANES)
            win = ext_ref[pl.ds(r0, rows + _HALO_ROWS), ls].reshape(n_win, _SUBLANES, _LANES)
            acc = jnp.broadcast_to(bd_ref[:, ls], (n_vreg, _SUBLANES, _LANES))
            for r in range(_SUBLANES):
                n_a = (_CONV_WIDTH - r + _SUBLANES - 1) // _SUBLANES
                base, k = divmod(lead + r, _SUBLANES)
                need = n_vreg + n_a - 1
                if k == 0:
                    sh = win[base:base + need]
                else:
                    rot = pltpu.roll(win[base:base + need + 1], _SUBLANES - k, axis=1)
                    sh = jnp.where(sub < _SUBLANES - k, rot[:-1], rot[1:])
                for a in range(n_a):
                    w = _SUBLANES * a + r
                    acc = acc + sh[a:a + n_vreg] * wd_ref[w:w + 1, ls]
            cols.append(acc.reshape(rows, _LANES))
        c = jnp.concatenate(cols, axis=-1)
        out_ref[0, pl.ds(r0, rows), :] = _ln_swish(c, g_ref[...], b_ref[...]).astype(_BF16)
        return carry

    lax.fori_loop(0, tt // rows, chunk, 0)


def _conv_prompt(glu, w_dw, b_dw, g_ln, b_ln, tt):
    nb, s, ch = glu.shape
    per = tt // _HALO_ROWS
    return pl.pallas_call(
        functools.partial(_conv_kernel, rows=64),
        grid=(nb, s // tt),
        in_specs=[pl.BlockSpec((1, tt, ch), lambda b, t: (b, t, 0)),
                  pl.BlockSpec((1, _HALO_ROWS, ch), lambda b, t: (b, jnp.maximum(t * per - 1, 0), 0)),
                  _resident(w_dw.shape), _resident((1, ch)), _resident((1, ch)), _resident((1, ch))],
        out_specs=pl.BlockSpec((1, tt, ch), lambda b, t: (b, t, 0)),
        out_shape=jax.ShapeDtypeStruct((nb, s, ch), _BF16),
        scratch_shapes=[pltpu.VMEM((tt + _HALO_ROWS, ch), _F32)],
        compiler_params=_params(2),
        name="conv_prompt",
    )(glu, glu, w_dw, b_dw, g_ln, b_ln)


def _conv_step_kernel(ext_ref, wd_ref, bd_ref, g_ref, b_ref, out_ref):
    n_q = out_ref.shape[0]
    for t in range(n_q):
        acc = jnp.broadcast_to(bd_ref[...], ext_ref.shape[1:])
        for w in range(_CONV_WIDTH):
            acc = acc + ext_ref[t + w] * wd_ref[w:w + 1, :]
        out_ref[t] = _ln_swish(acc, g_ref[...], b_ref[...]).astype(_BF16)


def _conv_step(ext_tm, w_dw, b_dw, g_ln, b_ln):
    n_rows, n_seq, ch = ext_tm.shape
    n_q = n_rows - (_CONV_WIDTH - 1)
    return pl.pallas_call(
        _conv_step_kernel,
        grid=(1,),
        in_specs=[_resident(ext_tm.shape), _resident(w_dw.shape), _resident((1, ch)), _resident((1, ch)),
                  _resident((1, ch))],
        out_specs=pl.BlockSpec((n_q, n_seq, ch), lambda i: (0, 0, 0)),
        out_shape=jax.ShapeDtypeStruct((n_q, n_seq, ch), _BF16),
        compiler_params=_params(1),
        name="conv_step",
    )(ext_tm, w_dw, b_dw, g_ln, b_ln)


def _rel_bucket(dist):
    n = jnp.maximum(dist, 0)
    max_exact = _N_BUCKETS // 2
    nf = jnp.maximum(n, 1).astype(_F32)
    large = max_exact + (jnp.log(nf / max_exact) / math.log(_MAX_DISTANCE / max_exact)
                         * (_N_BUCKETS - max_exact)).astype(jnp.int32)
    large = jnp.minimum(large, _N_BUCKETS - 1)
    return jnp.where(n < max_exact, n, large)


def _bias_of_dist(table, dist, per_row=False):
    n_col = table.shape[1]
    bucket = jnp.where(dist >= 0, _rel_bucket(dist), -1)
    if per_row:
        col_shape = (n_col,) + (1,) * (dist.ndim - 1)
    else:
        bucket = bucket[None]
        col_shape = (n_col,) + (1,) * dist.ndim
    out = jnp.full(jnp.broadcast_shapes(bucket.shape, col_shape), _NEG_INF, _F32)
    for b in range(_N_BUCKETS):
        out = jnp.where(bucket == b, table[b].reshape(col_shape), out)
    return out


def _topk_block_mask(score, blk, n_valid):
    rank = jnp.zeros(score.shape, jnp.int32)
    for m in range(score.shape[0]):
        sm = score[m:m + 1, :]
        counts = jnp.where(m < n_valid, 1, 0)
        rank = rank + jnp.where(sm > score, counts, 0) + jnp.where(sm == score, jnp.where(m < blk, counts, 0), 0)
    return (blk < n_valid) & (rank < _MOBA_TOPK)


def _moba_prompt_kernel(far_ref, q_ref, k_ref, vt_ref, bias_ref, o_ref, kmean_ref, pen_ref, s_buf, p_buf, *,
                        n_blk):
    blk_len = _MOBA_BLOCK
    hp = pl.program_id(1)
    own = pl.program_id(2)
    half = _HEAD_DIM

    @pl.when(own == 0)
    def _():
        for j in range(n_blk):
            kj = k_ref[0, j * blk_len:(j + 1) * blk_len, :].astype(_F32)
            kmean_ref[j:j + 1, :] = jnp.mean(kj, axis=0, keepdims=True)

    qf = q_ref[0].astype(_F32)
    lane = lax.broadcasted_iota(jnp.int32, qf.shape, 1)
    blk = lax.broadcasted_iota(jnp.int32, (n_blk, blk_len), 0)
    kmean = kmean_ref[...].astype(_BF16)
    prev = jnp.maximum(own - 1, 0)

    qhs, adj_pens = [], []
    for hd in range(2):
        qh = jnp.where((lane >= hd * half) & (lane < (hd + 1) * half), qf, 0.0).astype(_BF16)
        sel = _topk_block_mask(_nt_dot(kmean, qh), blk, own)
        pen_ref[hd] = jnp.where(sel & (blk < prev), far_ref[hp * 2 + hd], _NEG_INF)
        sel_adj = jnp.sum(jnp.where(sel & (blk == prev), 1.0, 0.0), axis=0, keepdims=True)
        adj_pens.append(jnp.where(own == 0, 0.0, jnp.where(sel_adj > 0.5, 0.0, _NEG_INF)))
        qhs.append(qh)

    log2e = math.log2(math.e)
    lo, hi = slice(0, blk_len), slice(blk_len, 2 * blk_len)

    def logits_into_s_buf(start_a, start_b):
        for hd in range(2):
            s_buf[hd, lo, :] = _nt_dot(k_ref[0, pl.ds(start_a, blk_len), :], qhs[hd])
            s_buf[hd, hi, :] = _nt_dot(k_ref[0, pl.ds(start_b, blk_len), :], qhs[hd])

    def pv_from_p_buf(start_a, start_b):
        res = []
        for hd in range(2):
            rows = slice(hd * half, (hd + 1) * half)
            res.append(_dot(vt_ref[0, rows, pl.ds(start_a, blk_len)], p_buf[hd, lo, :])
                       + _dot(vt_ref[0, rows, pl.ds(start_b, blk_len)], p_buf[hd, hi, :]))
        return res

    def softmax_stage(hd, row_a, row_b, m_old):
        s_a, s_b = s_buf[hd, lo, :], s_buf[hd, hi, :]
        m_new = jnp.maximum(jnp.max(s_a, axis=0, keepdims=True) + row_a, jnp.max(s_b, axis=0, keepdims=True) + row_b)
        if m_old is not None:
            m_new = jnp.maximum(m_new, m_old)
        p_a = jnp.exp2(s_a * log2e + (row_a - m_new) * log2e)
        p_b = jnp.exp2(s_b * log2e + (row_b - m_new) * log2e)
        p_buf[hd, lo, :] = p_a.astype(_BF16)
        p_buf[hd, hi, :] = p_b.astype(_BF16)
        return m_new, jnp.sum(p_a, axis=0, keepdims=True) + jnp.sum(p_b, axis=0, keepdims=True)

    start_a0 = pl.multiple_of(prev * blk_len, blk_len)
    start_b0 = pl.multiple_of(jnp.maximum(own, 1) * blk_len, blk_len)
    logits_into_s_buf(start_a0, start_b0)
    kind_a = jnp.where(own == 0, 0, 1)
    mask_b = jnp.where(own == 0, _NEG_INF, 0.0)
    zero_row = jnp.zeros((1, blk_len), _F32)
    carry = []
    for hd in range(2):
        s_buf[hd, lo, :] = s_buf[hd, lo, :] + (bias_ref[hd, kind_a] + adj_pens[hd])
        s_buf[hd, hi, :] = s_buf[hd, hi, :] + (bias_ref[hd, 0] + mask_b)
        m0, l0 = softmax_stage(hd, zero_row, zero_row, None)
        carry.append((m0, l0, jnp.zeros((half, blk_len), _F32), jnp.ones((1, blk_len), _F32)))
    n_far_stages = (prev + 1) // 2
    last_pair = n_blk // 2 - 1
    logits_into_s_buf(0, blk_len)

    def stage_blocks(t):
        far_a = (2 * t - 2) * blk_len
        return (pl.multiple_of(jnp.where(t == 0, start_a0, far_a), blk_len),
                pl.multiple_of(jnp.where(t == 0, start_b0, far_a + blk_len), blk_len))

    def far_stage(t, carry):
        pv = pv_from_p_buf(*stage_blocks(t))
        new = []
        for hd in range(2):
            m_old, l_old, acc, alpha_prev = carry[hd]
            m_new, l_blk = softmax_stage(hd, pen_ref[hd, pl.ds(2 * t, 1), :], pen_ref[hd, pl.ds(2 * t + 1, 1), :],
                                         m_old)
            alpha = jnp.exp(m_old - m_new)
            new.append((m_new, alpha * l_old + l_blk, alpha_prev * acc + pv[hd], alpha))
        nxt = pl.multiple_of(jnp.minimum(t + 1, last_pair) * 2 * blk_len, 2 * blk_len)
        logits_into_s_buf(nxt, nxt + blk_len)
        return tuple(new)

    carry = lax.fori_loop(0, n_far_stages, far_stage, tuple(carry))
    pv = pv_from_p_buf(*stage_blocks(n_far_stages))
    outs = [(alpha * acc + pv[hd]) * (1.0 / l) for hd, (_, l, acc, alpha) in enumerate(carry)]
    o_ref[0] = jnp.concatenate(outs, axis=0).T.astype(_BF16)


def _moba_prompt(q, kb, vt, rel_bias):
    nb, s, attn_dim = q.shape
    n_blk = s // _MOBA_BLOCK
    assert n_blk >= 2
    n_pair = attn_dim // _LANES
    kk = jnp.arange(_MOBA_BLOCK, dtype=jnp.int32)[:, None]
    qq = jnp.arange(_MOBA_BLOCK, dtype=jnp.int32)[None, :]
    table = rel_bias.astype(_F32)
    tiles = _bias_of_dist(table, jnp.stack([qq - kk, _MOBA_BLOCK + qq - kk]))
    far = _bias_of_dist(table, jnp.full((1,), _MOBA_BLOCK + 1, jnp.int32))[:, 0]
    return pl.pallas_call(
        functools.partial(_moba_prompt_kernel, n_blk=n_blk),
        grid=(nb, n_pair, n_blk),
        in_specs=[pl.BlockSpec(memory_space=pltpu.SMEM),
                  pl.BlockSpec((1, _MOBA_BLOCK, _LANES), lambda b, h, c: (b, c, h)),
                  pl.BlockSpec((1, s, _LANES), lambda b, h, c: (b, 0, h)),
                  pl.BlockSpec((1, _LANES, s), lambda b, h, c: (b, h, 0)),
                  pl.BlockSpec((2, 2, _MOBA_BLOCK, _MOBA_BLOCK), lambda b, h, c: (h, 0, 0, 0))],
        out_specs=pl.BlockSpec((1, _MOBA_BLOCK, _LANES), lambda b, h, c: (b, c, h)),
        out_shape=jax.ShapeDtypeStruct((nb, s, attn_dim), _BF16),
        scratch_shapes=[pltpu.VMEM((n_blk, _LANES), _F32), pltpu.VMEM((2, n_blk, _MOBA_BLOCK), _F32),
                        pltpu.VMEM((2, 2 * _MOBA_BLOCK, _MOBA_BLOCK), _F32),
                        pltpu.VMEM((2, 2 * _MOBA_BLOCK, _MOBA_BLOCK), _BF16)],
        compiler_params=_params(3),
        name="moba_prompt",
    )(far, q, kb, vt, tiles)


def _moba_step_kernel(pt_ref, qbd_ref, knew_ref, vnew_ref, badj_ref, bfar_ref, bown_ref, *rest, n_pages, n_q):
    del pt_ref
    kt_pages = rest[:n_pages]
    vt_pages = rest[n_pages:2 * n_pages]
    o_ref = rest[2 * n_pages]
    ppb = _MOBA_BLOCK // _PAGE_SIZE
    n_blk = n_pages // ppb
    qbd = qbd_ref[0]
    feat = qbd.shape[1]

    def block_of(pages, j):
        return jnp.concatenate([pages[p][0] for p in range(j * ppb, (j + 1) * ppb)], axis=1).astype(_BF16)

    s_blk = [_dot(qbd, block_of(kt_pages, j)) for j in range(n_blk)]
    score = [jnp.sum(s, axis=1, keepdims=True) for s in s_blk]
    logits = []
    for n in range(n_blk):
        rank = jnp.zeros(score[n].shape, jnp.int32)
        for m in range(n_blk):
            if m < n:
                rank = rank + jnp.where(score[m] >= score[n], 1, 0)
            elif m > n:
                rank = rank + jnp.where(score[m] > score[n], 1, 0)
        bias = badj_ref[...] if n == n_blk - 1 else bfar_ref[:, 0:1]
        logits.append(jnp.where(rank < _MOBA_TOPK, s_blk[n] + bias, _NEG_INF))
    s_own = _nt_dot(qbd, knew_ref[0].astype(_BF16)) + bown_ref[:, 0:_NEW_ROWS]
    m_row = jnp.max(s_own, axis=1, keepdims=True)
    for lg in logits:
        m_row = jnp.maximum(m_row, jnp.max(lg, axis=1, keepdims=True))
    p_own = jnp.exp(s_own - m_row)
    probs = [jnp.exp(lg - m_row) for lg in logits]
    denom = jnp.sum(p_own, axis=1, keepdims=True)
    for p in probs:
        denom = denom + jnp.sum(p, axis=1, keepdims=True)
    inv = 1.0 / denom
    out = _dot((p_own * inv).astype(_BF16), vnew_ref[0].astype(_BF16))
    for j in range(n_blk):
        out = out + _nt_dot((probs[j] * inv).astype(_BF16), block_of(vt_pages, j))
    r_head = lax.broadcasted_iota(jnp.int32, out.shape, 0) % _HEAD_GROUP
    c_head = lax.broadcasted_iota(jnp.int32, out.shape, 1) // _HEAD_DIM
    out = jnp.where(r_head == c_head, out, 0.0)
    o_ref[0] = jnp.sum(out.reshape(n_q, _HEAD_GROUP, feat), axis=1)


def _block_diag_queries(q, n_heads, head_dim, group):
    n, n_q, feat = q.shape
    r_head = jnp.arange(group, dtype=jnp.int32)[:, None]
    c_head = jnp.arange(feat, dtype=jnp.int32)[None, :] // head_dim
    mask = (r_head == c_head) & (r_head < n_heads)
    return jnp.where(mask[None, None], q[:, :, None, :], jnp.zeros((), q.dtype)).reshape(n, n_q * group, feat)


def _moba_step(page_table, q, k_new, v_new, cache_kt, cache_vt, rel_bias):
    n_seq, n_q, feat = q.shape
    n_pages = page_table.shape[1]
    past = n_pages * _PAGE_SIZE
    assert past % _MOBA_BLOCK == 0 and n_q <= _NEW_ROWS
    rows = n_q * _HEAD_GROUP
    qbd = _block_diag_queries(q, _N_HEADS, _HEAD_DIM, _HEAD_GROUP)
    padn = ((0, 0), (0, _NEW_ROWS - n_q), (0, 0))
    k_new = jnp.pad(k_new, padn)
    v_new = jnp.pad(v_new, padn)
    table = jnp.tile(jnp.pad(rel_bias.astype(_F32), ((0, 0), (0, _HEAD_GROUP - _N_HEADS))), (1, n_q))
    qi = jnp.arange(rows, dtype=jnp.int32)[:, None] // _HEAD_GROUP
    kk = jnp.arange(_MOBA_BLOCK, dtype=jnp.int32)[None, :]
    badj = _bias_of_dist(table, _MOBA_BLOCK + qi - kk, per_row=True)
    bfar = _bias_of_dist(table, jnp.full((rows, _LANES), _MOBA_BLOCK + 1, jnp.int32), per_row=True)
    kj = jnp.arange(_LANES, dtype=jnp.int32)[None, :]
    bown = _bias_of_dist(table, jnp.where(kj < n_q, qi - kj, -1), per_row=True)

    page = lambda p: pl.BlockSpec((1, feat, _PAGE_SIZE), lambda i, pt, p=p: (pt[i, p], 0, 0))
    seq3 = lambda r: pl.BlockSpec((1, r, feat), lambda i, pt: (i, 0, 0))
    const2 = lambda shape: pl.BlockSpec(shape, lambda i, pt: (0, 0), pipeline_mode=pl.Buffered(1))
    grid_spec = pltpu.PrefetchScalarGridSpec(
        num_scalar_prefetch=1,
        grid=(n_seq,),
        in_specs=[seq3(rows), seq3(_NEW_ROWS), seq3(_NEW_ROWS), const2((rows, _MOBA_BLOCK)),
                  const2((rows, _LANES)), const2((rows, _LANES))] + [page(p) for p in range(n_pages)] * 2,
        out_specs=pl.BlockSpec((1, n_q, feat), lambda i, pt: (i, 0, 0)))
    return pl.pallas_call(
        functools.partial(_moba_step_kernel, n_pages=n_pages, n_q=n_q),
        grid_spec=grid_spec,
        out_shape=jax.ShapeDtypeStruct((n_seq, n_q, feat), _F32),
        compiler_params=_params(1),
        name="moba_step",
    )(page_table, qbd, k_new, v_new, badj, bfar, bown, *([cache_kt] * n_pages), *([cache_vt] * n_pages))


def _softmax_rows(s):
    p = jnp.exp(s - jnp.max(s, axis=1, keepdims=True))
    return p * (1.0 / jnp.sum(p, axis=1, keepdims=True))


def _memattn_kernel(qm_ref, mk_ref, mv_ref, o_ref):
    scale = _MEM_HEAD_DIM ** -0.5
    for h in range(_MEM_HEADS):
        hs = slice(h * _MEM_HEAD_DIM, (h + 1) * _MEM_HEAD_DIM)
        p = _softmax_rows(_nt_dot(qm_ref[0, :, hs], mk_ref[0, :, hs]) * scale)
        o_ref[0, :, hs] = _dot(p.astype(_BF16), mv_ref[0, :, hs]).astype(_BF16)


def _memattn(qm, mkb, mvb, tq):
    nb, s, md = qm.shape
    m = mkb.shape[1]
    return pl.pallas_call(
        _memattn_kernel,
        grid=(nb, s // tq),
        in_specs=[pl.BlockSpec((1, tq, md), lambda b, i: (b, i, 0)),
                  pl.BlockSpec((1, m, md), lambda b, i: (b, 0, 0)),
                  pl.BlockSpec((1, m, md), lambda b, i: (b, 0, 0))],
        out_specs=pl.BlockSpec((1, tq, md), lambda b, i: (b, i, 0)),
        out_shape=jax.ShapeDtypeStruct((nb, s, md), _BF16),
        compiler_params=_params(2),
        name="memattn",
    )(qm, mkb, mvb)


def _memattn_step_kernel(q_ref, mk_ref, mv_ref, o_ref):
    scale = _MEM_HEAD_DIM ** -0.5
    for i in range(mk_ref.shape[0]):
        s = _nt_dot(q_ref[i], mk_ref[i].astype(_BF16)) * scale
        r_head = lax.broadcasted_iota(jnp.int32, s.shape, 0) % _MEM_HEADS
        c_head = lax.broadcasted_iota(jnp.int32, s.shape, 1) % _MEM_HEADS
        p = _softmax_rows(jnp.where(r_head == c_head, s, _NEG_INF))
        o_ref[i] = _dot(p.astype(_BF16), mv_ref[i].astype(_BF16))


def _memattn_step(qm, mem_k, mem_v, seqs_per_step):
    n_seq, rows, dh = qm.shape
    mh = mem_k.shape[1]
    blk = lambda i: (i, 0, 0)
    return pl.pallas_call(
        _memattn_step_kernel,
        grid=(n_seq // seqs_per_step,),
        in_specs=[pl.BlockSpec((seqs_per_step, rows, dh), blk), pl.BlockSpec((seqs_per_step, mh, dh), blk),
                  pl.BlockSpec((seqs_per_step, mh, dh), blk)],
        out_specs=pl.BlockSpec((seqs_per_step, rows, dh), blk),
        out_shape=jax.ShapeDtypeStruct((n_seq, rows, dh), _F32),
        compiler_params=_params(1),
        name="memattn_step",
    )(qm, mem_k, mem_v)


def _merge_kernel(h_ref, c_ref, o_ref, om_ref, gate_ref, wc_ref, wa_ref, wm_ref, wo_ref, gpost_ref,
                  g2pre_ref, w2in_ref, w2out_ref, g2post_ref, y_ref, *, n_chunk):
    d = h_ref.shape[1]
    merged = (gate_ref[:, 0:d].astype(_F32) * _dot(c_ref[...], wc_ref[...])
              + gate_ref[:, d:2 * d].astype(_F32) * _dot(o_ref[...], wa_ref[...])
              + gate_ref[:, 2 * d:3 * d].astype(_F32) * _dot(om_ref[...], wm_ref[...]))
    h = h_ref[...] + _rms_norm(_dot(merged.astype(_BF16), wo_ref[...]), gpost_ref[...])
    y_ref[...] = _ffn_half_step(h, g2pre_ref[...], w2in_ref, w2out_ref, g2post_ref[...], n_chunk)


def _merge(h, c_act, o, om, gates, w_conv_out, w_attn_out, w_mem_out, w_out, g_post, g2_pre, w2_in, w2_out,
           g2_post, tm):
    n, d = h.shape
    row = lambda i: (i, 0)
    tile = lambda a: pl.BlockSpec((tm, a.shape[1]), row)
    return pl.pallas_call(
        functools.partial(_merge_kernel, n_chunk=_ffn_chunks(w2_out.shape[0])),
        grid=(n // tm,),
        in_specs=[tile(h), tile(c_act), tile(o), tile(om), tile(gates),
                  _resident(w_conv_out.shape), _resident(w_attn_out.shape), _resident(w_mem_out.shape),
                  _resident(w_out.shape), _resident((1, d)), _resident((1, d)), _resident(w2_in.shape),
                  _resident(w2_out.shape), _resident((1, d))],
        out_specs=pl.BlockSpec((tm, d), row),
        out_shape=jax.ShapeDtypeStruct((n, d), _F32),
        compiler_params=_params(1),
        name="merge_ffn2",
    )(h, c_act, o, om, gates, w_conv_out, w_attn_out, w_mem_out, w_out, g_post, g2_pre, w2_in, w2_out, g2_post)


def _tile(n, want):
    t = min(n, want)
    assert n % t == 0
    return t


def _token_front(x, p):
    nb, s, d = x.shape
    n = nb * s
    h, u = _ffn1(x.reshape(n, d), p['g_ffn1_pre'], p['w_ffn1_in'], p['w_ffn1_out'], p['g_ffn1_post'],
                 p['g_mix_pre'], _tile(n, 512))
    return h, u.reshape(nb, s, d)


def _token_back(h, c_act, o, om, gates, p):
    n = h.shape[0]
    flat = lambda a: a.reshape(n, a.shape[-1])
    return _merge(h, flat(c_act), flat(o), flat(om), flat(gates), p['w_conv_out'], p['w_attn_out'],
                  p['w_mem_out'], p['w_out'], p['g_mix_post'], p['g_ffn2_pre'], p['w_ffn2_in'], p['w_ffn2_out'],
                  p['g_ffn2_post'], _tile(n, 256))


def _dims(p):
    conv_dim = p['w_conv_out'].shape[0]
    attn_dim = p['w_attn_out'].shape[0]
    mem_dim = p['w_mem_out'].shape[0]
    return conv_dim, attn_dim, mem_dim


def _prompt_layer(x, mem, rel_bias, p):
    nb, s, d = x.shape
    conv_dim, attn_dim, mem_dim = _dims(p)
    h, u = _token_front(x, p)
    glu, q, k, v, qm, gates, kb, vt = _inproj(u, p['w_in'], p['b_gate'], _tile(s, 512), conv_dim, attn_dim,
                                              mem_dim, True)
    mk, mv, mkb, mvb = _memkv(mem, p['g_mem'], p['w_mem_kv'])
    c_act = _conv_prompt(glu, p['w_dwconv'], p['b_dwconv'], p['g_conv_ln'], p['b_conv_ln'], _tile(s, 512))
    o = _moba_prompt(q, kb, vt, rel_bias)
    om = _memattn(qm, mkb, mvb, _tile(s, 512))
    y = _token_back(h, c_act, o, om, gates, p).reshape(nb, s, d)
    conv_state = glu[:, s - (_CONV_WIDTH - 1):]
    heads = lambda a: a.reshape(nb, s, _N_HEADS, _HEAD_DIM)
    mheads = lambda a: a.reshape(nb, a.shape[1], _MEM_HEADS, _MEM_HEAD_DIM)
    return y, heads(k), heads(v), mheads(mk), mheads(mv), conv_state


def _sample_layer(x, conv_prev, cache_k, cache_v, page_table, mem_k, mem_v, rel_bias, p):
    n_seq, n_q, d = x.shape
    conv_dim, attn_dim, mem_dim = _dims(p)
    n = n_seq * n_q
    h, u = _token_front(x.reshape(1, n, d), p)
    glu, q, k, v, qm, gates = _inproj(u, p['w_in'], p['b_gate'], _tile(n, 512), conv_dim, attn_dim, mem_dim,
                                      False)
    per_seq = lambda a: a.reshape(n_seq, n_q, a.shape[-1])
    glu = per_seq(glu)
    conv_in = jnp.concatenate([conv_prev, glu], axis=1)
    c_act = _conv_step(conv_in.transpose(1, 0, 2), p['w_dwconv'], p['b_dwconv'], p['g_conv_ln'], p['b_conv_ln'])
    c_act = c_act.transpose(1, 0, 2)
    page_t = lambda c: c.transpose(0, 2, 3, 1).reshape(c.shape[0], attn_dim, c.shape[1])
    o = _moba_step(page_table, per_seq(q), per_seq(k), per_seq(v), page_t(cache_k), page_t(cache_v), rel_bias)
    head_rows = lambda a: a.reshape(n_seq, -1, _MEM_HEAD_DIM)
    om = _memattn_step(head_rows(qm), head_rows(mem_k), head_rows(mem_v), 8)
    y = _token_back(h, c_act, o.astype(_BF16), om.reshape(n, mem_dim).astype(_BF16), gates, p)
    conv_state = conv_in[:, n_q:]
    heads = lambda a: a.reshape(n_seq, n_q, _N_HEADS, _HEAD_DIM)
    return y.reshape(n_seq, n_q, d), heads(k), heads(v), conv_state


def _layer_params(l, g_ffn1_pre, w_ffn1_in, w_ffn1_out, g_ffn1_post, g_mix_pre, w_in, b_gate, w_dwconv, b_dwconv,
                  g_conv_ln, b_conv_ln, w_conv_out, w_attn_out, g_mem, w_mem_kv, w_mem_out, w_out, g_mix_post,
                  g_ffn2_pre, w_ffn2_in, w_ffn2_out, g_ffn2_post):
    vec = lambda a: a[l].astype(_F32).reshape(1, -1)
    mat = lambda a: a[l].astype(_BF16)
    return dict(
        g_ffn1_pre=vec(g_ffn1_pre), w_ffn1_in=mat(w_ffn1_in), w_ffn1_out=mat(w_ffn1_out),
        g_ffn1_post=vec(g_ffn1_post), g_mix_pre=vec(g_mix_pre), w_in=mat(w_in), b_gate=vec(b_gate),
        w_dwconv=w_dwconv[l].astype(_F32), b_dwconv=vec(b_dwconv), g_conv_ln=vec(g_conv_ln),
        b_conv_ln=vec(b_conv_ln), w_conv_out=mat(w_conv_out), w_attn_out=mat(w_attn_out), g_mem=vec(g_mem),
        w_mem_kv=mat(w_mem_kv), w_mem_out=mat(w_mem_out), w_out=mat(w_out), g_mix_post=vec(g_mix_post),
        g_ffn2_pre=vec(g_ffn2_pre), w_ffn2_in=mat(w_ffn2_in), w_ffn2_out=mat(w_ffn2_out),
        g_ffn2_post=vec(g_ffn2_post))


def kernel(x_prompt, x_sample, cache_k, cache_v, cache_mem_k, cache_mem_v, state_conv, page_table, mem_prompt, g_ffn1_pre, w_ffn1_in, w_ffn1_out, g_ffn1_post, g_mix_pre, w_in, b_gate, w_dwconv, b_dwconv, g_conv_ln, b_conv_ln, w_conv_out, w_attn_out, rel_bias, g_mem, w_mem_kv, w_mem_out, w_out, g_mix_post, g_ffn2_pre, w_ffn2_in, w_ffn2_out, g_ffn2_post):
    depth = w_in.shape[0]
    hp, hs = x_prompt, x_sample
    outs = [[] for _ in range(8)]
    for l in range(depth):
        p = _layer_params(l, g_ffn1_pre, w_ffn1_in, w_ffn1_out, g_ffn1_post, g_mix_pre, w_in, b_gate, w_dwconv,
                          b_dwconv, g_conv_ln, b_conv_ln, w_conv_out, w_attn_out, g_mem, w_mem_kv, w_mem_out,
                          w_out, g_mix_post, g_ffn2_pre, w_ffn2_in, w_ffn2_out, g_ffn2_post)
        hp, kp, vp, mkp, mvp, cp = _prompt_layer(hp, mem_prompt, rel_bias, p)
        hs, ks, vs, cs = _sample_layer(hs, state_conv[l], cache_k[l], cache_v[l], page_table, cache_mem_k[l],
                                       cache_mem_v[l], rel_bias, p)
        for acc, val in zip(outs, (kp, vp, mkp, mvp, cp, ks, vs, cs)):
            acc.append(val)
    return (hp, hs) + tuple(jnp.stack(a) for a in outs)
```

```python
import functools
import math

import jax
import jax.numpy as jnp
from jax import lax
from jax.experimental import pallas as pl
from jax.experimental.pallas import tpu as pltpu

_F32 = jnp.float32
_BF16 = jnp.bfloat16

_CONV_WIDTH = 31
_N_HEADS = 12
_HEAD_DIM = 64
_MOBA_BLOCK = 256
_MOBA_TOPK = 3
_MEM_HEADS = 4
_MEM_HEAD_DIM = 128
_N_BUCKETS = 32
_MAX_DISTANCE = 128
_N_BRANCH = 3
_PAGE_SIZE = 128
_NORM_EPS = 1e-6
_NEG_INF = -1e30

_LANES = 128
_SUBLANES = 8
_VMEM_BYTES_V7X = 64 * 1024 * 1024
_VMEM_LIMIT = _VMEM_BYTES_V7X - 8 * 1024 * 1024

_HALO_ROWS = 32
_CONV_CHUNK_ROWS = 64
_HEAD_GROUP = 16
_NEW_ROWS = 16
_QUERY_BLOCKS_PER_STEP = 2


def _nt_dot(a, b):
    return lax.dot_general(a, b, (((1,), (1,)), ((), ())), preferred_element_type=_F32)


def _dot(a, b):
    return jnp.dot(a, b, preferred_element_type=_F32)


def _sigmoid(x):
    return 1.0 / (1.0 + jnp.exp(-x))


def _rms_norm(x, g):
    return x * lax.rsqrt(jnp.mean(x * x, axis=-1, keepdims=True) + _NORM_EPS) * g


def _resident(shape):
    nd = len(shape)
    return pl.BlockSpec(shape, lambda *_: (0,) * nd, pipeline_mode=pl.Buffered(1))


def _params(n_parallel_axes):
    return pltpu.CompilerParams(
        dimension_semantics=("arbitrary",) * n_parallel_axes,
        vmem_limit_bytes=_VMEM_LIMIT)


def _ffn_half_step(x, g_pre, w_in_ref, w_out_ref, g_post, n_chunk):
    d_ff = w_out_ref.shape[0]
    fc = d_ff // n_chunk
    xn = _rms_norm(x, g_pre).astype(_BF16)
    acc = None
    for c in range(n_chunk):
        a = _dot(xn, w_in_ref[:, c * fc:(c + 1) * fc])
        b = _dot(xn, w_in_ref[:, d_ff + c * fc:d_ff + (c + 1) * fc])
        hid = (a * _sigmoid(a) * b).astype(_BF16)
        part = _dot(hid, w_out_ref[c * fc:(c + 1) * fc, :])
        acc = part if acc is None else acc + part
    return x + 0.5 * _rms_norm(acc, g_post)


def _ffn1_kernel(x_ref, gpre_ref, win_ref, wout_ref, gpost_ref, gmix_ref, h_ref, u_ref, *, n_chunk):
    h = _ffn_half_step(x_ref[...], gpre_ref[...], win_ref, wout_ref, gpost_ref[...], n_chunk)
    h_ref[...] = h
    u_ref[...] = _rms_norm(h, gmix_ref[...]).astype(_BF16)


def _ffn_chunks(d_ff):
    for n in (2, 1):
        if d_ff % (n * _LANES) == 0:
            return n
    return 1


def _ffn1(x, g_pre, w_in, w_out, g_post, g_mix, tm):
    n, d = x.shape
    d_ff = w_out.shape[0]
    row = lambda i: (i, 0)
    return pl.pallas_call(
        functools.partial(_ffn1_kernel, n_chunk=_ffn_chunks(d_ff)),
        grid=(n // tm,),
        in_specs=[pl.BlockSpec((tm, d), row), _resident((1, d)), _resident(w_in.shape),
                  _resident(w_out.shape), _resident((1, d)), _resident((1, d))],
        out_specs=[pl.BlockSpec((tm, d), row), pl.BlockSpec((tm, d), row)],
        out_shape=[jax.ShapeDtypeStruct((n, d), _F32), jax.ShapeDtypeStruct((n, d), _BF16)],
        compiler_params=_params(1),
        name="ffn1",
    )(x, g_pre, w_in, w_out, g_post, g_mix)


def _inproj_kernel(u_ref, w_ref, bg_ref, *rest, conv_dim, attn_dim, mem_dim, whole_sequences):
    if whole_sequences:
        wd_ref, bd_ref, gln_ref, bln_ref = rest[:4]
        rest = rest[4:]
    conv_ref, q_ref, k_ref, v_ref, qm_ref, gate_ref = rest[:6]
    u = u_ref[0]
    tm = u.shape[0]
    o1 = 2 * conv_dim
    o2 = o1 + attn_dim
    o3 = o2 + attn_dim
    o4 = o3 + attn_dim
    o5 = o4 + mem_dim

    def proj(lo, hi):
        return _dot(u, w_ref[:, lo:hi])

    if whole_sequences:
        kb_ref, vt_ref, cact_ref, ext_ref, cbuf_ref = rest[6:]
        @pl.when(pl.program_id(1) == 0)
        def _():
            ext_ref[0:_HALO_ROWS, :] = jnp.zeros((_HALO_ROWS, conv_dim), _F32)

        @pl.when(pl.program_id(1) > 0)
        def _():
            ext_ref[0:_HALO_ROWS, :] = ext_ref[tm:tm + _HALO_ROWS, :]

    glu = proj(0, conv_dim) * _sigmoid(proj(conv_dim, o1))
    conv_ref[0] = glu
    if whole_sequences:
        ext_ref[_HALO_ROWS:, :] = glu
        for r0 in range(0, tm, _CONV_CHUNK_ROWS):
            _conv_module_rows(ext_ref, cbuf_ref, cact_ref, r0, _CONV_CHUNK_ROWS, wd_ref, bd_ref, gln_ref, bln_ref)
    q_ref[0] = (proj(o1, o2) * (_HEAD_DIM ** -0.5)).astype(_BF16)
    k = proj(o2, o3)
    v = proj(o3, o4)
    k_ref[0] = k
    v_ref[0] = v
    qm_ref[0] = proj(o4, o5).astype(_BF16)
    gate_ref[0] = _sigmoid(proj(o5, w_ref.shape[1]) + bg_ref[...]).astype(_BF16)
    if whole_sequences:
        kb_ref[0] = k.astype(_BF16)
        vt_ref[0] = v.T.astype(_BF16)


def _inproj(u, w_in, b_gate, tm, conv_dim, attn_dim, mem_dim, conv_params=None):
    nb, s, d = u.shape
    n_gate = b_gate.shape[1]
    whole_sequences = conv_params is not None
    tok = lambda b, i: (b, i, 0)
    in_specs = [pl.BlockSpec((1, tm, d), tok), _resident(w_in.shape), _resident((1, n_gate))]
    out_specs = [pl.BlockSpec((1, tm, conv_dim), tok), pl.BlockSpec((1, tm, attn_dim), tok),
                 pl.BlockSpec((1, tm, attn_dim), tok), pl.BlockSpec((1, tm, attn_dim), tok),
                 pl.BlockSpec((1, tm, mem_dim), tok), pl.BlockSpec((1, tm, n_gate), tok)]
    out_shape = [jax.ShapeDtypeStruct((nb, s, conv_dim), _F32), jax.ShapeDtypeStruct((nb, s, attn_dim), _BF16),
                 jax.ShapeDtypeStruct((nb, s, attn_dim), _F32), jax.ShapeDtypeStruct((nb, s, attn_dim), _F32),
                 jax.ShapeDtypeStruct((nb, s, mem_dim), _BF16), jax.ShapeDtypeStruct((nb, s, n_gate), _BF16)]
    scratch_shapes = []
    operands = [u, w_in, b_gate]
    if whole_sequences:
        assert tm % _CONV_CHUNK_ROWS == 0
        in_specs += [_resident(a.shape) for a in conv_params]
        operands += list(conv_params)
        out_specs += [pl.BlockSpec((1, tm, attn_dim), tok), pl.BlockSpec((1, attn_dim, tm), lambda b, i: (b, 0, i)),
                      pl.BlockSpec((1, tm, conv_dim), tok)]
        out_shape += [jax.ShapeDtypeStruct((nb, s, attn_dim), _BF16), jax.ShapeDtypeStruct((nb, attn_dim, s), _BF16),
                      jax.ShapeDtypeStruct((nb, s, conv_dim), _BF16)]
        scratch_shapes = [pltpu.VMEM((tm + _HALO_ROWS, conv_dim), _F32), pltpu.VMEM((tm, conv_dim), _F32)]
    return pl.pallas_call(
        functools.partial(_inproj_kernel, conv_dim=conv_dim, attn_dim=attn_dim, mem_dim=mem_dim,
                          whole_sequences=whole_sequences),
        grid=(nb, s // tm),
        in_specs=in_specs, out_specs=out_specs, out_shape=out_shape, scratch_shapes=scratch_shapes,
        compiler_params=_params(2),
        name="inproj",
    )(*operands)


def _memkv_kernel(mem_ref, g_ref, w_ref, mk_ref, mv_ref, mkb_ref, mvb_ref):
    kv = _dot(_rms_norm(mem_ref[0], g_ref[...]).astype(_BF16), w_ref[...])
    half = kv.shape[1] // 2
    mk_ref[0] = kv[:, :half]
    mv_ref[0] = kv[:, half:]
    mkb_ref[0] = kv[:, :half].astype(_BF16)
    mvb_ref[0] = kv[:, half:].astype(_BF16)


def _memkv(mem, g_mem, w_mem_kv):
    nb, m, d = mem.shape
    md = w_mem_kv.shape[1] // 2
    blk = lambda b: (b, 0, 0)
    return pl.pallas_call(
        _memkv_kernel,
        grid=(nb,),
        in_specs=[pl.BlockSpec((1, m, d), blk), _resident((1, d)), _resident(w_mem_kv.shape)],
        out_specs=[pl.BlockSpec((1, m, md), blk)] * 4,
        out_shape=[jax.ShapeDtypeStruct((nb, m, md), _F32)] * 2 + [jax.ShapeDtypeStruct((nb, m, md), _BF16)] * 2,
        compiler_params=_params(1),
        name="memkv",
    )(mem, g_mem, w_mem_kv)


def _ln_swish(c, g, b):
    mu = jnp.mean(c, axis=-1, keepdims=True)
    cc = c - mu
    var = jnp.mean(cc * cc, axis=-1, keepdims=True)
    y = cc * lax.rsqrt(var + _NORM_EPS) * g + b
    return y * _sigmoid(y)


def _conv_module_rows(ext_ref, cbuf_ref, out_ref, r0, rows, wd_ref, bd_ref, g_ref, b_ref):
    ch = ext_ref.shape[1]
    lead = _HALO_ROWS - (_CONV_WIDTH - 1)
    n_vreg = rows // _SUBLANES
    n_win = n_vreg + _HALO_ROWS // _SUBLANES
    sub = lax.broadcasted_iota(jnp.int32, (1, _SUBLANES, _LANES), 1)
    blocks = [slice(lb * _LANES, (lb + 1) * _LANES) for lb in range(ch // _LANES)]
    rs = slice(r0, r0 + rows)
    total = None
    for ls in blocks:
        win = ext_ref[r0:r0 + rows + _HALO_ROWS, ls].reshape(n_win, _SUBLANES, _LANES)
        acc = jnp.broadcast_to(bd_ref[:, ls], (n_vreg, _SUBLANES, _LANES))
        for r in range(_SUBLANES):
            n_a = (_CONV_WIDTH - r + _SUBLANES - 1) // _SUBLANES
            base, k = divmod(lead + r, _SUBLANES)
            need = n_vreg + n_a - 1
            if k == 0:
                sh = win[base:base + need]
            else:
                rot = pltpu.roll(win[base:base + need + 1], _SUBLANES - k, axis=1)
                sh = jnp.where(sub < _SUBLANES - k, rot[:-1], rot[1:])
            for a in range(n_a):
                w = _SUBLANES * a + r
                acc = acc + sh[a:a + n_vreg] * wd_ref[w:w + 1, ls]
        acc = acc.reshape(rows, _LANES)
        cbuf_ref[rs, ls] = acc
        total = acc if total is None else total + acc
    mu = jnp.sum(total, axis=-1, keepdims=True) * (1.0 / ch)
    sq = None
    for ls in blocks:
        d = cbuf_ref[rs, ls] - mu
        sq = d * d if sq is None else sq + d * d
    rstd = lax.rsqrt(jnp.sum(sq, axis=-1, keepdims=True) * (1.0 / ch) + _NORM_EPS)
    for ls in blocks:
        y = (cbuf_ref[rs, ls] - mu) * rstd * g_ref[:, ls] + b_ref[:, ls]
        out_ref[0, rs, ls] = (y * _sigmoid(y)).astype(_BF16)


def _conv_step_kernel(ext_ref, wd_ref, bd_ref, g_ref, b_ref, out_ref):
    n_q = out_ref.shape[0]
    for t in range(n_q):
        acc = jnp.broadcast_to(bd_ref[...], ext_ref.shape[1:])
        for w in range(_CONV_WIDTH):
            acc = acc + ext_ref[t + w] * wd_ref[w:w + 1, :]
        out_ref[t] = _ln_swish(acc, g_ref[...], b_ref[...]).astype(_BF16)


def _conv_step(ext_tm, w_dw, b_dw, g_ln, b_ln):
    n_rows, n_seq, ch = ext_tm.shape
    n_q = n_rows - (_CONV_WIDTH - 1)
    return pl.pallas_call(
        _conv_step_kernel,
        grid=(1,),
        in_specs=[_resident(ext_tm.shape), _resident(w_dw.shape), _resident((1, ch)), _resident((1, ch)),
                  _resident((1, ch))],
        out_specs=pl.BlockSpec((n_q, n_seq, ch), lambda i: (0, 0, 0)),
        out_shape=jax.ShapeDtypeStruct((n_q, n_seq, ch), _BF16),
        compiler_params=_params(1),
        name="conv_step",
    )(ext_tm, w_dw, b_dw, g_ln, b_ln)


def _rel_bucket(dist):
    n = jnp.maximum(dist, 0)
    max_exact = _N_BUCKETS // 2
    nf = jnp.maximum(n, 1).astype(_F32)
    large = max_exact + (jnp.log(nf / max_exact) / math.log(_MAX_DISTANCE / max_exact)
                         * (_N_BUCKETS - max_exact)).astype(jnp.int32)
    large = jnp.minimum(large, _N_BUCKETS - 1)
    return jnp.where(n < max_exact, n, large)


def _bias_of_dist(table, dist, per_row=False):
    n_col = table.shape[1]
    bucket = jnp.where(dist >= 0, _rel_bucket(dist), -1)
    if per_row:
        col_shape = (n_col,) + (1,) * (dist.ndim - 1)
    else:
        bucket = bucket[None]
        col_shape = (n_col,) + (1,) * dist.ndim
    out = jnp.full(jnp.broadcast_shapes(bucket.shape, col_shape), _NEG_INF, _F32)
    for b in range(_N_BUCKETS):
        out = jnp.where(bucket == b, table[b].reshape(col_shape), out)
    return out


def _topk_block_mask(score, blk, n_valid):
    rank = jnp.zeros(score.shape, jnp.int32)
    for m in range(score.shape[0]):
        sm = score[m:m + 1, :]
        counts = jnp.where(m < n_valid, 1, 0)
        rank = rank + jnp.where(sm > score, counts, 0) + jnp.where(sm == score, jnp.where(m < blk, counts, 0), 0)
    return (blk < n_valid) & (rank < _MOBA_TOPK)


def _moba_prompt_kernel(far_ref, q_ref, k_ref, vt_ref, bias_ref, o_ref, kmean_ref, pen_ref, s_buf, p_buf,
                        m_ref, l_ref, alpha_ref, acc_ref, *, n_blk):
    blk_len = _MOBA_BLOCK
    n_query = q_ref.shape[1]
    n_qb = n_query // blk_len
    hp = pl.program_id(1)
    g = pl.program_id(2)
    half = _HEAD_DIM
    owns = [n_qb * g + i for i in range(n_qb)]
    qcols = [slice(i * blk_len, (i + 1) * blk_len) for i in range(n_qb)]
    n_far = owns[0] // 2

    @pl.when(g == 0)
    def _():
        for j in range(n_blk):
            kj = k_ref[0, j * blk_len:(j + 1) * blk_len, :].astype(_F32)
            kmean_ref[j:j + 1, :] = jnp.mean(kj, axis=0, keepdims=True)

    qf = q_ref[0].astype(_F32)
    lane = lax.broadcasted_iota(jnp.int32, qf.shape, 1)
    blk = lax.broadcasted_iota(jnp.int32, (n_blk, n_query), 0)
    own_q = owns[0] + lax.broadcasted_iota(jnp.int32, (1, n_query), 1) // blk_len
    kmean = kmean_ref[...].astype(_BF16)

    qhs, adj_pens = [], []
    for hd in range(2):
        qh = jnp.where((lane >= hd * half) & (lane < (hd + 1) * half), qf, 0.0).astype(_BF16)
        sel = _topk_block_mask(_nt_dot(kmean, qh), blk, own_q)
        pen_ref[hd] = jnp.where(sel & (blk < own_q - 1), far_ref[hp * 2 + hd], _NEG_INF)
        sel_adj = jnp.sum(jnp.where(sel & (blk == own_q - 1), 1.0, 0.0), axis=0, keepdims=True)
        adj_pens.append(jnp.where(own_q == 0, 0.0, jnp.where(sel_adj > 0.5, 0.0, _NEG_INF)))
        qhs.append(qh)

    log2e = math.log2(math.e)
    lo, hi = slice(0, blk_len), slice(blk_len, 2 * blk_len)

    def block_start(j):
        return pl.multiple_of(j * blk_len, blk_len)

    stage0_blocks = [(block_start(jnp.maximum(o - 1, 0)), block_start(jnp.maximum(o, 1))) for o in owns]

    def blocks_before_far_stage(t):
        far_a = (2 * t - 2) * blk_len
        return [(pl.multiple_of(jnp.where(t == 0, a0, far_a), blk_len),
                 pl.multiple_of(jnp.where(t == 0, b0, far_a + blk_len), blk_len)) for a0, b0 in stage0_blocks]

    def stage0_logits_into(slot):
        for hd in range(2):
            for qb in range(n_qb):
                q_rows = qhs[hd][qcols[qb], :]
                a0, b0 = stage0_blocks[qb]
                s_buf[slot, hd, lo, qcols[qb]] = _nt_dot(k_ref[0, pl.ds(a0, blk_len), :], q_rows)
                s_buf[slot, hd, hi, qcols[qb]] = _nt_dot(k_ref[0, pl.ds(b0, blk_len), :], q_rows)

    def far_logits_into(slot, t):
        start = pl.multiple_of(2 * t * blk_len, 2 * blk_len)
        for hd in range(2):
            s_buf[slot, hd] = _nt_dot(k_ref[0, pl.ds(start, 2 * blk_len), :], qhs[hd])

    def pv_from(slot, blocks):
        res = []
        for hd in range(2):
            rows = slice(hd * half, (hd + 1) * half)
            parts = [_dot(vt_ref[0, rows, pl.ds(a, blk_len)], p_buf[slot, hd, lo, qcols[qb]])
                     + _dot(vt_ref[0, rows, pl.ds(b, blk_len)], p_buf[slot, hd, hi, qcols[qb]])
                     for qb, (a, b) in enumerate(blocks)]
            res.append(jnp.concatenate(parts, axis=1))
        return res

    def softmax_stage(s_slot, p_slot, hd, cols, row_a, row_b, m_old, tiles=None):
        s_a, s_b = s_buf[s_slot, hd, lo, cols], s_buf[s_slot, hd, hi, cols]
        if tiles is not None:
            s_a, s_b = s_a + tiles[0], s_b + tiles[1]
        m_new = jnp.maximum(jnp.max(s_a, axis=0, keepdims=True) + row_a, jnp.max(s_b, axis=0, keepdims=True) + row_b)
        if m_old is not None:
            m_new = jnp.maximum(m_new, m_old)
        p_a = jnp.exp2(s_a * log2e + (row_a - m_new) * log2e)
        p_b = jnp.exp2(s_b * log2e + (row_b - m_new) * log2e)
        p_buf[p_slot, hd, lo, cols] = p_a.astype(_BF16)
        p_buf[p_slot, hd, hi, cols] = p_b.astype(_BF16)
        return m_new, jnp.sum(p_a, axis=0, keepdims=True) + jnp.sum(p_b, axis=0, keepdims=True)

    stage0_logits_into(0)
    far_logits_into(1, 0)
    for hd in range(2):
        for qb in range(n_qb):
            first = owns[qb] == 0
            tiles = (bias_ref[hd, jnp.where(first, 0, 1)], bias_ref[hd, 0])
            m0, l0 = softmax_stage(0, 0, hd, qcols[qb], adj_pens[hd][:, qcols[qb]],
                                   jnp.where(first, _NEG_INF, 0.0), None, tiles=tiles)
            m_ref[hd, :, qcols[qb]] = m0
            l_ref[hd, :, qcols[qb]] = l0
        alpha_ref[hd] = jnp.ones((1, n_query), _F32)
        acc_ref[hd] = jnp.zeros((half, n_query), _F32)

    def far_stage(t, slot):
        pv = pv_from(slot, blocks_before_far_stage(t))
        for hd in range(2):
            acc_ref[hd] = alpha_ref[hd] * acc_ref[hd] + pv[hd]
            for cols in qcols:
                m_old = m_ref[hd, :, cols]
                m_new, l_blk = softmax_stage(1 - slot, 1 - slot, hd, cols, pen_ref[hd, pl.ds(2 * t, 1), cols],
                                             pen_ref[hd, pl.ds(2 * t + 1, 1), cols], m_old)
                alpha = jnp.exp(m_old - m_new)
                l_ref[hd, :, cols] = alpha * l_ref[hd, :, cols] + l_blk
                m_ref[hd, :, cols] = m_new
                alpha_ref[hd, :, cols] = alpha
        far_logits_into(slot, t + 1)

    def far_stage_by_parity(t, carry):
        for slot in range(2):
            pl.when(t % 2 == slot)(functools.partial(far_stage, t, slot))
        return carry

    lax.fori_loop(0, n_far, far_stage_by_parity, 0)

    def finish(slot):
        pv = pv_from(slot, blocks_before_far_stage(n_far))
        outs = [(alpha_ref[hd] * acc_ref[hd] + pv[hd]) * (1.0 / l_ref[hd]) for hd in range(2)]
        o_ref[0] = jnp.concatenate(outs, axis=0).T.astype(_BF16)

    for slot in range(2):
        pl.when(n_far % 2 == slot)(functools.partial(finish, slot))


def _moba_prompt(q, kb, vt, rel_bias):
    nb, s, attn_dim = q.shape
    n_blk = s // _MOBA_BLOCK
    assert n_blk >= 2 and n_blk % 2 == 0 and _QUERY_BLOCKS_PER_STEP in (1, 2)
    n_pair = attn_dim // _LANES
    n_query = _QUERY_BLOCKS_PER_STEP * _MOBA_BLOCK
    kk = jnp.arange(_MOBA_BLOCK, dtype=jnp.int32)[:, None]
    qq = jnp.arange(_MOBA_BLOCK, dtype=jnp.int32)[None, :]
    table = rel_bias.astype(_F32)
    tiles = _bias_of_dist(table, jnp.stack([qq - kk, _MOBA_BLOCK + qq - kk]))
    far = _bias_of_dist(table, jnp.full((1,), _MOBA_BLOCK + 1, jnp.int32))[:, 0]
    return pl.pallas_call(
        functools.partial(_moba_prompt_kernel, n_blk=n_blk),
        grid=(nb, n_pair, s // n_query),
        in_specs=[pl.BlockSpec(memory_space=pltpu.SMEM),
                  pl.BlockSpec((1, n_query, _LANES), lambda b, h, c: (b, c, h)),
                  pl.BlockSpec((1, s, _LANES), lambda b, h, c: (b, 0, h)),
                  pl.BlockSpec((1, _LANES, s), lambda b, h, c: (b, h, 0)),
                  pl.BlockSpec((2, 2, _MOBA_BLOCK, _MOBA_BLOCK), lambda b, h, c: (h, 0, 0, 0))],
        out_specs=pl.BlockSpec((1, n_query, _LANES), lambda b, h, c: (b, c, h)),
        out_shape=jax.ShapeDtypeStruct((nb, s, attn_dim), _BF16),
        scratch_shapes=[pltpu.VMEM((n_blk, _LANES), _F32),
                        pltpu.VMEM((2, n_blk, n_query), _F32),
                        pltpu.VMEM((2, 2, 2 * _MOBA_BLOCK, n_query), _F32),
                        pltpu.VMEM((2, 2, 2 * _MOBA_BLOCK, n_query), _BF16),
                        pltpu.VMEM((2, 1, n_query), _F32), pltpu.VMEM((2, 1, n_query), _F32),
                        pltpu.VMEM((2, 1, n_query), _F32),
                        pltpu.VMEM((2, _HEAD_DIM, n_query), _F32)],
        compiler_params=_params(3),
        name="moba_prompt",
    )(far, q, kb, vt, tiles)


def _moba_step_kernel(pt_ref, qbd_ref, knew_ref, vnew_ref, badj_ref, bfar_ref, bown_ref, *rest, n_pages, n_q):
    del pt_ref
    kt_pages = rest[:n_pages]
    vt_pages = rest[n_pages:2 * n_pages]
    o_ref = rest[2 * n_pages]
    ppb = _MOBA_BLOCK // _PAGE_SIZE
    n_blk = n_pages // ppb
    qbd = qbd_ref[0]
    feat = qbd.shape[1]

    def block_of(pages, j):
        return jnp.concatenate([pages[p][0] for p in range(j * ppb, (j + 1) * ppb)], axis=1).astype(_BF16)

    s_blk = [_dot(qbd, block_of(kt_pages, j)) for j in range(n_blk)]
    score = [jnp.sum(s, axis=1, keepdims=True) for s in s_blk]
    logits = []
    for n in range(n_blk):
        rank = jnp.zeros(score[n].shape, jnp.int32)
        for m in range(n_blk):
            if m < n:
                rank = rank + jnp.where(score[m] >= score[n], 1, 0)
            elif m > n:
                rank = rank + jnp.where(score[m] > score[n], 1, 0)
        bias = badj_ref[...] if n == n_blk - 1 else bfar_ref[:, 0:1]
        logits.append(jnp.where(rank < _MOBA_TOPK, s_blk[n] + bias, _NEG_INF))
    s_own = _nt_dot(qbd, knew_ref[0].astype(_BF16)) + bown_ref[:, 0:_NEW_ROWS]
    m_row = jnp.max(s_own, axis=1, keepdims=True)
    for lg in logits:
        m_row = jnp.maximum(m_row, jnp.max(lg, axis=1, keepdims=True))
    p_own = jnp.exp(s_own - m_row)
    probs = [jnp.exp(lg - m_row) for lg in logits]
    denom = jnp.sum(p_own, axis=1, keepdims=True)
    for p in probs:
        denom = denom + jnp.sum(p, axis=1, keepdims=True)
    inv = 1.0 / denom
    out = _dot((p_own * inv).astype(_BF16), vnew_ref[0].astype(_BF16))
    for j in range(n_blk):
        out = out + _nt_dot((probs[j] * inv).astype(_BF16), block_of(vt_pages, j))
    r_head = lax.broadcasted_iota(jnp.int32, out.shape, 0) % _HEAD_GROUP
    c_head = lax.broadcasted_iota(jnp.int32, out.shape, 1) // _HEAD_DIM
    out = jnp.where(r_head == c_head, out, 0.0)
    o_ref[0] = jnp.sum(out.reshape(n_q, _HEAD_GROUP, feat), axis=1)


def _block_diag_queries(q, n_heads, head_dim, group):
    n, n_q, feat = q.shape
    r_head = jnp.arange(group, dtype=jnp.int32)[:, None]
    c_head = jnp.arange(feat, dtype=jnp.int32)[None, :] // head_dim
    mask = (r_head == c_head) & (r_head < n_heads)
    return jnp.where(mask[None, None], q[:, :, None, :], jnp.zeros((), q.dtype)).reshape(n, n_q * group, feat)


def _moba_step(page_table, q, k_new, v_new, cache_kt, cache_vt, rel_bias):
    n_seq, n_q, feat = q.shape
    n_pages = page_table.shape[1]
    past = n_pages * _PAGE_SIZE
    assert past % _MOBA_BLOCK == 0 and n_q <= _NEW_ROWS
    rows = n_q * _HEAD_GROUP
    qbd = _block_diag_queries(q, _N_HEADS, _HEAD_DIM, _HEAD_GROUP)
    padn = ((0, 0), (0, _NEW_ROWS - n_q), (0, 0))
    k_new = jnp.pad(k_new, padn)
    v_new = jnp.pad(v_new, padn)
    table = jnp.tile(jnp.pad(rel_bias.astype(_F32), ((0, 0), (0, _HEAD_GROUP - _N_HEADS))), (1, n_q))
    qi = jnp.arange(rows, dtype=jnp.int32)[:, None] // _HEAD_GROUP
    kk = jnp.arange(_MOBA_BLOCK, dtype=jnp.int32)[None, :]
    badj = _bias_of_dist(table, _MOBA_BLOCK + qi - kk, per_row=True)
    bfar = _bias_of_dist(table, jnp.full((rows, _LANES), _MOBA_BLOCK + 1, jnp.int32), per_row=True)
    kj = jnp.arange(_LANES, dtype=jnp.int32)[None, :]
    bown = _bias_of_dist(table, jnp.where(kj < n_q, qi - kj, -1), per_row=True)

    page = lambda p: pl.BlockSpec((1, feat, _PAGE_SIZE), lambda i, pt, p=p: (pt[i, p], 0, 0))
    seq3 = lambda r: pl.BlockSpec((1, r, feat), lambda i, pt: (i, 0, 0))
    const2 = lambda shape: pl.BlockSpec(shape, lambda i, pt: (0, 0), pipeline_mode=pl.Buffered(1))
    grid_spec = pltpu.PrefetchScalarGridSpec(
        num_scalar_prefetch=1,
        grid=(n_seq,),
        in_specs=[seq3(rows), seq3(_NEW_ROWS), seq3(_NEW_ROWS), const2((rows, _MOBA_BLOCK)),
                  const2((rows, _LANES)), const2((rows, _LANES))] + [page(p) for p in range(n_pages)] * 2,
        out_specs=pl.BlockSpec((1, n_q, feat), lambda i, pt: (i, 0, 0)))
    return pl.pallas_call(
        functools.partial(_moba_step_kernel, n_pages=n_pages, n_q=n_q),
        grid_spec=grid_spec,
        out_shape=jax.ShapeDtypeStruct((n_seq, n_q, feat), _F32),
        compiler_params=_params(1),
        name="moba_step",
    )(page_table, qbd, k_new, v_new, badj, bfar, bown, *([cache_kt] * n_pages), *([cache_vt] * n_pages))


def _softmax_rows(s):
    p = jnp.exp(s - jnp.max(s, axis=1, keepdims=True))
    return p * (1.0 / jnp.sum(p, axis=1, keepdims=True))


def _memattn_kernel(qm_ref, mk_ref, mv_ref, o_ref):
    scale = _MEM_HEAD_DIM ** -0.5
    for h in range(_MEM_HEADS):
        hs = slice(h * _MEM_HEAD_DIM, (h + 1) * _MEM_HEAD_DIM)
        p = _softmax_rows(_nt_dot(qm_ref[0, :, hs], mk_ref[0, :, hs]) * scale)
        o_ref[0, :, hs] = _dot(p.astype(_BF16), mv_ref[0, :, hs]).astype(_BF16)


def _memattn(qm, mkb, mvb, tq):
    nb, s, md = qm.shape
    m = mkb.shape[1]
    return pl.pallas_call(
        _memattn_kernel,
        grid=(nb, s // tq),
        in_specs=[pl.BlockSpec((1, tq, md), lambda b, i: (b, i, 0)),
                  pl.BlockSpec((1, m, md), lambda b, i: (b, 0, 0)),
                  pl.BlockSpec((1, m, md), lambda b, i: (b, 0, 0))],
        out_specs=pl.BlockSpec((1, tq, md), lambda b, i: (b, i, 0)),
        out_shape=jax.ShapeDtypeStruct((nb, s, md), _BF16),
        compiler_params=_params(2),
        name="memattn",
    )(qm, mkb, mvb)


def _memattn_step_kernel(q_ref, mk_ref, mv_ref, o_ref):
    scale = _MEM_HEAD_DIM ** -0.5
    for i in range(mk_ref.shape[0]):
        s = _nt_dot(q_ref[i], mk_ref[i].astype(_BF16)) * scale
        r_head = lax.broadcasted_iota(jnp.int32, s.shape, 0) % _MEM_HEADS
        c_head = lax.broadcasted_iota(jnp.int32, s.shape, 1) % _MEM_HEADS
        p = _softmax_rows(jnp.where(r_head == c_head, s, _NEG_INF))
        o_ref[i] = _dot(p.astype(_BF16), mv_ref[i].astype(_BF16))


def _memattn_step(qm, mem_k, mem_v, seqs_per_step):
    n_seq, rows, dh = qm.shape
    mh = mem_k.shape[1]
    blk = lambda i: (i, 0, 0)
    return pl.pallas_call(
        _memattn_step_kernel,
        grid=(n_seq // seqs_per_step,),
        in_specs=[pl.BlockSpec((seqs_per_step, rows, dh), blk), pl.BlockSpec((seqs_per_step, mh, dh), blk),
                  pl.BlockSpec((seqs_per_step, mh, dh), blk)],
        out_specs=pl.BlockSpec((seqs_per_step, rows, dh), blk),
        out_shape=jax.ShapeDtypeStruct((n_seq, rows, dh), _F32),
        compiler_params=_params(1),
        name="memattn_step",
    )(qm, mem_k, mem_v)


def _merge_kernel(h_ref, c_ref, o_ref, om_ref, gate_ref, wc_ref, wa_ref, wm_ref, wo_ref, gpost_ref,
                  g2pre_ref, w2in_ref, w2out_ref, g2post_ref, y_ref, *, n_chunk):
    d = h_ref.shape[1]
    merged = (gate_ref[:, 0:d].astype(_F32) * _dot(c_ref[...], wc_ref[...])
              + gate_ref[:, d:2 * d].astype(_F32) * _dot(o_ref[...], wa_ref[...])
              + gate_ref[:, 2 * d:3 * d].astype(_F32) * _dot(om_ref[...], wm_ref[...]))
    h = h_ref[...] + _rms_norm(_dot(merged.astype(_BF16), wo_ref[...]), gpost_ref[...])
    y_ref[...] = _ffn_half_step(h, g2pre_ref[...], w2in_ref, w2out_ref, g2post_ref[...], n_chunk)


def _merge(h, c_act, o, om, gates, w_conv_out, w_attn_out, w_mem_out, w_out, g_post, g2_pre, w2_in, w2_out,
           g2_post, tm):
    n, d = h.shape
    row = lambda i: (i, 0)
    tile = lambda a: pl.BlockSpec((tm, a.shape[1]), row)
    return pl.pallas_call(
        functools.partial(_merge_kernel, n_chunk=_ffn_chunks(w2_out.shape[0])),
        grid=(n // tm,),
        in_specs=[tile(h), tile(c_act), tile(o), tile(om), tile(gates),
                  _resident(w_conv_out.shape), _resident(w_attn_out.shape), _resident(w_mem_out.shape),
                  _resident(w_out.shape), _resident((1, d)), _resident((1, d)), _resident(w2_in.shape),
                  _resident(w2_out.shape), _resident((1, d))],
        out_specs=pl.BlockSpec((tm, d), row),
        out_shape=jax.ShapeDtypeStruct((n, d), _F32),
        compiler_params=_params(1),
        name="merge_ffn2",
    )(h, c_act, o, om, gates, w_conv_out, w_attn_out, w_mem_out, w_out, g_post, g2_pre, w2_in, w2_out, g2_post)


def _tile(n, want):
    t = min(n, want)
    assert n % t == 0
    return t


def _token_front(x, p):
    nb, s, d = x.shape
    n = nb * s
    h, u = _ffn1(x.reshape(n, d), p['g_ffn1_pre'], p['w_ffn1_in'], p['w_ffn1_out'], p['g_ffn1_post'],
                 p['g_mix_pre'], _tile(n, 512))
    return h, u.reshape(nb, s, d)


def _token_back(h, c_act, o, om, gates, p):
    n = h.shape[0]
    flat = lambda a: a.reshape(n, a.shape[-1])
    return _merge(h, flat(c_act), flat(o), flat(om), flat(gates), p['w_conv_out'], p['w_attn_out'],
                  p['w_mem_out'], p['w_out'], p['g_mix_post'], p['g_ffn2_pre'], p['w_ffn2_in'], p['w_ffn2_out'],
                  p['g_ffn2_post'], _tile(n, 256))


def _dims(p):
    conv_dim = p['w_conv_out'].shape[0]
    attn_dim = p['w_attn_out'].shape[0]
    mem_dim = p['w_mem_out'].shape[0]
    return conv_dim, attn_dim, mem_dim


def _prompt_layer(x, mem, rel_bias, p):
    nb, s, d = x.shape
    conv_dim, attn_dim, mem_dim = _dims(p)
    h, u = _token_front(x, p)
    glu, q, k, v, qm, gates, kb, vt, c_act = _inproj(
        u, p['w_in'], p['b_gate'], _tile(s, 512), conv_dim, attn_dim, mem_dim,
        conv_params=(p['w_dwconv'], p['b_dwconv'], p['g_conv_ln'], p['b_conv_ln']))
    mk, mv, mkb, mvb = _memkv(mem, p['g_mem'], p['w_mem_kv'])
    o = _moba_prompt(q, kb, vt, rel_bias)
    om = _memattn(qm, mkb, mvb, _tile(s, 512))
    y = _token_back(h, c_act, o, om, gates, p).reshape(nb, s, d)
    conv_state = glu[:, s - (_CONV_WIDTH - 1):]
    heads = lambda a: a.reshape(nb, s, _N_HEADS, _HEAD_DIM)
    mheads = lambda a: a.reshape(nb, a.shape[1], _MEM_HEADS, _MEM_HEAD_DIM)
    return y, heads(k), heads(v), mheads(mk), mheads(mv), conv_state


def _sample_layer(x, conv_prev, cache_k, cache_v, page_table, mem_k, mem_v, rel_bias, p):
    n_seq, n_q, d = x.shape
    conv_dim, attn_dim, mem_dim = _dims(p)
    n = n_seq * n_q
    h, u = _token_front(x.reshape(1, n, d), p)
    glu, q, k, v, qm, gates = _inproj(u, p['w_in'], p['b_gate'], _tile(n, 512), conv_dim, attn_dim, mem_dim)
    per_seq = lambda a: a.reshape(n_seq, n_q, a.shape[-1])
    glu = per_seq(glu)
    conv_in = jnp.concatenate([conv_prev, glu], axis=1)
    c_act = _conv_step(conv_in.transpose(1, 0, 2), p['w_dwconv'], p['b_dwconv'], p['g_conv_ln'], p['b_conv_ln'])
    c_act = c_act.transpose(1, 0, 2)
    page_t = lambda c: c.transpose(0, 2, 3, 1).reshape(c.shape[0], attn_dim, c.shape[1])
    o = _moba_step(page_table, per_seq(q), per_seq(k), per_seq(v), page_t(cache_k), page_t(cache_v), rel_bias)
    head_rows = lambda a: a.reshape(n_seq, -1, _MEM_HEAD_DIM)
    om = _memattn_step(head_rows(qm), head_rows(mem_k), head_rows(mem_v), 8)
    y = _token_back(h, c_act, o.astype(_BF16), om.reshape(n, mem_dim).astype(_BF16), gates, p)
    conv_state = conv_in[:, n_q:]
    heads = lambda a: a.reshape(n_seq, n_q, _N_HEADS, _HEAD_DIM)
    return y.reshape(n_seq, n_q, d), heads(k), heads(v), conv_state


def _layer_params(l, g_ffn1_pre, w_ffn1_in, w_ffn1_out, g_ffn1_post, g_mix_pre, w_in, b_gate, w_dwconv, b_dwconv,
                  g_conv_ln, b_conv_ln, w_conv_out, w_attn_out, g_mem, w_mem_kv, w_mem_out, w_out, g_mix_post,
                  g_ffn2_pre, w_ffn2_in, w_ffn2_out, g_ffn2_post):
    vec = lambda a: a[l].astype(_F32).reshape(1, -1)
    mat = lambda a: a[l].astype(_BF16)
    return dict(
        g_ffn1_pre=vec(g_ffn1_pre), w_ffn1_in=mat(w_ffn1_in), w_ffn1_out=mat(w_ffn1_out),
        g_ffn1_post=vec(g_ffn1_post), g_mix_pre=vec(g_mix_pre), w_in=mat(w_in), b_gate=vec(b_gate),
        w_dwconv=w_dwconv[l].astype(_F32), b_dwconv=vec(b_dwconv), g_conv_ln=vec(g_conv_ln),
        b_conv_ln=vec(b_conv_ln), w_conv_out=mat(w_conv_out), w_attn_out=mat(w_attn_out), g_mem=vec(g_mem),
        w_mem_kv=mat(w_mem_kv), w_mem_out=mat(w_mem_out), w_out=mat(w_out), g_mix_post=vec(g_mix_post),
        g_ffn2_pre=vec(g_ffn2_pre), w_ffn2_in=mat(w_ffn2_in), w_ffn2_out=mat(w_ffn2_out),
        g_ffn2_post=vec(g_ffn2_post))


def kernel(x_prompt, x_sample, cache_k, cache_v, cache_mem_k, cache_mem_v, state_conv, page_table, mem_prompt, g_ffn1_pre, w_ffn1_in, w_ffn1_out, g_ffn1_post, g_mix_pre, w_in, b_gate, w_dwconv, b_dwconv, g_conv_ln, b_conv_ln, w_conv_out, w_attn_out, rel_bias, g_mem, w_mem_kv, w_mem_out, w_out, g_mix_post, g_ffn2_pre, w_ffn2_in, w_ffn2_out, g_ffn2_post):
    depth = w_in.shape[0]
    hp, hs = x_prompt, x_sample
    outs = [[] for _ in range(8)]
    for l in range(depth):
        p = _layer_params(l, g_ffn1_pre, w_ffn1_in, w_ffn1_out, g_ffn1_post, g_mix_pre, w_in, b_gate, w_dwconv,
                          b_dwconv, g_conv_ln, b_conv_ln, w_conv_out, w_attn_out, g_mem, w_mem_kv, w_mem_out,
                          w_out, g_mix_post, g_ffn2_pre, w_ffn2_in, w_ffn2_out, g_ffn2_post)
        hp, kp, vp, mkp, mvp, cp = _prompt_layer(hp, mem_prompt, rel_bias, p)
        hs, ks, vs, cs = _sample_layer(hs, state_conv[l], cache_k[l], cache_v[l], page_table, cache_mem_k[l],
                                       cache_mem_v[l], rel_bias, p)
        for acc, val in zip(outs, (kp, vp, mkp, mvp, cp, ks, vs, cs)):
            acc.append(val)
    return (hp, hs) + tuple(jnp.stack(a) for a in outs)
```

```python
import functools
import math

import jax
import jax.numpy as jnp
from jax import lax
from jax.experimental import pallas as pl
from jax.experimental.pallas import tpu as pltpu

_F32 = jnp.float32
_BF16 = jnp.bfloat16

_CONV_WIDTH = 31
_N_HEADS = 12
_HEAD_DIM = 64
_MOBA_BLOCK = 256
_MOBA_TOPK = 3
_MEM_HEADS = 4
_MEM_HEAD_DIM = 128
_N_BUCKETS = 32
_MAX_DISTANCE = 128
_N_BRANCH = 3
_PAGE_SIZE = 128
_NORM_EPS = 1e-6
_NEG_INF = -1e30

_LANES = 128
_SUBLANES = 8
_VMEM_BYTES_V7X = 64 * 1024 * 1024
_VMEM_LIMIT = _VMEM_BYTES_V7X - 8 * 1024 * 1024

_HALO_ROWS = 32
_CONV_CHUNK_ROWS = 64
_HEAD_GROUP = 16
_NEW_ROWS = 16
_QUERY_BLOCKS_PER_STEP = 2


def _nt_dot(a, b):
    return lax.dot_general(a, b, (((1,), (1,)), ((), ())), preferred_element_type=_F32)


def _dot(a, b):
    return jnp.dot(a, b, preferred_element_type=_F32)


def _sigmoid(x):
    return 1.0 / (1.0 + jnp.exp(-x))


def _rms_norm(x, g):
    return x * lax.rsqrt(jnp.mean(x * x, axis=-1, keepdims=True) + _NORM_EPS) * g


def _resident(shape):
    nd = len(shape)
    return pl.BlockSpec(shape, lambda *_: (0,) * nd, pipeline_mode=pl.Buffered(1))


def _params(n_parallel_axes):
    return pltpu.CompilerParams(
        dimension_semantics=("arbitrary",) * n_parallel_axes,
        vmem_limit_bytes=_VMEM_LIMIT)


def _ffn_half_step(x, g_pre, w_in_ref, w_out_ref, g_post, n_chunk):
    d_ff = w_out_ref.shape[0]
    fc = d_ff // n_chunk
    xn = _rms_norm(x, g_pre).astype(_BF16)
    acc = None
    for c in range(n_chunk):
        a = _dot(xn, w_in_ref[:, c * fc:(c + 1) * fc])
        b = _dot(xn, w_in_ref[:, d_ff + c * fc:d_ff + (c + 1) * fc])
        hid = (a * _sigmoid(a) * b).astype(_BF16)
        part = _dot(hid, w_out_ref[c * fc:(c + 1) * fc, :])
        acc = part if acc is None else acc + part
    return x + 0.5 * _rms_norm(acc, g_post)


def _ffn1_kernel(x_ref, gpre_ref, win_ref, wout_ref, gpost_ref, gmix_ref, h_ref, u_ref, *, n_chunk):
    h = _ffn_half_step(x_ref[...], gpre_ref[...], win_ref, wout_ref, gpost_ref[...], n_chunk)
    h_ref[...] = h
    u_ref[...] = _rms_norm(h, gmix_ref[...]).astype(_BF16)


def _ffn_chunks(d_ff):
    for n in (2, 1):
        if d_ff % (n * _LANES) == 0:
            return n
    return 1


def _ffn1(x, g_pre, w_in, w_out, g_post, g_mix, tm):
    n, d = x.shape
    d_ff = w_out.shape[0]
    row = lambda i: (i, 0)
    return pl.pallas_call(
        functools.partial(_ffn1_kernel, n_chunk=_ffn_chunks(d_ff)),
        grid=(n // tm,),
        in_specs=[pl.BlockSpec((tm, d), row), _resident((1, d)), _resident(w_in.shape),
                  _resident(w_out.shape), _resident((1, d)), _resident((1, d))],
        out_specs=[pl.BlockSpec((tm, d), row), pl.BlockSpec((tm, d), row)],
        out_shape=[jax.ShapeDtypeStruct((n, d), _F32), jax.ShapeDtypeStruct((n, d), _BF16)],
        compiler_params=_params(1),
        name="ffn1",
    )(x, g_pre, w_in, w_out, g_post, g_mix)


def _inproj_kernel(u_ref, w_ref, bg_ref, *rest, conv_dim, attn_dim, mem_dim, whole_sequences):
    if whole_sequences:
        wd_ref, bd_ref, gln_ref, bln_ref = rest[:4]
        rest = rest[4:]
    conv_ref, q_ref, k_ref, v_ref, qm_ref, gate_ref = rest[:6]
    u = u_ref[0]
    tm = u.shape[0]
    o1 = 2 * conv_dim
    o2 = o1 + attn_dim
    o3 = o2 + attn_dim
    o4 = o3 + attn_dim
    o5 = o4 + mem_dim

    def proj(lo, hi):
        return _dot(u, w_ref[:, lo:hi])

    if whole_sequences:
        kb_ref, vt_ref, cact_ref, ext_ref, cbuf_ref = rest[6:]
        @pl.when(pl.program_id(1) == 0)
        def _():
            ext_ref[0:_HALO_ROWS, :] = jnp.zeros((_HALO_ROWS, conv_dim), _F32)

        @pl.when(pl.program_id(1) > 0)
        def _():
            ext_ref[0:_HALO_ROWS, :] = ext_ref[tm:tm + _HALO_ROWS, :]

    glu = proj(0, conv_dim) * _sigmoid(proj(conv_dim, o1))
    conv_ref[0] = glu
    if whole_sequences:
        ext_ref[_HALO_ROWS:, :] = glu
        for r0 in range(0, tm, _CONV_CHUNK_ROWS):
            _conv_module_rows(ext_ref, cbuf_ref, cact_ref, r0, _CONV_CHUNK_ROWS, wd_ref, bd_ref, gln_ref, bln_ref)
    q_ref[0] = (proj(o1, o2) * (_HEAD_DIM ** -0.5)).astype(_BF16)
    k = proj(o2, o3)
    v = proj(o3, o4)
    k_ref[0] = k
    v_ref[0] = v
    qm_ref[0] = proj(o4, o5).astype(_BF16)
    gate_ref[0] = _sigmoid(proj(o5, w_ref.shape[1]) + bg_ref[...]).astype(_BF16)
    if whole_sequences:
        kb_ref[0] = k.astype(_BF16)
        vt_ref[0] = v.T.astype(_BF16)


def _inproj(u, w_in, b_gate, tm, conv_dim, attn_dim, mem_dim, conv_params=None):
    nb, s, d = u.shape
    n_gate = b_gate.shape[1]
    whole_sequences = conv_params is not None
    tok = lambda b, i: (b, i, 0)
    in_specs = [pl.BlockSpec((1, tm, d), tok), _resident(w_in.shape), _resident((1, n_gate))]
    out_specs = [pl.BlockSpec((1, tm, conv_dim), lambda b, i: (b, 0, 0)), pl.BlockSpec((1, tm, attn_dim), tok),
                 pl.BlockSpec((1, tm, attn_dim), tok), pl.BlockSpec((1, tm, attn_dim), tok),
                 pl.BlockSpec((1, tm, mem_dim), tok), pl.BlockSpec((1, tm, n_gate), tok)]
    out_shape = [jax.ShapeDtypeStruct((nb, tm, conv_dim), _F32), jax.ShapeDtypeStruct((nb, s, attn_dim), _BF16),
                 jax.ShapeDtypeStruct((nb, s, attn_dim), _F32), jax.ShapeDtypeStruct((nb, s, attn_dim), _F32),
                 jax.ShapeDtypeStruct((nb, s, mem_dim), _BF16), jax.ShapeDtypeStruct((nb, s, n_gate), _BF16)]
    scratch_shapes = []
    operands = [u, w_in, b_gate]
    if whole_sequences:
        assert tm % _CONV_CHUNK_ROWS == 0
        in_specs += [_resident(a.shape) for a in conv_params]
        operands += list(conv_params)
        out_specs += [pl.BlockSpec((1, tm, attn_dim), tok), pl.BlockSpec((1, attn_dim, tm), lambda b, i: (b, 0, i)),
                      pl.BlockSpec((1, tm, conv_dim), tok)]
        out_shape += [jax.ShapeDtypeStruct((nb, s, attn_dim), _BF16), jax.ShapeDtypeStruct((nb, attn_dim, s), _BF16),
                      jax.ShapeDtypeStruct((nb, s, conv_dim), _BF16)]
        scratch_shapes = [pltpu.VMEM((tm + _HALO_ROWS, conv_dim), _F32), pltpu.VMEM((tm, conv_dim), _F32)]
    return pl.pallas_call(
        functools.partial(_inproj_kernel, conv_dim=conv_dim, attn_dim=attn_dim, mem_dim=mem_dim,
                          whole_sequences=whole_sequences),
        grid=(nb, s // tm),
        in_specs=in_specs, out_specs=out_specs, out_shape=out_shape, scratch_shapes=scratch_shapes,
        compiler_params=_params(2),
        name="inproj",
    )(*operands)


def _memkv_kernel(mem_ref, g_ref, w_ref, mk_ref, mv_ref, mkb_ref, mvb_ref):
    kv = _dot(_rms_norm(mem_ref[0], g_ref[...]).astype(_BF16), w_ref[...])
    half = kv.shape[1] // 2
    mk_ref[0] = kv[:, :half]
    mv_ref[0] = kv[:, half:]
    mkb_ref[0] = kv[:, :half].astype(_BF16)
    mvb_ref[0] = kv[:, half:].astype(_BF16)


def _memkv(mem, g_mem, w_mem_kv):
    nb, m, d = mem.shape
    md = w_mem_kv.shape[1] // 2
    blk = lambda b: (b, 0, 0)
    return pl.pallas_call(
        _memkv_kernel,
        grid=(nb,),
        in_specs=[pl.BlockSpec((1, m, d), blk), _resident((1, d)), _resident(w_mem_kv.shape)],
        out_specs=[pl.BlockSpec((1, m, md), blk)] * 4,
        out_shape=[jax.ShapeDtypeStruct((nb, m, md), _F32)] * 2 + [jax.ShapeDtypeStruct((nb, m, md), _BF16)] * 2,
        compiler_params=_params(1),
        name="memkv",
    )(mem, g_mem, w_mem_kv)


def _ln_swish(c, g, b):
    mu = jnp.mean(c, axis=-1, keepdims=True)
    cc = c - mu
    var = jnp.mean(cc * cc, axis=-1, keepdims=True)
    y = cc * lax.rsqrt(var + _NORM_EPS) * g + b
    return y * _sigmoid(y)


def _conv_module_rows(ext_ref, cbuf_ref, out_ref, r0, rows, wd_ref, bd_ref, g_ref, b_ref):
    ch = ext_ref.shape[1]
    lead = _HALO_ROWS - (_CONV_WIDTH - 1)
    n_vreg = rows // _SUBLANES
    n_win = n_vreg + _HALO_ROWS // _SUBLANES
    sub = lax.broadcasted_iota(jnp.int32, (1, _SUBLANES, _LANES), 1)
    blocks = [slice(lb * _LANES, (lb + 1) * _LANES) for lb in range(ch // _LANES)]
    rs = slice(r0, r0 + rows)
    total = None
    for ls in blocks:
        win = ext_ref[r0:r0 + rows + _HALO_ROWS, ls].reshape(n_win, _SUBLANES, _LANES)
        acc = jnp.broadcast_to(bd_ref[:, ls], (n_vreg, _SUBLANES, _LANES))
        for r in range(_SUBLANES):
            n_a = (_CONV_WIDTH - r + _SUBLANES - 1) // _SUBLANES
            base, k = divmod(lead + r, _SUBLANES)
            need = n_vreg + n_a - 1
            if k == 0:
                sh = win[base:base + need]
            else:
                rot = pltpu.roll(win[base:base + need + 1], _SUBLANES - k, axis=1)
                sh = jnp.where(sub < _SUBLANES - k, rot[:-1], rot[1:])
            for a in range(n_a):
                w = _SUBLANES * a + r
                acc = acc + sh[a:a + n_vreg] * wd_ref[w:w + 1, ls]
        acc = acc.reshape(rows, _LANES)
        cbuf_ref[rs, ls] = acc
        total = acc if total is None else total + acc
    mu = jnp.sum(total, axis=-1, keepdims=True) * (1.0 / ch)
    sq = None
    for ls in blocks:
        d = cbuf_ref[rs, ls] - mu
        sq = d * d if sq is None else sq + d * d
    rstd = lax.rsqrt(jnp.sum(sq, axis=-1, keepdims=True) * (1.0 / ch) + _NORM_EPS)
    for ls in blocks:
        y = (cbuf_ref[rs, ls] - mu) * rstd * g_ref[:, ls] + b_ref[:, ls]
        out_ref[0, rs, ls] = (y * _sigmoid(y)).astype(_BF16)


def _conv_step_kernel(ext_ref, wd_ref, bd_ref, g_ref, b_ref, out_ref):
    n_q = out_ref.shape[0]
    for t in range(n_q):
        acc = jnp.broadcast_to(bd_ref[...], ext_ref.shape[1:])
        for w in range(_CONV_WIDTH):
            acc = acc + ext_ref[t + w] * wd_ref[w:w + 1, :]
        out_ref[t] = _ln_swish(acc, g_ref[...], b_ref[...]).astype(_BF16)


def _conv_step(ext_tm, w_dw, b_dw, g_ln, b_ln):
    n_rows, n_seq, ch = ext_tm.shape
    n_q = n_rows - (_CONV_WIDTH - 1)
    return pl.pallas_call(
        _conv_step_kernel,
        grid=(1,),
        in_specs=[_resident(ext_tm.shape), _resident(w_dw.shape), _resident((1, ch)), _resident((1, ch)),
                  _resident((1, ch))],
        out_specs=pl.BlockSpec((n_q, n_seq, ch), lambda i: (0, 0, 0)),
        out_shape=jax.ShapeDtypeStruct((n_q, n_seq, ch), _BF16),
        compiler_params=_params(1),
        name="conv_step",
    )(ext_tm, w_dw, b_dw, g_ln, b_ln)


def _rel_bucket(dist):
    n = jnp.maximum(dist, 0)
    max_exact = _N_BUCKETS // 2
    nf = jnp.maximum(n, 1).astype(_F32)
    large = max_exact + (jnp.log(nf / max_exact) / math.log(_MAX_DISTANCE / max_exact)
                         * (_N_BUCKETS - max_exact)).astype(jnp.int32)
    large = jnp.minimum(large, _N_BUCKETS - 1)
    return jnp.where(n < max_exact, n, large)


def _bias_of_dist(table, dist, per_row=False):
    n_col = table.shape[1]
    bucket = jnp.where(dist >= 0, _rel_bucket(dist), -1)
    if per_row:
        col_shape = (n_col,) + (1,) * (dist.ndim - 1)
    else:
        bucket = bucket[None]
        col_shape = (n_col,) + (1,) * dist.ndim
    out = jnp.full(jnp.broadcast_shapes(bucket.shape, col_shape), _NEG_INF, _F32)
    for b in range(_N_BUCKETS):
        out = jnp.where(bucket == b, table[b].reshape(col_shape), out)
    return out


def _topk_block_mask(score, blk, n_valid):
    rank = jnp.zeros(score.shape, jnp.int32)
    for m in range(score.shape[0]):
        sm = score[m:m + 1, :]
        counts = jnp.where(m < n_valid, 1, 0)
        rank = rank + jnp.where(sm > score, counts, 0) + jnp.where(sm == score, jnp.where(m < blk, counts, 0), 0)
    return (blk < n_valid) & (rank < _MOBA_TOPK)


def _moba_prompt_kernel(far_ref, q_ref, k_ref, vt_ref, bias_ref, o_ref, kmean_ref, pen_ref, s_buf, p_buf,
                        m_ref, l_ref, alpha_ref, acc_ref, *, n_blk):
    blk_len = _MOBA_BLOCK
    n_query = q_ref.shape[1]
    n_qb = n_query // blk_len
    hp = pl.program_id(1)
    g = pl.program_id(2)
    half = _HEAD_DIM
    owns = [n_qb * g + i for i in range(n_qb)]
    qcols = [slice(i * blk_len, (i + 1) * blk_len) for i in range(n_qb)]
    n_far = owns[0] // 2

    @pl.when(g == 0)
    def _():
        for j in range(n_blk):
            kj = k_ref[0, j * blk_len:(j + 1) * blk_len, :].astype(_F32)
            kmean_ref[j:j + 1, :] = jnp.mean(kj, axis=0, keepdims=True)

    qf = q_ref[0].astype(_F32)
    lane = lax.broadcasted_iota(jnp.int32, qf.shape, 1)
    blk = lax.broadcasted_iota(jnp.int32, (n_blk, n_query), 0)
    own_q = owns[0] + lax.broadcasted_iota(jnp.int32, (1, n_query), 1) // blk_len
    kmean = kmean_ref[...].astype(_BF16)

    qhs, adj_pens = [], []
    for hd in range(2):
        qh = jnp.where((lane >= hd * half) & (lane < (hd + 1) * half), qf, 0.0).astype(_BF16)
        sel = _topk_block_mask(_nt_dot(kmean, qh), blk, own_q)
        pen_ref[hd] = jnp.where(sel & (blk < own_q - 1), far_ref[hp * 2 + hd], _NEG_INF)
        sel_adj = jnp.sum(jnp.where(sel & (blk == own_q - 1), 1.0, 0.0), axis=0, keepdims=True)
        adj_pens.append(jnp.where(own_q == 0, 0.0, jnp.where(sel_adj > 0.5, 0.0, _NEG_INF)))
        qhs.append(qh)

    log2e = math.log2(math.e)
    lo, hi = slice(0, blk_len), slice(blk_len, 2 * blk_len)

    def block_start(j):
        return pl.multiple_of(j * blk_len, blk_len)

    stage0_blocks = [(block_start(jnp.maximum(o - 1, 0)), block_start(jnp.maximum(o, 1))) for o in owns]

    def blocks_before_far_stage(t):
        far_a = (2 * t - 2) * blk_len
        return [(pl.multiple_of(jnp.where(t == 0, a0, far_a), blk_len),
                 pl.multiple_of(jnp.where(t == 0, b0, far_a + blk_len), blk_len)) for a0, b0 in stage0_blocks]

    def stage0_logits_into(slot):
        for hd in range(2):
            for qb in range(n_qb):
                q_rows = qhs[hd][qcols[qb], :]
                a0, b0 = stage0_blocks[qb]
                s_buf[slot, hd, lo, qcols[qb]] = _nt_dot(k_ref[0, pl.ds(a0, blk_len), :], q_rows)
                s_buf[slot, hd, hi, qcols[qb]] = _nt_dot(k_ref[0, pl.ds(b0, blk_len), :], q_rows)

    def far_logits_into(slot, t, hd, cols):
        start = pl.multiple_of(2 * t * blk_len, 2 * blk_len)
        s = _nt_dot(k_ref[0, pl.ds(start, 2 * blk_len), :], qhs[hd][cols, :])
        s_buf[slot, hd, :, cols] = s
        bits = lax.shift_right_logical(pltpu.bitcast(s[0:_SUBLANES, :], jnp.uint32), jnp.uint32(32))
        return pltpu.bitcast(bits, _F32)[0:1, :]

    def pv_from(slot, blocks):
        res = []
        for hd in range(2):
            rows = slice(hd * half, (hd + 1) * half)
            parts = [_dot(vt_ref[0, rows, pl.ds(a, blk_len)], p_buf[slot, hd, lo, qcols[qb]])
                     + _dot(vt_ref[0, rows, pl.ds(b, blk_len)], p_buf[slot, hd, hi, qcols[qb]])
                     for qb, (a, b) in enumerate(blocks)]
            res.append(jnp.concatenate(parts, axis=1))
        return res

    def softmax_stage(s_slot, p_slot, hd, cols, row_a, row_b, m_old, tiles=None):
        s_a, s_b = s_buf[s_slot, hd, lo, cols], s_buf[s_slot, hd, hi, cols]
        if tiles is not None:
            s_a, s_b = s_a + tiles[0], s_b + tiles[1]
        m_new = jnp.maximum(jnp.max(s_a, axis=0, keepdims=True) + row_a, jnp.max(s_b, axis=0, keepdims=True) + row_b)
        if m_old is not None:
            m_new = jnp.maximum(m_new, m_old)
        p_a = jnp.exp2(s_a * log2e + (row_a - m_new) * log2e)
        p_b = jnp.exp2(s_b * log2e + (row_b - m_new) * log2e)
        p_buf[p_slot, hd, lo, cols] = p_a.astype(_BF16)
        p_buf[p_slot, hd, hi, cols] = p_b.astype(_BF16)
        return m_new, jnp.sum(p_a, axis=0, keepdims=True) + jnp.sum(p_b, axis=0, keepdims=True)

    stage0_logits_into(0)
    for hd in range(2):
        for cols in qcols:
            far_logits_into(1, 0, hd, cols)
    for hd in range(2):
        for qb in range(n_qb):
            first = owns[qb] == 0
            tiles = (bias_ref[hd, jnp.where(first, 0, 1)], bias_ref[hd, 0])
            m0, l0 = softmax_stage(0, 0, hd, qcols[qb], adj_pens[hd][:, qcols[qb]],
                                   jnp.where(first, _NEG_INF, 0.0), None, tiles=tiles)
            m_ref[hd, :, qcols[qb]] = m0
            l_ref[hd, :, qcols[qb]] = l0
        alpha_ref[hd] = jnp.ones((1, n_query), _F32)
        acc_ref[hd] = jnp.zeros((half, n_query), _F32)

    def far_stage(t, slot):
        pv = pv_from(slot, blocks_before_far_stage(t))
        for hd in range(2):
            acc_ref[hd] = alpha_ref[hd] * acc_ref[hd] + pv[hd]
            for cols in qcols:
                zero = far_logits_into(slot, t + 1, hd, cols)
                m_old = m_ref[hd, :, cols]
                m_new, l_blk = softmax_stage(1 - slot, 1 - slot, hd, cols, pen_ref[hd, pl.ds(2 * t, 1), cols] + zero,
                                             pen_ref[hd, pl.ds(2 * t + 1, 1), cols], m_old)
                alpha = jnp.exp(m_old - m_new)
                l_ref[hd, :, cols] = alpha * l_ref[hd, :, cols] + l_blk
                m_ref[hd, :, cols] = m_new
                alpha_ref[hd, :, cols] = alpha

    def far_stage_by_parity(t, carry):
        for slot in range(2):
            pl.when(t % 2 == slot)(functools.partial(far_stage, t, slot))
        return carry

    lax.fori_loop(0, n_far, far_stage_by_parity, 0)

    def finish(slot):
        pv = pv_from(slot, blocks_before_far_stage(n_far))
        outs = [(alpha_ref[hd] * acc_ref[hd] + pv[hd]) * (1.0 / l_ref[hd]) for hd in range(2)]
        o_ref[0] = jnp.concatenate(outs, axis=0).T.astype(_BF16)

    for slot in range(2):
        pl.when(n_far % 2 == slot)(functools.partial(finish, slot))


def _moba_prompt(q, kb, vt, rel_bias):
    nb, s, attn_dim = q.shape
    n_blk = s // _MOBA_BLOCK
    assert n_blk >= 2 and n_blk % 2 == 0 and _QUERY_BLOCKS_PER_STEP in (1, 2)
    n_pair = attn_dim // _LANES
    n_query = _QUERY_BLOCKS_PER_STEP * _MOBA_BLOCK
    kk = jnp.arange(_MOBA_BLOCK, dtype=jnp.int32)[:, None]
    qq = jnp.arange(_MOBA_BLOCK, dtype=jnp.int32)[None, :]
    table = rel_bias.astype(_F32)
    tiles = _bias_of_dist(table, jnp.stack([qq - kk, _MOBA_BLOCK + qq - kk]))
    far = _bias_of_dist(table, jnp.full((1,), _MOBA_BLOCK + 1, jnp.int32))[:, 0]
    return pl.pallas_call(
        functools.partial(_moba_prompt_kernel, n_blk=n_blk),
        grid=(nb, n_pair, s // n_query),
        in_specs=[pl.BlockSpec(memory_space=pltpu.SMEM),
                  pl.BlockSpec((1, n_query, _LANES), lambda b, h, c: (b, c, h)),
                  pl.BlockSpec((1, s, _LANES), lambda b, h, c: (b, 0, h)),
                  pl.BlockSpec((1, _LANES, s), lambda b, h, c: (b, h, 0)),
                  pl.BlockSpec((2, 2, _MOBA_BLOCK, _MOBA_BLOCK), lambda b, h, c: (h, 0, 0, 0))],
        out_specs=pl.BlockSpec((1, n_query, _LANES), lambda b, h, c: (b, c, h)),
        out_shape=jax.ShapeDtypeStruct((nb, s, attn_dim), _BF16),
        scratch_shapes=[pltpu.VMEM((n_blk, _LANES), _F32),
                        pltpu.VMEM((2, n_blk, n_query), _F32),
                        pltpu.VMEM((2, 2, 2 * _MOBA_BLOCK, n_query), _F32),
                        pltpu.VMEM((2, 2, 2 * _MOBA_BLOCK, n_query), _BF16),
                        pltpu.VMEM((2, 1, n_query), _F32), pltpu.VMEM((2, 1, n_query), _F32),
                        pltpu.VMEM((2, 1, n_query), _F32),
                        pltpu.VMEM((2, _HEAD_DIM, n_query), _F32)],
        compiler_params=_params(3),
        name="moba_prompt",
    )(far, q, kb, vt, tiles)


def _moba_step_kernel(pt_ref, qbd_ref, knew_ref, vnew_ref, badj_ref, bfar_ref, bown_ref, *rest, n_pages, n_q):
    del pt_ref
    kt_pages = rest[:n_pages]
    vt_pages = rest[n_pages:2 * n_pages]
    o_ref = rest[2 * n_pages]
    ppb = _MOBA_BLOCK // _PAGE_SIZE
    n_blk = n_pages // ppb
    qbd = qbd_ref[0]
    feat = qbd.shape[1]

    def block_of(pages, j):
        return jnp.concatenate([pages[p][0] for p in range(j * ppb, (j + 1) * ppb)], axis=1).astype(_BF16)

    s_blk = [_dot(qbd, block_of(kt_pages, j)) for j in range(n_blk)]
    score = [jnp.sum(s, axis=1, keepdims=True) for s in s_blk]
    logits = []
    for n in range(n_blk):
        rank = jnp.zeros(score[n].shape, jnp.int32)
        for m in range(n_blk):
            if m < n:
                rank = rank + jnp.where(score[m] >= score[n], 1, 0)
            elif m > n:
                rank = rank + jnp.where(score[m] > score[n], 1, 0)
        bias = badj_ref[...] if n == n_blk - 1 else bfar_ref[:, 0:1]
        logits.append(jnp.where(rank < _MOBA_TOPK, s_blk[n] + bias, _NEG_INF))
    s_own = _nt_dot(qbd, knew_ref[0].astype(_BF16)) + bown_ref[:, 0:_NEW_ROWS]
    m_row = jnp.max(s_own, axis=1, keepdims=True)
    for lg in logits:
        m_row = jnp.maximum(m_row, jnp.max(lg, axis=1, keepdims=True))
    p_own = jnp.exp(s_own - m_row)
    probs = [jnp.exp(lg - m_row) for lg in logits]
    denom = jnp.sum(p_own, axis=1, keepdims=True)
    for p in probs:
        denom = denom + jnp.sum(p, axis=1, keepdims=True)
    inv = 1.0 / denom
    out = _dot((p_own * inv).astype(_BF16), vnew_ref[0].astype(_BF16))
    for j in range(n_blk):
        out = out + _nt_dot((probs[j] * inv).astype(_BF16), block_of(vt_pages, j))
    r_head = lax.broadcasted_iota(jnp.int32, out.shape, 0) % _HEAD_GROUP
    c_head = lax.broadcasted_iota(jnp.int32, out.shape, 1) // _HEAD_DIM
    out = jnp.where(r_head == c_head, out, 0.0)
    o_ref[0] = jnp.sum(out.reshape(n_q, _HEAD_GROUP, feat), axis=1)


def _block_diag_queries(q, n_heads, head_dim, group):
    n, n_q, feat = q.shape
    r_head = jnp.arange(group, dtype=jnp.int32)[:, None]
    c_head = jnp.arange(feat, dtype=jnp.int32)[None, :] // head_dim
    mask = (r_head == c_head) & (r_head < n_heads)
    return jnp.where(mask[None, None], q[:, :, None, :], jnp.zeros((), q.dtype)).reshape(n, n_q * group, feat)


def _moba_step(page_table, q, k_new, v_new, cache_kt, cache_vt, rel_bias):
    n_seq, n_q, feat = q.shape
    n_pages = page_table.shape[1]
    past = n_pages * _PAGE_SIZE
    assert past % _MOBA_BLOCK == 0 and n_q <= _NEW_ROWS
    rows = n_q * _HEAD_GROUP
    qbd = _block_diag_queries(q, _N_HEADS, _HEAD_DIM, _HEAD_GROUP)
    padn = ((0, 0), (0, _NEW_ROWS - n_q), (0, 0))
    k_new = jnp.pad(k_new, padn)
    v_new = jnp.pad(v_new, padn)
    table = jnp.tile(jnp.pad(rel_bias.astype(_F32), ((0, 0), (0, _HEAD_GROUP - _N_HEADS))), (1, n_q))
    qi = jnp.arange(rows, dtype=jnp.int32)[:, None] // _HEAD_GROUP
    kk = jnp.arange(_MOBA_BLOCK, dtype=jnp.int32)[None, :]
    badj = _bias_of_dist(table, _MOBA_BLOCK + qi - kk, per_row=True)
    bfar = _bias_of_dist(table, jnp.full((rows, _LANES), _MOBA_BLOCK + 1, jnp.int32), per_row=True)
    kj = jnp.arange(_LANES, dtype=jnp.int32)[None, :]
    bown = _bias_of_dist(table, jnp.where(kj < n_q, qi - kj, -1), per_row=True)

    page = lambda p: pl.BlockSpec((1, feat, _PAGE_SIZE), lambda i, pt, p=p: (pt[i, p], 0, 0))
    seq3 = lambda r: pl.BlockSpec((1, r, feat), lambda i, pt: (i, 0, 0))
    const2 = lambda shape: pl.BlockSpec(shape, lambda i, pt: (0, 0), pipeline_mode=pl.Buffered(1))
    grid_spec = pltpu.PrefetchScalarGridSpec(
        num_scalar_prefetch=1,
        grid=(n_seq,),
        in_specs=[seq3(rows), seq3(_NEW_ROWS), seq3(_NEW_ROWS), const2((rows, _MOBA_BLOCK)),
                  const2((rows, _LANES)), const2((rows, _LANES))] + [page(p) for p in range(n_pages)] * 2,
        out_specs=pl.BlockSpec((1, n_q, feat), lambda i, pt: (i, 0, 0)))
    return pl.pallas_call(
        functools.partial(_moba_step_kernel, n_pages=n_pages, n_q=n_q),
        grid_spec=grid_spec,
        out_shape=jax.ShapeDtypeStruct((n_seq, n_q, feat), _F32),
        compiler_params=_params(1),
        name="moba_step",
    )(page_table, qbd, k_new, v_new, badj, bfar, bown, *([cache_kt] * n_pages), *([cache_vt] * n_pages))


def _softmax_rows(s):
    p = jnp.exp(s - jnp.max(s, axis=1, keepdims=True))
    return p * (1.0 / jnp.sum(p, axis=1, keepdims=True))


def _memattn_kernel(qm_ref, mk_ref, mv_ref, o_ref):
    scale = _MEM_HEAD_DIM ** -0.5
    for h in range(_MEM_HEADS):
        hs = slice(h * _MEM_HEAD_DIM, (h + 1) * _MEM_HEAD_DIM)
        p = _softmax_rows(_nt_dot(qm_ref[0, :, hs], mk_ref[0, :, hs]) * scale)
        o_ref[0, :, hs] = _dot(p.astype(_BF16), mv_ref[0, :, hs]).astype(_BF16)


def _memattn(qm, mkb, mvb, tq):
    nb, s, md = qm.shape
    m = mkb.shape[1]
    return pl.pallas_call(
        _memattn_kernel,
        grid=(nb, s // tq),
        in_specs=[pl.BlockSpec((1, tq, md), lambda b, i: (b, i, 0)),
                  pl.BlockSpec((1, m, md), lambda b, i: (b, 0, 0)),
                  pl.BlockSpec((1, m, md), lambda b, i: (b, 0, 0))],
        out_specs=pl.BlockSpec((1, tq, md), lambda b, i: (b, i, 0)),
        out_shape=jax.ShapeDtypeStruct((nb, s, md), _BF16),
        compiler_params=_params(2),
        name="memattn",
    )(qm, mkb, mvb)


def _memattn_step_kernel(q_ref, mk_ref, mv_ref, o_ref):
    scale = _MEM_HEAD_DIM ** -0.5
    for i in range(mk_ref.shape[0]):
        s = _nt_dot(q_ref[i], mk_ref[i].astype(_BF16)) * scale
        r_head = lax.broadcasted_iota(jnp.int32, s.shape, 0) % _MEM_HEADS
        c_head = lax.broadcasted_iota(jnp.int32, s.shape, 1) % _MEM_HEADS
        p = _softmax_rows(jnp.where(r_head == c_head, s, _NEG_INF))
        o_ref[i] = _dot(p.astype(_BF16), mv_ref[i].astype(_BF16))


def _memattn_step(qm, mem_k, mem_v, seqs_per_step):
    n_seq, rows, dh = qm.shape
    mh = mem_k.shape[1]
    blk = lambda i: (i, 0, 0)
    return pl.pallas_call(
        _memattn_step_kernel,
        grid=(n_seq // seqs_per_step,),
        in_specs=[pl.BlockSpec((seqs_per_step, rows, dh), blk), pl.BlockSpec((seqs_per_step, mh, dh), blk),
                  pl.BlockSpec((seqs_per_step, mh, dh), blk)],
        out_specs=pl.BlockSpec((seqs_per_step, rows, dh), blk),
        out_shape=jax.ShapeDtypeStruct((n_seq, rows, dh), _F32),
        compiler_params=_params(1),
        name="memattn_step",
    )(qm, mem_k, mem_v)


def _merge_kernel(h_ref, c_ref, o_ref, om_ref, gate_ref, wc_ref, wa_ref, wm_ref, wo_ref, gpost_ref,
                  g2pre_ref, w2in_ref, w2out_ref, g2post_ref, y_ref, *, n_chunk):
    d = h_ref.shape[1]
    merged = (gate_ref[:, 0:d].astype(_F32) * _dot(c_ref[...], wc_ref[...])
              + gate_ref[:, d:2 * d].astype(_F32) * _dot(o_ref[...], wa_ref[...])
              + gate_ref[:, 2 * d:3 * d].astype(_F32) * _dot(om_ref[...], wm_ref[...]))
    h = h_ref[...] + _rms_norm(_dot(merged.astype(_BF16), wo_ref[...]), gpost_ref[...])
    y_ref[...] = _ffn_half_step(h, g2pre_ref[...], w2in_ref, w2out_ref, g2post_ref[...], n_chunk)


def _merge(h, c_act, o, om, gates, w_conv_out, w_attn_out, w_mem_out, w_out, g_post, g2_pre, w2_in, w2_out,
           g2_post, tm):
    n, d = h.shape
    row = lambda i: (i, 0)
    tile = lambda a: pl.BlockSpec((tm, a.shape[1]), row)
    return pl.pallas_call(
        functools.partial(_merge_kernel, n_chunk=_ffn_chunks(w2_out.shape[0])),
        grid=(n // tm,),
        in_specs=[tile(h), tile(c_act), tile(o), tile(om), tile(gates),
                  _resident(w_conv_out.shape), _resident(w_attn_out.shape), _resident(w_mem_out.shape),
                  _resident(w_out.shape), _resident((1, d)), _resident((1, d)), _resident(w2_in.shape),
                  _resident(w2_out.shape), _resident((1, d))],
        out_specs=pl.BlockSpec((tm, d), row),
        out_shape=jax.ShapeDtypeStruct((n, d), _F32),
        compiler_params=_params(1),
        name="merge_ffn2",
    )(h, c_act, o, om, gates, w_conv_out, w_attn_out, w_mem_out, w_out, g_post, g2_pre, w2_in, w2_out, g2_post)


def _tile(n, want):
    t = min(n, want)
    assert n % t == 0
    return t


def _token_front(x, p):
    nb, s, d = x.shape
    n = nb * s
    h, u = _ffn1(x.reshape(n, d), p['g_ffn1_pre'], p['w_ffn1_in'], p['w_ffn1_out'], p['g_ffn1_post'],
                 p['g_mix_pre'], _tile(n, 512))
    return h, u.reshape(nb, s, d)


def _token_back(h, c_act, o, om, gates, p):
    n = h.shape[0]
    flat = lambda a: a.reshape(n, a.shape[-1])
    return _merge(h, flat(c_act), flat(o), flat(om), flat(gates), p['w_conv_out'], p['w_attn_out'],
                  p['w_mem_out'], p['w_out'], p['g_mix_post'], p['g_ffn2_pre'], p['w_ffn2_in'], p['w_ffn2_out'],
                  p['g_ffn2_post'], _tile(n, 256))


def _dims(p):
    conv_dim = p['w_conv_out'].shape[0]
    attn_dim = p['w_attn_out'].shape[0]
    mem_dim = p['w_mem_out'].shape[0]
    return conv_dim, attn_dim, mem_dim


def _prompt_layer(x, mem, rel_bias, p):
    nb, s, d = x.shape
    conv_dim, attn_dim, mem_dim = _dims(p)
    h, u = _token_front(x, p)
    glu, q, k, v, qm, gates, kb, vt, c_act = _inproj(
        u, p['w_in'], p['b_gate'], _tile(s, 512), conv_dim, attn_dim, mem_dim,
        conv_params=(p['w_dwconv'], p['b_dwconv'], p['g_conv_ln'], p['b_conv_ln']))
    mk, mv, mkb, mvb = _memkv(mem, p['g_mem'], p['w_mem_kv'])
    o = _moba_prompt(q, kb, vt, rel_bias)
    om = _memattn(qm, mkb, mvb, _tile(s, 512))
    y = _token_back(h, c_act, o, om, gates, p).reshape(nb, s, d)
    conv_state = glu[:, glu.shape[1] - (_CONV_WIDTH - 1):]
    heads = lambda a: a.reshape(nb, s, _N_HEADS, _HEAD_DIM)
    mheads = lambda a: a.reshape(nb, a.shape[1], _MEM_HEADS, _MEM_HEAD_DIM)
    return y, heads(k), heads(v), mheads(mk), mheads(mv), conv_state


def _sample_layer(x, conv_prev, cache_k, cache_v, page_table, mem_k, mem_v, rel_bias, p):
    n_seq, n_q, d = x.shape
    conv_dim, attn_dim, mem_dim = _dims(p)
    n = n_seq * n_q
    h, u = _token_front(x.reshape(1, n, d), p)
    glu, q, k, v, qm, gates = _inproj(u, p['w_in'], p['b_gate'], _tile(n, 512), conv_dim, attn_dim, mem_dim)
    per_seq = lambda a: a.reshape(n_seq, n_q, a.shape[-1])
    glu = per_seq(glu)
    conv_in = jnp.concatenate([conv_prev, glu], axis=1)
    c_act = _conv_step(conv_in.transpose(1, 0, 2), p['w_dwconv'], p['b_dwconv'], p['g_conv_ln'], p['b_conv_ln'])
    c_act = c_act.transpose(1, 0, 2)
    page_t = lambda c: c.transpose(0, 2, 3, 1).reshape(c.shape[0], attn_dim, c.shape[1])
    o = _moba_step(page_table, per_seq(q), per_seq(k), per_seq(v), page_t(cache_k), page_t(cache_v), rel_bias)
    head_rows = lambda a: a.reshape(n_seq, -1, _MEM_HEAD_DIM)
    om = _memattn_step(head_rows(qm), head_rows(mem_k), head_rows(mem_v), 8)
    y = _token_back(h, c_act, o.astype(_BF16), om.reshape(n, mem_dim).astype(_BF16), gates, p)
    conv_state = conv_in[:, n_q:]
    heads = lambda a: a.reshape(n_seq, n_q, _N_HEADS, _HEAD_DIM)
    return y.reshape(n_seq, n_q, d), heads(k), heads(v), conv_state


def _layer_params(l, g_ffn1_pre, w_ffn1_in, w_ffn1_out, g_ffn1_post, g_mix_pre, w_in, b_gate, w_dwconv, b_dwconv,
                  g_conv_ln, b_conv_ln, w_conv_out, w_attn_out, g_mem, w_mem_kv, w_mem_out, w_out, g_mix_post,
                  g_ffn2_pre, w_ffn2_in, w_ffn2_out, g_ffn2_post):
    vec = lambda a: a[l].astype(_F32).reshape(1, -1)
    mat = lambda a: a[l].astype(_BF16)
    return dict(
        g_ffn1_pre=vec(g_ffn1_pre), w_ffn1_in=mat(w_ffn1_in), w_ffn1_out=mat(w_ffn1_out),
        g_ffn1_post=vec(g_ffn1_post), g_mix_pre=vec(g_mix_pre), w_in=mat(w_in), b_gate=vec(b_gate),
        w_dwconv=w_dwconv[l].astype(_F32), b_dwconv=vec(b_dwconv), g_conv_ln=vec(g_conv_ln),
        b_conv_ln=vec(b_conv_ln), w_conv_out=mat(w_conv_out), w_attn_out=mat(w_attn_out), g_mem=vec(g_mem),
        w_mem_kv=mat(w_mem_kv), w_mem_out=mat(w_mem_out), w_out=mat(w_out), g_mix_post=vec(g_mix_post),
        g_ffn2_pre=vec(g_ffn2_pre), w_ffn2_in=mat(w_ffn2_in), w_ffn2_out=mat(w_ffn2_out),
        g_ffn2_post=vec(g_ffn2_post))


def kernel(x_prompt, x_sample, cache_k, cache_v, cache_mem_k, cache_mem_v, state_conv, page_table, mem_prompt, g_ffn1_pre, w_ffn1_in, w_ffn1_out, g_ffn1_post, g_mix_pre, w_in, b_gate, w_dwconv, b_dwconv, g_conv_ln, b_conv_ln, w_conv_out, w_attn_out, rel_bias, g_mem, w_mem_kv, w_mem_out, w_out, g_mix_post, g_ffn2_pre, w_ffn2_in, w_ffn2_out, g_ffn2_post):
    depth = w_in.shape[0]
    hp, hs = x_prompt, x_sample
    outs = [[] for _ in range(8)]
    for l in range(depth):
        p = _layer_params(l, g_ffn1_pre, w_ffn1_in, w_ffn1_out, g_ffn1_post, g_mix_pre, w_in, b_gate, w_dwconv,
                          b_dwconv, g_conv_ln, b_conv_ln, w_conv_out, w_attn_out, g_mem, w_mem_kv, w_mem_out,
                          w_out, g_mix_post, g_ffn2_pre, w_ffn2_in, w_ffn2_out, g_ffn2_post)
        hp, kp, vp, mkp, mvp, cp = _prompt_layer(hp, mem_prompt, rel_bias, p)
        hs, ks, vs, cs = _sample_layer(hs, state_conv[l], cache_k[l], cache_v[l], page_table, cache_mem_k[l],
                                       cache_mem_v[l], rel_bias, p)
        for acc, val in zip(outs, (kp, vp, mkp, mvp, cp, ks, vs, cs)):
            acc.append(val)
    return (hp, hs) + tuple(jnp.stack(a) for a in outs)
```

```python
import functools
import math

import jax
import jax.numpy as jnp
from jax import lax
from jax.experimental import pallas as pl
from jax.experimental.pallas import tpu as pltpu

_F32 = jnp.float32
_BF16 = jnp.bfloat16

_CONV_WIDTH = 31
_N_HEADS = 12
_HEAD_DIM = 64
_MOBA_BLOCK = 256
_MOBA_TOPK = 3
_MEM_HEADS = 4
_MEM_HEAD_DIM = 128
_N_BUCKETS = 32
_MAX_DISTANCE = 128
_N_BRANCH = 3
_PAGE_SIZE = 128
_NORM_EPS = 1e-6
_NEG_INF = -1e30

_LANES = 128
_SUBLANES = 8
_MXU_COLS_V7X = 256
_VMEM_BYTES_V7X = 64 * 1024 * 1024
_VMEM_LIMIT = _VMEM_BYTES_V7X - 8 * 1024 * 1024

_HALO_ROWS = 32
_CONV_CHUNK_ROWS = 64
_HEAD_GROUP = 16
_NEW_ROWS = 16
_QUERY_BLOCKS_PER_STEP = 2


def _nt_dot(a, b):
    return lax.dot_general(a, b, (((1,), (1,)), ((), ())), preferred_element_type=_F32)


def _dot(a, b):
    return jnp.dot(a, b, preferred_element_type=_F32)


def _sigmoid(x):
    return 1.0 / (1.0 + jnp.exp(-x))


def _rms_norm(x, g):
    return x * lax.rsqrt(jnp.mean(x * x, axis=-1, keepdims=True) + _NORM_EPS) * g


def _resident(shape):
    nd = len(shape)
    return pl.BlockSpec(shape, lambda *_: (0,) * nd, pipeline_mode=pl.Buffered(1))


def _params(n_parallel_axes):
    return pltpu.CompilerParams(
        dimension_semantics=("arbitrary",) * n_parallel_axes,
        vmem_limit_bytes=_VMEM_LIMIT)


def _ffn_half_step(x, g_pre, w_in_ref, w_out_ref, g_post, n_chunk):
    d_ff = w_out_ref.shape[0]
    fc = d_ff // n_chunk
    xn = _rms_norm(x, g_pre).astype(_BF16)
    acc = None
    for c in range(n_chunk):
        a = _dot(xn, w_in_ref[:, c * fc:(c + 1) * fc])
        b = _dot(xn, w_in_ref[:, d_ff + c * fc:d_ff + (c + 1) * fc])
        hid = (a * _sigmoid(a) * b).astype(_BF16)
        part = _dot(hid, w_out_ref[c * fc:(c + 1) * fc, :])
        acc = part if acc is None else acc + part
    return x + 0.5 * _rms_norm(acc, g_post)


def _ffn1_kernel(x_ref, gpre_ref, win_ref, wout_ref, gpost_ref, gmix_ref, h_ref, u_ref, *, n_chunk):
    h = _ffn_half_step(x_ref[...], gpre_ref[...], win_ref, wout_ref, gpost_ref[...], n_chunk)
    h_ref[...] = h
    u_ref[...] = _rms_norm(h, gmix_ref[...]).astype(_BF16)


def _ffn_chunks(d_ff):
    return d_ff // _MXU_COLS_V7X if d_ff % _MXU_COLS_V7X == 0 else 1


def _ffn1(x, g_pre, w_in, w_out, g_post, g_mix, tm):
    n, d = x.shape
    d_ff = w_out.shape[0]
    row = lambda i: (i, 0)
    return pl.pallas_call(
        functools.partial(_ffn1_kernel, n_chunk=_ffn_chunks(d_ff)),
        grid=(n // tm,),
        in_specs=[pl.BlockSpec((tm, d), row), _resident((1, d)), _resident(w_in.shape),
                  _resident(w_out.shape), _resident((1, d)), _resident((1, d))],
        out_specs=[pl.BlockSpec((tm, d), row), pl.BlockSpec((tm, d), row)],
        out_shape=[jax.ShapeDtypeStruct((n, d), _F32), jax.ShapeDtypeStruct((n, d), _BF16)],
        compiler_params=_params(1),
        name="ffn1",
    )(x, g_pre, w_in, w_out, g_post, g_mix)


def _inproj_kernel(u_ref, w_ref, bg_ref, *rest, conv_dim, attn_dim, mem_dim, whole_sequences):
    if whole_sequences:
        wd_ref, bd_ref, gln_ref, bln_ref = rest[:4]
        rest = rest[4:]
    conv_ref, q_ref, k_ref, v_ref, qm_ref, gate_ref = rest[:6]
    u = u_ref[0]
    tm = u.shape[0]
    o1 = 2 * conv_dim
    o2 = o1 + attn_dim
    o3 = o2 + attn_dim
    o4 = o3 + attn_dim
    o5 = o4 + mem_dim

    def proj(lo, hi):
        return _dot(u, w_ref[:, lo:hi])

    if whole_sequences:
        kb_ref, vt_ref, cact_ref, ext_ref, cbuf_ref = rest[6:]
        @pl.when(pl.program_id(1) == 0)
        def _():
            ext_ref[0:_HALO_ROWS, :] = jnp.zeros((_HALO_ROWS, conv_dim), _F32)

        @pl.when(pl.program_id(1) > 0)
        def _():
            ext_ref[0:_HALO_ROWS, :] = ext_ref[tm:tm + _HALO_ROWS, :]

    glu = proj(0, conv_dim) * _sigmoid(proj(conv_dim, o1))
    conv_ref[0] = glu
    if whole_sequences:
        ext_ref[_HALO_ROWS:, :] = glu
        for r0 in range(0, tm, _CONV_CHUNK_ROWS):
            _conv_module_rows(ext_ref, cbuf_ref, cact_ref, r0, _CONV_CHUNK_ROWS, wd_ref, bd_ref, gln_ref, bln_ref)
    q_ref[0] = (proj(o1, o2) * (_HEAD_DIM ** -0.5)).astype(_BF16)
    k = proj(o2, o3)
    v = proj(o3, o4)
    k_ref[0] = k
    v_ref[0] = v
    qm_ref[0] = proj(o4, o5).astype(_BF16)
    gate_ref[0] = _sigmoid(proj(o5, w_ref.shape[1]) + bg_ref[...]).astype(_BF16)
    if whole_sequences:
        kb_ref[0] = k.astype(_BF16)
        vt_ref[0] = v.T.astype(_BF16)


def _inproj(u, w_in, b_gate, tm, conv_dim, attn_dim, mem_dim, conv_params=None):
    nb, s, d = u.shape
    n_gate = b_gate.shape[1]
    whole_sequences = conv_params is not None
    tok = lambda b, i: (b, i, 0)
    in_specs = [pl.BlockSpec((1, tm, d), tok), _resident(w_in.shape), _resident((1, n_gate))]
    out_specs = [pl.BlockSpec((1, tm, conv_dim), tok), pl.BlockSpec((1, tm, attn_dim), tok),
                 pl.BlockSpec((1, tm, attn_dim), tok), pl.BlockSpec((1, tm, attn_dim), tok),
                 pl.BlockSpec((1, tm, mem_dim), tok), pl.BlockSpec((1, tm, n_gate), tok)]
    out_shape = [jax.ShapeDtypeStruct((nb, s, conv_dim), _F32), jax.ShapeDtypeStruct((nb, s, attn_dim), _BF16),
                 jax.ShapeDtypeStruct((nb, s, attn_dim), _F32), jax.ShapeDtypeStruct((nb, s, attn_dim), _F32),
                 jax.ShapeDtypeStruct((nb, s, mem_dim), _BF16), jax.ShapeDtypeStruct((nb, s, n_gate), _BF16)]
    scratch_shapes = []
    operands = [u, w_in, b_gate]
    if whole_sequences:
        assert tm % _CONV_CHUNK_ROWS == 0
        in_specs += [_resident(a.shape) for a in conv_params]
        operands += list(conv_params)
        out_specs += [pl.BlockSpec((1, tm, attn_dim), tok), pl.BlockSpec((1, attn_dim, tm), lambda b, i: (b, 0, i)),
                      pl.BlockSpec((1, tm, conv_dim), tok)]
        out_shape += [jax.ShapeDtypeStruct((nb, s, attn_dim), _BF16), jax.ShapeDtypeStruct((nb, attn_dim, s), _BF16),
                      jax.ShapeDtypeStruct((nb, s, conv_dim), _BF16)]
        scratch_shapes = [pltpu.VMEM((tm + _HALO_ROWS, conv_dim), _F32), pltpu.VMEM((tm, conv_dim), _F32)]
    return pl.pallas_call(
        functools.partial(_inproj_kernel, conv_dim=conv_dim, attn_dim=attn_dim, mem_dim=mem_dim,
                          whole_sequences=whole_sequences),
        grid=(nb, s // tm),
        in_specs=in_specs, out_specs=out_specs, out_shape=out_shape, scratch_shapes=scratch_shapes,
        compiler_params=_params(2),
        name="inproj",
    )(*operands)


def _memkv_kernel(mem_ref, g_ref, w_ref, mk_ref, mv_ref, mkb_ref, mvb_ref):
    kv = _dot(_rms_norm(mem_ref[0], g_ref[...]).astype(_BF16), w_ref[...])
    half = kv.shape[1] // 2
    mk_ref[0] = kv[:, :half]
    mv_ref[0] = kv[:, half:]
    mkb_ref[0] = kv[:, :half].astype(_BF16)
    mvb_ref[0] = kv[:, half:].astype(_BF16)


def _memkv(mem, g_mem, w_mem_kv):
    nb, m, d = mem.shape
    md = w_mem_kv.shape[1] // 2
    blk = lambda b: (b, 0, 0)
    return pl.pallas_call(
        _memkv_kernel,
        grid=(nb,),
        in_specs=[pl.BlockSpec((1, m, d), blk), _resident((1, d)), _resident(w_mem_kv.shape)],
        out_specs=[pl.BlockSpec((1, m, md), blk)] * 4,
        out_shape=[jax.ShapeDtypeStruct((nb, m, md), _F32)] * 2 + [jax.ShapeDtypeStruct((nb, m, md), _BF16)] * 2,
        compiler_params=_params(1),
        name="memkv",
    )(mem, g_mem, w_mem_kv)


def _ln_swish(c, g, b):
    mu = jnp.mean(c, axis=-1, keepdims=True)
    cc = c - mu
    var = jnp.mean(cc * cc, axis=-1, keepdims=True)
    y = cc * lax.rsqrt(var + _NORM_EPS) * g + b
    return y * _sigmoid(y)


def _conv_module_rows(ext_ref, cbuf_ref, out_ref, r0, rows, wd_ref, bd_ref, g_ref, b_ref):
    ch = ext_ref.shape[1]
    lead = _HALO_ROWS - (_CONV_WIDTH - 1)
    n_vreg = rows // _SUBLANES
    n_win = n_vreg + _HALO_ROWS // _SUBLANES
    sub = lax.broadcasted_iota(jnp.int32, (1, _SUBLANES, _LANES), 1)
    blocks = [slice(lb * _LANES, (lb + 1) * _LANES) for lb in range(ch // _LANES)]
    rs = slice(r0, r0 + rows)
    total = None
    for ls in blocks:
        win = ext_ref[r0:r0 + rows + _HALO_ROWS, ls].reshape(n_win, _SUBLANES, _LANES)
        acc = jnp.broadcast_to(bd_ref[:, ls], (n_vreg, _SUBLANES, _LANES))
        for r in range(_SUBLANES):
            n_a = (_CONV_WIDTH - r + _SUBLANES - 1) // _SUBLANES
            base, k = divmod(lead + r, _SUBLANES)
            need = n_vreg + n_a - 1
            if k == 0:
                sh = win[base:base + need]
            else:
                rot = pltpu.roll(win[base:base + need + 1], _SUBLANES - k, axis=1)
                sh = jnp.where(sub < _SUBLANES - k, rot[:-1], rot[1:])
            for a in range(n_a):
                w = _SUBLANES * a + r
                acc = acc + sh[a:a + n_vreg] * wd_ref[w:w + 1, ls]
        acc = acc.reshape(rows, _LANES)
        cbuf_ref[rs, ls] = acc
        total = acc if total is None else total + acc
    mu = jnp.sum(total, axis=-1, keepdims=True) * (1.0 / ch)
    sq = None
    for ls in blocks:
        d = cbuf_ref[rs, ls] - mu
        sq = d * d if sq is None else sq + d * d
    rstd = lax.rsqrt(jnp.sum(sq, axis=-1, keepdims=True) * (1.0 / ch) + _NORM_EPS)
    for ls in blocks:
        y = (cbuf_ref[rs, ls] - mu) * rstd * g_ref[:, ls] + b_ref[:, ls]
        out_ref[0, rs, ls] = (y * _sigmoid(y)).astype(_BF16)


def _conv_step_kernel(ext_ref, wd_ref, bd_ref, g_ref, b_ref, out_ref):
    n_q = out_ref.shape[0]
    for t in range(n_q):
        acc = jnp.broadcast_to(bd_ref[...], ext_ref.shape[1:])
        for w in range(_CONV_WIDTH):
            acc = acc + ext_ref[t + w] * wd_ref[w:w + 1, :]
        out_ref[t] = _ln_swish(acc, g_ref[...], b_ref[...]).astype(_BF16)


def _conv_step(ext_tm, w_dw, b_dw, g_ln, b_ln):
    n_rows, n_seq, ch = ext_tm.shape
    n_q = n_rows - (_CONV_WIDTH - 1)
    return pl.pallas_call(
        _conv_step_kernel,
        grid=(1,),
        in_specs=[_resident(ext_tm.shape), _resident(w_dw.shape), _resident((1, ch)), _resident((1, ch)),
                  _resident((1, ch))],
        out_specs=pl.BlockSpec((n_q, n_seq, ch), lambda i: (0, 0, 0)),
        out_shape=jax.ShapeDtypeStruct((n_q, n_seq, ch), _BF16),
        compiler_params=_params(1),
        name="conv_step",
    )(ext_tm, w_dw, b_dw, g_ln, b_ln)


def _rel_bucket(dist):
    n = jnp.maximum(dist, 0)
    max_exact = _N_BUCKETS // 2
    nf = jnp.maximum(n, 1).astype(_F32)
    large = max_exact + (jnp.log(nf / max_exact) / math.log(_MAX_DISTANCE / max_exact)
                         * (_N_BUCKETS - max_exact)).astype(jnp.int32)
    large = jnp.minimum(large, _N_BUCKETS - 1)
    return jnp.where(n < max_exact, n, large)


def _bias_of_dist(table, dist, per_row=False):
    n_col = table.shape[1]
    bucket = jnp.where(dist >= 0, _rel_bucket(dist), -1)
    if per_row:
        col_shape = (n_col,) + (1,) * (dist.ndim - 1)
    else:
        bucket = bucket[None]
        col_shape = (n_col,) + (1,) * dist.ndim
    out = jnp.full(jnp.broadcast_shapes(bucket.shape, col_shape), _NEG_INF, _F32)
    for b in range(_N_BUCKETS):
        out = jnp.where(bucket == b, table[b].reshape(col_shape), out)
    return out


def _topk_block_mask(score, blk, n_valid):
    rank = jnp.zeros(score.shape, jnp.int32)
    for m in range(score.shape[0]):
        sm = score[m:m + 1, :]
        counts = jnp.where(m < n_valid, 1, 0)
        rank = rank + jnp.where(sm > score, counts, 0) + jnp.where(sm == score, jnp.where(m < blk, counts, 0), 0)
    return (blk < n_valid) & (rank < _MOBA_TOPK)


def _moba_prompt_kernel(far_ref, q_ref, k_ref, vt_ref, bias_ref, o_ref, kmean_ref, pen_ref, s_buf, p_buf,
                        m_ref, l_ref, alpha_ref, acc_ref, *, n_blk):
    blk_len = _MOBA_BLOCK
    n_query = q_ref.shape[1]
    n_qb = n_query // blk_len
    hp = pl.program_id(1)
    g = pl.program_id(2)
    half = _HEAD_DIM
    owns = [n_qb * g + i for i in range(n_qb)]
    qcols = [slice(i * blk_len, (i + 1) * blk_len) for i in range(n_qb)]
    n_far = owns[0] // 2

    @pl.when(g == 0)
    def _():
        for j in range(n_blk):
            kj = k_ref[0, j * blk_len:(j + 1) * blk_len, :].astype(_F32)
            kmean_ref[j:j + 1, :] = jnp.mean(kj, axis=0, keepdims=True)

    qf = q_ref[0].astype(_F32)
    lane = lax.broadcasted_iota(jnp.int32, qf.shape, 1)
    blk = lax.broadcasted_iota(jnp.int32, (n_blk, n_query), 0)
    own_q = owns[0] + lax.broadcasted_iota(jnp.int32, (1, n_query), 1) // blk_len
    kmean = kmean_ref[...].astype(_BF16)

    qhs, adj_pens = [], []
    for hd in range(2):
        qh = jnp.where((lane >= hd * half) & (lane < (hd + 1) * half), qf, 0.0).astype(_BF16)
        sel = _topk_block_mask(_nt_dot(kmean, qh), blk, own_q)
        pen_ref[hd] = jnp.where(sel & (blk < own_q - 1), far_ref[hp * 2 + hd], _NEG_INF)
        sel_adj = jnp.sum(jnp.where(sel & (blk == own_q - 1), 1.0, 0.0), axis=0, keepdims=True)
        adj_pens.append(jnp.where(own_q == 0, 0.0, jnp.where(sel_adj > 0.5, 0.0, _NEG_INF)))
        qhs.append(qh)

    log2e = math.log2(math.e)
    lo, hi = slice(0, blk_len), slice(blk_len, 2 * blk_len)

    def block_start(j):
        return pl.multiple_of(j * blk_len, blk_len)

    stage0_blocks = [(block_start(jnp.maximum(o - 1, 0)), block_start(jnp.maximum(o, 1))) for o in owns]

    def blocks_before_far_stage(t):
        far_a = (2 * t - 2) * blk_len
        return [(pl.multiple_of(jnp.where(t == 0, a0, far_a), blk_len),
                 pl.multiple_of(jnp.where(t == 0, b0, far_a + blk_len), blk_len)) for a0, b0 in stage0_blocks]

    def stage0_logits_into(slot):
        for hd in range(2):
            for qb in range(n_qb):
                q_rows = qhs[hd][qcols[qb], :]
                a0, b0 = stage0_blocks[qb]
                s_buf[slot, hd, lo, qcols[qb]] = _nt_dot(k_ref[0, pl.ds(a0, blk_len), :], q_rows)
                s_buf[slot, hd, hi, qcols[qb]] = _nt_dot(k_ref[0, pl.ds(b0, blk_len), :], q_rows)

    def far_logits_into(slot, t):
        start = pl.multiple_of(2 * t * blk_len, 2 * blk_len)
        for hd in range(2):
            s_buf[slot, hd] = _nt_dot(k_ref[0, pl.ds(start, 2 * blk_len), :], qhs[hd])

    def pv_from(slot, blocks):
        res = []
        for hd in range(2):
            rows = slice(hd * half, (hd + 1) * half)
            parts = [_dot(vt_ref[0, rows, pl.ds(a, blk_len)], p_buf[slot, hd, lo, qcols[qb]])
                     + _dot(vt_ref[0, rows, pl.ds(b, blk_len)], p_buf[slot, hd, hi, qcols[qb]])
                     for qb, (a, b) in enumerate(blocks)]
            res.append(jnp.concatenate(parts, axis=1))
        return res

    def softmax_stage(s_slot, p_slot, hd, cols, row_a, row_b, m_old, tiles=None):
        s_a, s_b = s_buf[s_slot, hd, lo, cols], s_buf[s_slot, hd, hi, cols]
        if tiles is not None:
            s_a, s_b = s_a + tiles[0], s_b + tiles[1]
        m_new = jnp.maximum(jnp.max(s_a, axis=0, keepdims=True) + row_a, jnp.max(s_b, axis=0, keepdims=True) + row_b)
        if m_old is not None:
            m_new = jnp.maximum(m_new, m_old)
        p_a = jnp.exp2(s_a * log2e + (row_a - m_new) * log2e)
        p_b = jnp.exp2(s_b * log2e + (row_b - m_new) * log2e)
        p_buf[p_slot, hd, lo, cols] = p_a.astype(_BF16)
        p_buf[p_slot, hd, hi, cols] = p_b.astype(_BF16)
        return m_new, jnp.sum(p_a, axis=0, keepdims=True) + jnp.sum(p_b, axis=0, keepdims=True)

    stage0_logits_into(0)
    far_logits_into(1, 0)
    for hd in range(2):
        for qb in range(n_qb):
            first = owns[qb] == 0
            tiles = (bias_ref[hd, jnp.where(first, 0, 1)], bias_ref[hd, 0])
            m0, l0 = softmax_stage(0, 0, hd, qcols[qb], adj_pens[hd][:, qcols[qb]],
                                   jnp.where(first, _NEG_INF, 0.0), None, tiles=tiles)
            m_ref[hd, :, qcols[qb]] = m0
            l_ref[hd, :, qcols[qb]] = l0
        alpha_ref[hd] = jnp.ones((1, n_query), _F32)
        acc_ref[hd] = jnp.zeros((half, n_query), _F32)

    def far_stage(t, slot):
        pv = pv_from(slot, blocks_before_far_stage(t))
        for hd in range(2):
            acc_ref[hd] = alpha_ref[hd] * acc_ref[hd] + pv[hd]
            for cols in qcols:
                m_old = m_ref[hd, :, cols]
                m_new, l_blk = softmax_stage(1 - slot, 1 - slot, hd, cols, pen_ref[hd, pl.ds(2 * t, 1), cols],
                                             pen_ref[hd, pl.ds(2 * t + 1, 1), cols], m_old)
                alpha = jnp.exp(m_old - m_new)
                l_ref[hd, :, cols] = alpha * l_ref[hd, :, cols] + l_blk
                m_ref[hd, :, cols] = m_new
                alpha_ref[hd, :, cols] = alpha
        far_logits_into(slot, t + 1)

    def far_stage_by_parity(t, carry):
        for slot in range(2):
            pl.when(t % 2 == slot)(functools.partial(far_stage, t, slot))
        return carry

    lax.fori_loop(0, n_far, far_stage_by_parity, 0)

    def finish(slot):
        pv = pv_from(slot, blocks_before_far_stage(n_far))
        outs = [(alpha_ref[hd] * acc_ref[hd] + pv[hd]) * (1.0 / l_ref[hd]) for hd in range(2)]
        o_ref[0] = jnp.concatenate(outs, axis=0).T.astype(_BF16)

    for slot in range(2):
        pl.when(n_far % 2 == slot)(functools.partial(finish, slot))


def _moba_prompt(q, kb, vt, rel_bias):
    nb, s, attn_dim = q.shape
    n_blk = s // _MOBA_BLOCK
    assert n_blk >= 2 and n_blk % 2 == 0 and _QUERY_BLOCKS_PER_STEP in (1, 2)
    n_pair = attn_dim // _LANES
    n_query = _QUERY_BLOCKS_PER_STEP * _MOBA_BLOCK
    kk = jnp.arange(_MOBA_BLOCK, dtype=jnp.int32)[:, None]
    qq = jnp.arange(_MOBA_BLOCK, dtype=jnp.int32)[None, :]
    table = rel_bias.astype(_F32)
    tiles = _bias_of_dist(table, jnp.stack([qq - kk, _MOBA_BLOCK + qq - kk]))
    far = _bias_of_dist(table, jnp.full((1,), _MOBA_BLOCK + 1, jnp.int32))[:, 0]
    return pl.pallas_call(
        functools.partial(_moba_prompt_kernel, n_blk=n_blk),
        grid=(nb, n_pair, s // n_query),
        in_specs=[pl.BlockSpec(memory_space=pltpu.SMEM),
                  pl.BlockSpec((1, n_query, _LANES), lambda b, h, c: (b, c, h)),
                  pl.BlockSpec((1, s, _LANES), lambda b, h, c: (b, 0, h)),
                  pl.BlockSpec((1, _LANES, s), lambda b, h, c: (b, h, 0)),
                  pl.BlockSpec((2, 2, _MOBA_BLOCK, _MOBA_BLOCK), lambda b, h, c: (h, 0, 0, 0))],
        out_specs=pl.BlockSpec((1, n_query, _LANES), lambda b, h, c: (b, c, h)),
        out_shape=jax.ShapeDtypeStruct((nb, s, attn_dim), _BF16),
        scratch_shapes=[pltpu.VMEM((n_blk, _LANES), _F32),
                        pltpu.VMEM((2, n_blk, n_query), _F32),
                        pltpu.VMEM((2, 2, 2 * _MOBA_BLOCK, n_query), _F32),
                        pltpu.VMEM((2, 2, 2 * _MOBA_BLOCK, n_query), _BF16),
                        pltpu.VMEM((2, 1, n_query), _F32), pltpu.VMEM((2, 1, n_query), _F32),
                        pltpu.VMEM((2, 1, n_query), _F32),
                        pltpu.VMEM((2, _HEAD_DIM, n_query), _F32)],
        compiler_params=_params(3),
        name="moba_prompt",
    )(far, q, kb, vt, tiles)


def _moba_step_kernel(pt_ref, qbd_ref, knew_ref, vnew_ref, badj_ref, bfar_ref, bown_ref, *rest, n_pages, n_q):
    del pt_ref
    kt_pages = rest[:n_pages]
    vt_pages = rest[n_pages:2 * n_pages]
    o_ref = rest[2 * n_pages]
    ppb = _MOBA_BLOCK // _PAGE_SIZE
    n_blk = n_pages // ppb
    qbd = qbd_ref[0]
    feat = qbd.shape[1]

    def block_of(pages, j):
        return jnp.concatenate([pages[p][0] for p in range(j * ppb, (j + 1) * ppb)], axis=1).astype(_BF16)

    s_blk = [_dot(qbd, block_of(kt_pages, j)) for j in range(n_blk)]
    score = [jnp.sum(s, axis=1, keepdims=True) for s in s_blk]
    logits = []
    for n in range(n_blk):
        rank = jnp.zeros(score[n].shape, jnp.int32)
        for m in range(n_blk):
            if m < n:
                rank = rank + jnp.where(score[m] >= score[n], 1, 0)
            elif m > n:
                rank = rank + jnp.where(score[m] > score[n], 1, 0)
        bias = badj_ref[...] if n == n_blk - 1 else bfar_ref[:, 0:1]
        logits.append(jnp.where(rank < _MOBA_TOPK, s_blk[n] + bias, _NEG_INF))
    s_own = _nt_dot(qbd, knew_ref[0].astype(_BF16)) + bown_ref[:, 0:_NEW_ROWS]
    m_row = jnp.max(s_own, axis=1, keepdims=True)
    for lg in logits:
        m_row = jnp.maximum(m_row, jnp.max(lg, axis=1, keepdims=True))
    p_own = jnp.exp(s_own - m_row)
    probs = [jnp.exp(lg - m_row) for lg in logits]
    denom = jnp.sum(p_own, axis=1, keepdims=True)
    for p in probs:
        denom = denom + jnp.sum(p, axis=1, keepdims=True)
    inv = 1.0 / denom
    out = _dot((p_own * inv).astype(_BF16), vnew_ref[0].astype(_BF16))
    for j in range(n_blk):
        out = out + _nt_dot((probs[j] * inv).astype(_BF16), block_of(vt_pages, j))
    r_head = lax.broadcasted_iota(jnp.int32, out.shape, 0) % _HEAD_GROUP
    c_head = lax.broadcasted_iota(jnp.int32, out.shape, 1) // _HEAD_DIM
    out = jnp.where(r_head == c_head, out, 0.0)
    o_ref[0] = jnp.sum(out.reshape(n_q, _HEAD_GROUP, feat), axis=1)


def _block_diag_queries(q, n_heads, head_dim, group):
    n, n_q, feat = q.shape
    r_head = jnp.arange(group, dtype=jnp.int32)[:, None]
    c_head = jnp.arange(feat, dtype=jnp.int32)[None, :] // head_dim
    mask = (r_head == c_head) & (r_head < n_heads)
    return jnp.where(mask[None, None], q[:, :, None, :], jnp.zeros((), q.dtype)).reshape(n, n_q * group, feat)


def _moba_step(page_table, q, k_new, v_new, cache_kt, cache_vt, rel_bias):
    n_seq, n_q, feat = q.shape
    n_pages = page_table.shape[1]
    past = n_pages * _PAGE_SIZE
    assert past % _MOBA_BLOCK == 0 and n_q <= _NEW_ROWS
    rows = n_q * _HEAD_GROUP
    qbd = _block_diag_queries(q, _N_HEADS, _HEAD_DIM, _HEAD_GROUP)
    padn = ((0, 0), (0, _NEW_ROWS - n_q), (0, 0))
    k_new = jnp.pad(k_new, padn)
    v_new = jnp.pad(v_new, padn)
    table = jnp.tile(jnp.pad(rel_bias.astype(_F32), ((0, 0), (0, _HEAD_GROUP - _N_HEADS))), (1, n_q))
    qi = jnp.arange(rows, dtype=jnp.int32)[:, None] // _HEAD_GROUP
    kk = jnp.arange(_MOBA_BLOCK, dtype=jnp.int32)[None, :]
    badj = _bias_of_dist(table, _MOBA_BLOCK + qi - kk, per_row=True)
    bfar = _bias_of_dist(table, jnp.full((rows, _LANES), _MOBA_BLOCK + 1, jnp.int32), per_row=True)
    kj = jnp.arange(_LANES, dtype=jnp.int32)[None, :]
    bown = _bias_of_dist(table, jnp.where(kj < n_q, qi - kj, -1), per_row=True)

    page = lambda p: pl.BlockSpec((1, feat, _PAGE_SIZE), lambda i, pt, p=p: (pt[i, p], 0, 0))
    seq3 = lambda r: pl.BlockSpec((1, r, feat), lambda i, pt: (i, 0, 0))
    const2 = lambda shape: pl.BlockSpec(shape, lambda i, pt: (0, 0), pipeline_mode=pl.Buffered(1))
    grid_spec = pltpu.PrefetchScalarGridSpec(
        num_scalar_prefetch=1,
        grid=(n_seq,),
        in_specs=[seq3(rows), seq3(_NEW_ROWS), seq3(_NEW_ROWS), const2((rows, _MOBA_BLOCK)),
                  const2((rows, _LANES)), const2((rows, _LANES))] + [page(p) for p in range(n_pages)] * 2,
        out_specs=pl.BlockSpec((1, n_q, feat), lambda i, pt: (i, 0, 0)))
    return pl.pallas_call(
        functools.partial(_moba_step_kernel, n_pages=n_pages, n_q=n_q),
        grid_spec=grid_spec,
        out_shape=jax.ShapeDtypeStruct((n_seq, n_q, feat), _F32),
        compiler_params=_params(1),
        name="moba_step",
    )(page_table, qbd, k_new, v_new, badj, bfar, bown, *([cache_kt] * n_pages), *([cache_vt] * n_pages))


def _softmax_rows(s):
    p = jnp.exp(s - jnp.max(s, axis=1, keepdims=True))
    return p * (1.0 / jnp.sum(p, axis=1, keepdims=True))


def _memattn_kernel(qm_ref, mk_ref, mv_ref, o_ref):
    scale = _MEM_HEAD_DIM ** -0.5
    for h in range(_MEM_HEADS):
        hs = slice(h * _MEM_HEAD_DIM, (h + 1) * _MEM_HEAD_DIM)
        p = _softmax_rows(_nt_dot(qm_ref[0, :, hs], mk_ref[0, :, hs]) * scale)
        o_ref[0, :, hs] = _dot(p.astype(_BF16), mv_ref[0, :, hs]).astype(_BF16)


def _memattn(qm, mkb, mvb, tq):
    nb, s, md = qm.shape
    m = mkb.shape[1]
    return pl.pallas_call(
        _memattn_kernel,
        grid=(nb, s // tq),
        in_specs=[pl.BlockSpec((1, tq, md), lambda b, i: (b, i, 0)),
                  pl.BlockSpec((1, m, md), lambda b, i: (b, 0, 0)),
                  pl.BlockSpec((1, m, md), lambda b, i: (b, 0, 0))],
        out_specs=pl.BlockSpec((1, tq, md), lambda b, i: (b, i, 0)),
        out_shape=jax.ShapeDtypeStruct((nb, s, md), _BF16),
        compiler_params=_params(2),
        name="memattn",
    )(qm, mkb, mvb)


def _memattn_step_kernel(q_ref, mk_ref, mv_ref, o_ref):
    scale = _MEM_HEAD_DIM ** -0.5
    for i in range(mk_ref.shape[0]):
        s = _nt_dot(q_ref[i], mk_ref[i].astype(_BF16)) * scale
        r_head = lax.broadcasted_iota(jnp.int32, s.shape, 0) % _MEM_HEADS
        c_head = lax.broadcasted_iota(jnp.int32, s.shape, 1) % _MEM_HEADS
        p = _softmax_rows(jnp.where(r_head == c_head, s, _NEG_INF))
        o_ref[i] = _dot(p.astype(_BF16), mv_ref[i].astype(_BF16))


def _memattn_step(qm, mem_k, mem_v, seqs_per_step):
    n_seq, rows, dh = qm.shape
    mh = mem_k.shape[1]
    blk = lambda i: (i, 0, 0)
    return pl.pallas_call(
        _memattn_step_kernel,
        grid=(n_seq // seqs_per_step,),
        in_specs=[pl.BlockSpec((seqs_per_step, rows, dh), blk), pl.BlockSpec((seqs_per_step, mh, dh), blk),
                  pl.BlockSpec((seqs_per_step, mh, dh), blk)],
        out_specs=pl.BlockSpec((seqs_per_step, rows, dh), blk),
        out_shape=jax.ShapeDtypeStruct((n_seq, rows, dh), _F32),
        compiler_params=_params(1),
        name="memattn_step",
    )(qm, mem_k, mem_v)


def _merge_kernel(h_ref, c_ref, o_ref, om_ref, gate_ref, wc_ref, wa_ref, wm_ref, wo_ref, gpost_ref,
                  g2pre_ref, w2in_ref, w2out_ref, g2post_ref, y_ref, *, n_chunk):
    d = h_ref.shape[1]
    merged = (gate_ref[:, 0:d].astype(_F32) * _dot(c_ref[...], wc_ref[...])
              + gate_ref[:, d:2 * d].astype(_F32) * _dot(o_ref[...], wa_ref[...])
              + gate_ref[:, 2 * d:3 * d].astype(_F32) * _dot(om_ref[...], wm_ref[...]))
    h = h_ref[...] + _rms_norm(_dot(merged.astype(_BF16), wo_ref[...]), gpost_ref[...])
    y_ref[...] = _ffn_half_step(h, g2pre_ref[...], w2in_ref, w2out_ref, g2post_ref[...], n_chunk)


def _merge(h, c_act, o, om, gates, w_conv_out, w_attn_out, w_mem_out, w_out, g_post, g2_pre, w2_in, w2_out,
           g2_post, tm):
    n, d = h.shape
    row = lambda i: (i, 0)
    tile = lambda a: pl.BlockSpec((tm, a.shape[1]), row)
    return pl.pallas_call(
        functools.partial(_merge_kernel, n_chunk=_ffn_chunks(w2_out.shape[0])),
        grid=(n // tm,),
        in_specs=[tile(h), tile(c_act), tile(o), tile(om), tile(gates),
                  _resident(w_conv_out.shape), _resident(w_attn_out.shape), _resident(w_mem_out.shape),
                  _resident(w_out.shape), _resident((1, d)), _resident((1, d)), _resident(w2_in.shape),
                  _resident(w2_out.shape), _resident((1, d))],
        out_specs=pl.BlockSpec((tm, d), row),
        out_shape=jax.ShapeDtypeStruct((n, d), _F32),
        compiler_params=_params(1),
        name="merge_ffn2",
    )(h, c_act, o, om, gates, w_conv_out, w_attn_out, w_mem_out, w_out, g_post, g2_pre, w2_in, w2_out, g2_post)


def _tile(n, want):
    t = min(n, want)
    assert n % t == 0
    return t


def _token_front(x, p):
    nb, s, d = x.shape
    n = nb * s
    h, u = _ffn1(x.reshape(n, d), p['g_ffn1_pre'], p['w_ffn1_in'], p['w_ffn1_out'], p['g_ffn1_post'],
                 p['g_mix_pre'], _tile(n, 512))
    return h, u.reshape(nb, s, d)


def _token_back(h, c_act, o, om, gates, p):
    n = h.shape[0]
    flat = lambda a: a.reshape(n, a.shape[-1])
    return _merge(h, flat(c_act), flat(o), flat(om), flat(gates), p['w_conv_out'], p['w_attn_out'],
                  p['w_mem_out'], p['w_out'], p['g_mix_post'], p['g_ffn2_pre'], p['w_ffn2_in'], p['w_ffn2_out'],
                  p['g_ffn2_post'], _tile(n, 512))


def _dims(p):
    conv_dim = p['w_conv_out'].shape[0]
    attn_dim = p['w_attn_out'].shape[0]
    mem_dim = p['w_mem_out'].shape[0]
    return conv_dim, attn_dim, mem_dim


def _prompt_layer(x, mem, rel_bias, p):
    nb, s, d = x.shape
    conv_dim, attn_dim, mem_dim = _dims(p)
    h, u = _token_front(x, p)
    glu, q, k, v, qm, gates, kb, vt, c_act = _inproj(
        u, p['w_in'], p['b_gate'], _tile(s, 512), conv_dim, attn_dim, mem_dim,
        conv_params=(p['w_dwconv'], p['b_dwconv'], p['g_conv_ln'], p['b_conv_ln']))
    mk, mv, mkb, mvb = _memkv(mem, p['g_mem'], p['w_mem_kv'])
    o = _moba_prompt(q, kb, vt, rel_bias)
    om = _memattn(qm, mkb, mvb, _tile(s, 512))
    y = _token_back(h, c_act, o, om, gates, p).reshape(nb, s, d)
    conv_state = glu[:, s - (_CONV_WIDTH - 1):]
    heads = lambda a: a.reshape(nb, s, _N_HEADS, _HEAD_DIM)
    mheads = lambda a: a.reshape(nb, a.shape[1], _MEM_HEADS, _MEM_HEAD_DIM)
    return y, heads(k), heads(v), mheads(mk), mheads(mv), conv_state


def _sample_layer(x, conv_prev, cache_k, cache_v, page_table, mem_k, mem_v, rel_bias, p):
    n_seq, n_q, d = x.shape
    conv_dim, attn_dim, mem_dim = _dims(p)
    n = n_seq * n_q
    h, u = _token_front(x.reshape(1, n, d), p)
    glu, q, k, v, qm, gates = _inproj(u, p['w_in'], p['b_gate'], _tile(n, 512), conv_dim, attn_dim, mem_dim)
    per_seq = lambda a: a.reshape(n_seq, n_q, a.shape[-1])
    glu = per_seq(glu)
    conv_in = jnp.concatenate([conv_prev, glu], axis=1)
    c_act = _conv_step(conv_in.transpose(1, 0, 2), p['w_dwconv'], p['b_dwconv'], p['g_conv_ln'], p['b_conv_ln'])
    c_act = c_act.transpose(1, 0, 2)
    page_t = lambda c: c.transpose(0, 2, 3, 1).reshape(c.shape[0], attn_dim, c.shape[1])
    o = _moba_step(page_table, per_seq(q), per_seq(k), per_seq(v), page_t(cache_k), page_t(cache_v), rel_bias)
    head_rows = lambda a: a.reshape(n_seq, -1, _MEM_HEAD_DIM)
    om = _memattn_step(head_rows(qm), head_rows(mem_k), head_rows(mem_v), 8)
    y = _token_back(h, c_act, o.astype(_BF16), om.reshape(n, mem_dim).astype(_BF16), gates, p)
    conv_state = conv_in[:, n_q:]
    heads = lambda a: a.reshape(n_seq, n_q, _N_HEADS, _HEAD_DIM)
    return y.reshape(n_seq, n_q, d), heads(k), heads(v), conv_state


def _layer_params(l, g_ffn1_pre, w_ffn1_in, w_ffn1_out, g_ffn1_post, g_mix_pre, w_in, b_gate, w_dwconv, b_dwconv,
                  g_conv_ln, b_conv_ln, w_conv_out, w_attn_out, g_mem, w_mem_kv, w_mem_out, w_out, g_mix_post,
                  g_ffn2_pre, w_ffn2_in, w_ffn2_out, g_ffn2_post):
    vec = lambda a: a[l].astype(_F32).reshape(1, -1)
    mat = lambda a: a[l].astype(_BF16)
    return dict(
        g_ffn1_pre=vec(g_ffn1_pre), w_ffn1_in=mat(w_ffn1_in), w_ffn1_out=mat(w_ffn1_out),
        g_ffn1_post=vec(g_ffn1_post), g_mix_pre=vec(g_mix_pre), w_in=mat(w_in), b_gate=vec(b_gate),
        w_dwconv=w_dwconv[l].astype(_F32), b_dwconv=vec(b_dwconv), g_conv_ln=vec(g_conv_ln),
        b_conv_ln=vec(b_conv_ln), w_conv_out=mat(w_conv_out), w_attn_out=mat(w_attn_out), g_mem=vec(g_mem),
        w_mem_kv=mat(w_mem_kv), w_mem_out=mat(w_mem_out), w_out=mat(w_out), g_mix_post=vec(g_mix_post),
        g_ffn2_pre=vec(g_ffn2_pre), w_ffn2_in=mat(w_ffn2_in), w_ffn2_out=mat(w_ffn2_out),
        g_ffn2_post=vec(g_ffn2_post))


def kernel(x_prompt, x_sample, cache_k, cache_v, cache_mem_k, cache_mem_v, state_conv, page_table, mem_prompt, g_ffn1_pre, w_ffn1_in, w_ffn1_out, g_ffn1_post, g_mix_pre, w_in, b_gate, w_dwconv, b_dwconv, g_conv_ln, b_conv_ln, w_conv_out, w_attn_out, rel_bias, g_mem, w_mem_kv, w_mem_out, w_out, g_mix_post, g_ffn2_pre, w_ffn2_in, w_ffn2_out, g_ffn2_post):
    depth = w_in.shape[0]
    hp, hs = x_prompt, x_sample
    outs = [[] for _ in range(8)]
    for l in range(depth):
        p = _layer_params(l, g_ffn1_pre, w_ffn1_in, w_ffn1_out, g_ffn1_post, g_mix_pre, w_in, b_gate, w_dwconv,
                          b_dwconv, g_conv_ln, b_conv_ln, w_conv_out, w_attn_out, g_mem, w_mem_kv, w_mem_out,
                          w_out, g_mix_post, g_ffn2_pre, w_ffn2_in, w_ffn2_out, g_ffn2_post)
        hp, kp, vp, mkp, mvp, cp = _prompt_layer(hp, mem_prompt, rel_bias, p)
        hs, ks, vs, cs = _sample_layer(hs, state_conv[l], cache_k[l], cache_v[l], page_table, cache_mem_k[l],
                                       cache_mem_v[l], rel_bias, p)
        for acc, val in zip(outs, (kp, vp, mkp, mvp, cp, ks, vs, cs)):
            acc.append(val)
    return (hp, hs) + tuple(jnp.stack(a) for a in outs)
```

```python
import functools
import math

import jax
import jax.numpy as jnp
from jax import lax
from jax.experimental import pallas as pl
from jax.experimental.pallas import tpu as pltpu

_F32 = jnp.float32
_BF16 = jnp.bfloat16

_CONV_WIDTH = 31
_N_HEADS = 12
_HEAD_DIM = 64
_MOBA_BLOCK = 256
_MOBA_TOPK = 3
_MEM_HEADS = 4
_MEM_HEAD_DIM = 128
_N_BUCKETS = 32
_MAX_DISTANCE = 128
_N_BRANCH = 3
_PAGE_SIZE = 128
_NORM_EPS = 1e-6
_NEG_INF = -1e30

_LANES = 128
_SUBLANES = 8
_MXU_COLS_V7X = 256
_VMEM_BYTES_V7X = 64 * 1024 * 1024
_VMEM_LIMIT = _VMEM_BYTES_V7X - 8 * 1024 * 1024

_HALO_ROWS = 32
_CONV_CHUNK_ROWS = 64
_HEAD_GROUP = 16
_NEW_ROWS = 16
_QUERY_BLOCKS_PER_STEP = 2


def _nt_dot(a, b):
    return lax.dot_general(a, b, (((1,), (1,)), ((), ())), preferred_element_type=_F32)


def _dot(a, b):
    return jnp.dot(a, b, preferred_element_type=_F32)


def _sigmoid(x):
    return 1.0 / (1.0 + jnp.exp(-x))


def _rms_norm(x, g):
    return x * lax.rsqrt(jnp.mean(x * x, axis=-1, keepdims=True) + _NORM_EPS) * g


def _resident(shape):
    nd = len(shape)
    return pl.BlockSpec(shape, lambda *_: (0,) * nd, pipeline_mode=pl.Buffered(1))


def _params(n_parallel_axes):
    return pltpu.CompilerParams(
        dimension_semantics=("arbitrary",) * n_parallel_axes,
        vmem_limit_bytes=_VMEM_LIMIT)


def _ffn_half_step(x, g_pre, w_in_ref, w_out_ref, g_post, n_chunk):
    d_ff = w_out_ref.shape[0]
    fc = d_ff // n_chunk
    xn = _rms_norm(x, g_pre).astype(_BF16)
    acc = None
    for c in range(n_chunk):
        a = _dot(xn, w_in_ref[:, c * fc:(c + 1) * fc])
        b = _dot(xn, w_in_ref[:, d_ff + c * fc:d_ff + (c + 1) * fc])
        hid = (a * _sigmoid(a) * b).astype(_BF16)
        part = _dot(hid, w_out_ref[c * fc:(c + 1) * fc, :])
        acc = part if acc is None else acc + part
    return x + 0.5 * _rms_norm(acc, g_post)


def _ffn1_kernel(x_ref, gpre_ref, win_ref, wout_ref, gpost_ref, gmix_ref, h_ref, u_ref, *, n_chunk):
    h = _ffn_half_step(x_ref[...], gpre_ref[...], win_ref, wout_ref, gpost_ref[...], n_chunk)
    h_ref[...] = h
    u_ref[...] = _rms_norm(h, gmix_ref[...]).astype(_BF16)


def _ffn_chunks(d_ff):
    return d_ff // _MXU_COLS_V7X if d_ff % _MXU_COLS_V7X == 0 else 1


def _ffn1(x, g_pre, w_in, w_out, g_post, g_mix, tm):
    n, d = x.shape
    d_ff = w_out.shape[0]
    row = lambda i: (i, 0)
    return pl.pallas_call(
        functools.partial(_ffn1_kernel, n_chunk=_ffn_chunks(d_ff)),
        grid=(n // tm,),
        in_specs=[pl.BlockSpec((tm, d), row), _resident((1, d)), _resident(w_in.shape),
                  _resident(w_out.shape), _resident((1, d)), _resident((1, d))],
        out_specs=[pl.BlockSpec((tm, d), row), pl.BlockSpec((tm, d), row)],
        out_shape=[jax.ShapeDtypeStruct((n, d), _F32), jax.ShapeDtypeStruct((n, d), _BF16)],
        compiler_params=_params(1),
        name="ffn1",
    )(x, g_pre, w_in, w_out, g_post, g_mix)


def _inproj_kernel(u_ref, w_ref, bg_ref, *rest, conv_dim, attn_dim, mem_dim, whole_sequences):
    if whole_sequences:
        wd_ref, bd_ref, gln_ref, bln_ref = rest[:4]
        rest = rest[4:]
    conv_ref, q_ref, k_ref, v_ref, qm_ref, gate_ref = rest[:6]
    u = u_ref[0]
    tm = u.shape[0]
    o1 = 2 * conv_dim
    o2 = o1 + attn_dim
    o3 = o2 + attn_dim
    o4 = o3 + attn_dim
    o5 = o4 + mem_dim

    def proj(lo, hi):
        return _dot(u, w_ref[:, lo:hi])

    if whole_sequences:
        kb_ref, vt_ref, cact_ref, ext_ref, cbuf_ref = rest[6:]
        @pl.when(pl.program_id(1) == 0)
        def _():
            ext_ref[0:_HALO_ROWS, :] = jnp.zeros((_HALO_ROWS, conv_dim), _F32)

        @pl.when(pl.program_id(1) > 0)
        def _():
            ext_ref[0:_HALO_ROWS, :] = ext_ref[tm:tm + _HALO_ROWS, :]

    glu = proj(0, conv_dim) * _sigmoid(proj(conv_dim, o1))
    conv_ref[0] = glu
    if whole_sequences:
        ext_ref[_HALO_ROWS:, :] = glu
        for r0 in range(0, tm, _CONV_CHUNK_ROWS):
            _conv_module_rows(ext_ref, cbuf_ref, cact_ref, r0, _CONV_CHUNK_ROWS, wd_ref, bd_ref, gln_ref, bln_ref)
    q_ref[0] = (proj(o1, o2) * (_HEAD_DIM ** -0.5)).astype(_BF16)
    k = proj(o2, o3)
    v = proj(o3, o4)
    k_ref[0] = k
    v_ref[0] = v
    qm_ref[0] = proj(o4, o5).astype(_BF16)
    gate_ref[0] = _sigmoid(proj(o5, w_ref.shape[1]) + bg_ref[...]).astype(_BF16)
    if whole_sequences:
        kb_ref[0] = k.astype(_BF16)
        vt_ref[0] = v.T.astype(_BF16)


def _inproj(u, w_in, b_gate, tm, conv_dim, attn_dim, mem_dim, conv_params=None):
    nb, s, d = u.shape
    n_gate = b_gate.shape[1]
    whole_sequences = conv_params is not None
    tok = lambda b, i: (b, i, 0)
    in_specs = [pl.BlockSpec((1, tm, d), tok), _resident(w_in.shape), _resident((1, n_gate))]
    out_specs = [pl.BlockSpec((1, tm, conv_dim), tok), pl.BlockSpec((1, tm, attn_dim), tok),
                 pl.BlockSpec((1, tm, attn_dim), tok), pl.BlockSpec((1, tm, attn_dim), tok),
                 pl.BlockSpec((1, tm, mem_dim), tok), pl.BlockSpec((1, tm, n_gate), tok)]
    out_shape = [jax.ShapeDtypeStruct((nb, s, conv_dim), _F32), jax.ShapeDtypeStruct((nb, s, attn_dim), _BF16),
                 jax.ShapeDtypeStruct((nb, s, attn_dim), _F32), jax.ShapeDtypeStruct((nb, s, attn_dim), _F32),
                 jax.ShapeDtypeStruct((nb, s, mem_dim), _BF16), jax.ShapeDtypeStruct((nb, s, n_gate), _BF16)]
    scratch_shapes = []
    operands = [u, w_in, b_gate]
    if whole_sequences:
        assert tm % _CONV_CHUNK_ROWS == 0
        in_specs += [_resident(a.shape) for a in conv_params]
        operands += list(conv_params)
        out_specs += [pl.BlockSpec((1, tm, attn_dim), tok), pl.BlockSpec((1, attn_dim, tm), lambda b, i: (b, 0, i)),
                      pl.BlockSpec((1, tm, conv_dim), tok)]
        out_shape += [jax.ShapeDtypeStruct((nb, s, attn_dim), _BF16), jax.ShapeDtypeStruct((nb, attn_dim, s), _BF16),
                      jax.ShapeDtypeStruct((nb, s, conv_dim), _BF16)]
        scratch_shapes = [pltpu.VMEM((tm + _HALO_ROWS, conv_dim), _F32), pltpu.VMEM((tm, conv_dim), _F32)]
    return pl.pallas_call(
        functools.partial(_inproj_kernel, conv_dim=conv_dim, attn_dim=attn_dim, mem_dim=mem_dim,
                          whole_sequences=whole_sequences),
        grid=(nb, s // tm),
        in_specs=in_specs, out_specs=out_specs, out_shape=out_shape, scratch_shapes=scratch_shapes,
        compiler_params=_params(2),
        name="inproj",
    )(*operands)


def _memkv_kernel(mem_ref, g_ref, w_ref, mk_ref, mv_ref, mkb_ref, mvb_ref):
    kv = _dot(_rms_norm(mem_ref[0], g_ref[...]).astype(_BF16), w_ref[...])
    half = kv.shape[1] // 2
    mk_ref[0] = kv[:, :half]
    mv_ref[0] = kv[:, half:]
    mkb_ref[0] = kv[:, :half].astype(_BF16)
    mvb_ref[0] = kv[:, half:].astype(_BF16)


def _memkv(mem, g_mem, w_mem_kv):
    nb, m, d = mem.shape
    md = w_mem_kv.shape[1] // 2
    blk = lambda b: (b, 0, 0)
    return pl.pallas_call(
        _memkv_kernel,
        grid=(nb,),
        in_specs=[pl.BlockSpec((1, m, d), blk), _resident((1, d)), _resident(w_mem_kv.shape)],
        out_specs=[pl.BlockSpec((1, m, md), blk)] * 4,
        out_shape=[jax.ShapeDtypeStruct((nb, m, md), _F32)] * 2 + [jax.ShapeDtypeStruct((nb, m, md), _BF16)] * 2,
        compiler_params=_params(1),
        name="memkv",
    )(mem, g_mem, w_mem_kv)


def _ln_swish(c, g, b):
    mu = jnp.mean(c, axis=-1, keepdims=True)
    cc = c - mu
    var = jnp.mean(cc * cc, axis=-1, keepdims=True)
    y = cc * lax.rsqrt(var + _NORM_EPS) * g + b
    return y * _sigmoid(y)


def _conv_module_rows(ext_ref, cbuf_ref, out_ref, r0, rows, wd_ref, bd_ref, g_ref, b_ref):
    ch = ext_ref.shape[1]
    lead = _HALO_ROWS - (_CONV_WIDTH - 1)
    n_vreg = rows // _SUBLANES
    n_win = n_vreg + _HALO_ROWS // _SUBLANES
    sub = lax.broadcasted_iota(jnp.int32, (1, _SUBLANES, _LANES), 1)
    blocks = [slice(lb * _LANES, (lb + 1) * _LANES) for lb in range(ch // _LANES)]
    rs = slice(r0, r0 + rows)
    total = None
    for ls in blocks:
        win = ext_ref[r0:r0 + rows + _HALO_ROWS, ls].reshape(n_win, _SUBLANES, _LANES)
        acc = jnp.broadcast_to(bd_ref[:, ls], (n_vreg, _SUBLANES, _LANES))
        for r in range(_SUBLANES):
            n_a = (_CONV_WIDTH - r + _SUBLANES - 1) // _SUBLANES
            base, k = divmod(lead + r, _SUBLANES)
            need = n_vreg + n_a - 1
            if k == 0:
                sh = win[base:base + need]
            else:
                rot = pltpu.roll(win[base:base + need + 1], _SUBLANES - k, axis=1)
                sh = jnp.where(sub < _SUBLANES - k, rot[:-1], rot[1:])
            for a in range(n_a):
                w = _SUBLANES * a + r
                acc = acc + sh[a:a + n_vreg] * wd_ref[w:w + 1, ls]
        acc = acc.reshape(rows, _LANES)
        cbuf_ref[rs, ls] = acc
        total = acc if total is None else total + acc
    mu = jnp.sum(total, axis=-1, keepdims=True) * (1.0 / ch)
    sq = None
    for ls in blocks:
        d = cbuf_ref[rs, ls] - mu
        sq = d * d if sq is None else sq + d * d
    rstd = lax.rsqrt(jnp.sum(sq, axis=-1, keepdims=True) * (1.0 / ch) + _NORM_EPS)
    for ls in blocks:
        y = (cbuf_ref[rs, ls] - mu) * rstd * g_ref[:, ls] + b_ref[:, ls]
        out_ref[0, rs, ls] = (y * _sigmoid(y)).astype(_BF16)


def _conv_step_kernel(ext_ref, wd_ref, bd_ref, g_ref, b_ref, out_ref):
    n_q = out_ref.shape[0]
    for t in range(n_q):
        acc = jnp.broadcast_to(bd_ref[...], ext_ref.shape[1:])
        for w in range(_CONV_WIDTH):
            acc = acc + ext_ref[t + w] * wd_ref[w:w + 1, :]
        out_ref[t] = _ln_swish(acc, g_ref[...], b_ref[...]).astype(_BF16)


def _conv_step(ext_tm, w_dw, b_dw, g_ln, b_ln):
    n_rows, n_seq, ch = ext_tm.shape
    n_q = n_rows - (_CONV_WIDTH - 1)
    return pl.pallas_call(
        _conv_step_kernel,
        grid=(1,),
        in_specs=[_resident(ext_tm.shape), _resident(w_dw.shape), _resident((1, ch)), _resident((1, ch)),
                  _resident((1, ch))],
        out_specs=pl.BlockSpec((n_q, n_seq, ch), lambda i: (0, 0, 0)),
        out_shape=jax.ShapeDtypeStruct((n_q, n_seq, ch), _BF16),
        compiler_params=_params(1),
        name="conv_step",
    )(ext_tm, w_dw, b_dw, g_ln, b_ln)


def _rel_bucket(dist):
    n = jnp.maximum(dist, 0)
    max_exact = _N_BUCKETS // 2
    nf = jnp.maximum(n, 1).astype(_F32)
    large = max_exact + (jnp.log(nf / max_exact) / math.log(_MAX_DISTANCE / max_exact)
                         * (_N_BUCKETS - max_exact)).astype(jnp.int32)
    large = jnp.minimum(large, _N_BUCKETS - 1)
    return jnp.where(n < max_exact, n, large)


def _bias_of_dist(table, dist, per_row=False):
    n_col = table.shape[1]
    bucket = jnp.where(dist >= 0, _rel_bucket(dist), -1)
    if per_row:
        col_shape = (n_col,) + (1,) * (dist.ndim - 1)
    else:
        bucket = bucket[None]
        col_shape = (n_col,) + (1,) * dist.ndim
    out = jnp.full(jnp.broadcast_shapes(bucket.shape, col_shape), _NEG_INF, _F32)
    for b in range(_N_BUCKETS):
        out = jnp.where(bucket == b, table[b].reshape(col_shape), out)
    return out


def _topk_block_mask(score, blk, n_valid):
    rank = jnp.zeros(score.shape, jnp.int32)
    for m in range(score.shape[0]):
        sm = score[m:m + 1, :]
        counts = jnp.where(m < n_valid, 1, 0)
        rank = rank + jnp.where(sm > score, counts, 0) + jnp.where(sm == score, jnp.where(m < blk, counts, 0), 0)
    return (blk < n_valid) & (rank < _MOBA_TOPK)


def _moba_prompt_kernel(far_ref, q_ref, k_ref, vt_ref, bias_ref, o_ref, kmean_ref, pen_ref, s_buf, p_buf,
                        m_ref, l_ref, alpha_ref, acc_ref, *, n_blk):
    blk_len = _MOBA_BLOCK
    n_query = q_ref.shape[1]
    n_qb = n_query // blk_len
    hp = pl.program_id(1)
    g = pl.program_id(2)
    half = _HEAD_DIM
    owns = [n_qb * g + i for i in range(n_qb)]
    qcols = [slice(i * blk_len, (i + 1) * blk_len) for i in range(n_qb)]
    n_far = owns[0] // 2

    @pl.when(g == 0)
    def _():
        for j in range(n_blk):
            kj = k_ref[0, j * blk_len:(j + 1) * blk_len, :].astype(_F32)
            kmean_ref[j:j + 1, :] = jnp.mean(kj, axis=0, keepdims=True)

    qf = q_ref[0].astype(_F32)
    lane = lax.broadcasted_iota(jnp.int32, qf.shape, 1)
    blk = lax.broadcasted_iota(jnp.int32, (n_blk, n_query), 0)
    own_q = owns[0] + lax.broadcasted_iota(jnp.int32, (1, n_query), 1) // blk_len
    kmean = kmean_ref[...].astype(_BF16)

    qhs, adj_pens = [], []
    for hd in range(2):
        qh = jnp.where((lane >= hd * half) & (lane < (hd + 1) * half), qf, 0.0).astype(_BF16)
        sel = _topk_block_mask(_nt_dot(kmean, qh), blk, own_q)
        pen_ref[hd] = jnp.where(sel & (blk < own_q - 1), far_ref[hp * 2 + hd], _NEG_INF)
        sel_adj = jnp.sum(jnp.where(sel & (blk == own_q - 1), 1.0, 0.0), axis=0, keepdims=True)
        adj_pens.append(jnp.where(own_q == 0, 0.0, jnp.where(sel_adj > 0.5, 0.0, _NEG_INF)))
        qhs.append(qh)

    log2e = math.log2(math.e)
    lo, hi = slice(0, blk_len), slice(blk_len, 2 * blk_len)

    def block_start(j):
        return pl.multiple_of(j * blk_len, blk_len)

    stage0_blocks = [(block_start(jnp.maximum(o - 1, 0)), block_start(jnp.maximum(o, 1))) for o in owns]

    def blocks_before_far_stage(t):
        far_a = (2 * t - 2) * blk_len
        return [(pl.multiple_of(jnp.where(t == 0, a0, far_a), blk_len),
                 pl.multiple_of(jnp.where(t == 0, b0, far_a + blk_len), blk_len)) for a0, b0 in stage0_blocks]

    def stage0_logits_into(slot):
        for hd in range(2):
            for qb in range(n_qb):
                q_rows = qhs[hd][qcols[qb], :]
                a0, b0 = stage0_blocks[qb]
                s_buf[slot, hd, lo, qcols[qb]] = _nt_dot(k_ref[0, pl.ds(a0, blk_len), :], q_rows)
                s_buf[slot, hd, hi, qcols[qb]] = _nt_dot(k_ref[0, pl.ds(b0, blk_len), :], q_rows)

    def far_logits_into(slot, t):
        start = pl.multiple_of(2 * t * blk_len, 2 * blk_len)
        for hd in range(2):
            s_buf[slot, hd] = _nt_dot(k_ref[0, pl.ds(start, 2 * blk_len), :], qhs[hd])

    def pv_from(slot, blocks):
        res = []
        for hd in range(2):
            rows = slice(hd * half, (hd + 1) * half)
            parts = [_dot(vt_ref[0, rows, pl.ds(a, blk_len)], p_buf[slot, hd, lo, qcols[qb]])
                     + _dot(vt_ref[0, rows, pl.ds(b, blk_len)], p_buf[slot, hd, hi, qcols[qb]])
                     for qb, (a, b) in enumerate(blocks)]
            res.append(jnp.concatenate(parts, axis=1))
        return res

    def softmax_stage(s_slot, p_slot, hd, cols, row_a, row_b, m_old, tiles=None):
        s_a, s_b = s_buf[s_slot, hd, lo, cols], s_buf[s_slot, hd, hi, cols]
        if tiles is not None:
            s_a, s_b = s_a + tiles[0], s_b + tiles[1]
        m_new = jnp.maximum(jnp.max(s_a, axis=0, keepdims=True) + row_a, jnp.max(s_b, axis=0, keepdims=True) + row_b)
        if m_old is not None:
            m_new = jnp.maximum(m_new, m_old)
        p_a = jnp.exp2(s_a * log2e + (row_a - m_new) * log2e)
        p_b = jnp.exp2(s_b * log2e + (row_b - m_new) * log2e)
        p_buf[p_slot, hd, lo, cols] = p_a.astype(_BF16)
        p_buf[p_slot, hd, hi, cols] = p_b.astype(_BF16)
        return m_new, jnp.sum(p_a, axis=0, keepdims=True) + jnp.sum(p_b, axis=0, keepdims=True)

    stage0_logits_into(0)
    far_logits_into(1, 0)
    for hd in range(2):
        for qb in range(n_qb):
            first = owns[qb] == 0
            tiles = (bias_ref[hd, jnp.where(first, 0, 1)], bias_ref[hd, 0])
            m0, l0 = softmax_stage(0, 0, hd, qcols[qb], adj_pens[hd][:, qcols[qb]],
                                   jnp.where(first, _NEG_INF, 0.0), None, tiles=tiles)
            m_ref[hd, :, qcols[qb]] = m0
            l_ref[hd, :, qcols[qb]] = l0
        alpha_ref[hd] = jnp.ones((1, n_query), _F32)
        acc_ref[hd] = jnp.zeros((half, n_query), _F32)

    def far_stage(t, slot):
        pv = pv_from(slot, blocks_before_far_stage(t))
        for hd in range(2):
            acc_ref[hd] = alpha_ref[hd] * acc_ref[hd] + pv[hd]
            for cols in qcols:
                m_old = m_ref[hd, :, cols]
                m_new, l_blk = softmax_stage(1 - slot, 1 - slot, hd, cols, pen_ref[hd, pl.ds(2 * t, 1), cols],
                                             pen_ref[hd, pl.ds(2 * t + 1, 1), cols], m_old)
                alpha = jnp.exp(m_old - m_new)
                l_ref[hd, :, cols] = alpha * l_ref[hd, :, cols] + l_blk
                m_ref[hd, :, cols] = m_new
                alpha_ref[hd, :, cols] = alpha
        far_logits_into(slot, jnp.minimum(t + 1, n_blk // 2 - 1))

    def two_far_stages(u, carry):
        far_stage(2 * u, 0)
        far_stage(2 * u + 1, 1)
        return carry

    n_iter = (n_far + 1) // 2
    lax.fori_loop(0, n_iter, two_far_stages, 0)
    pv = pv_from(0, blocks_before_far_stage(2 * n_iter))
    outs = [(alpha_ref[hd] * acc_ref[hd] + pv[hd]) * (1.0 / l_ref[hd]) for hd in range(2)]
    o_ref[0] = jnp.concatenate(outs, axis=0).T.astype(_BF16)


def _moba_prompt(q, kb, vt, rel_bias):
    nb, s, attn_dim = q.shape
    n_blk = s // _MOBA_BLOCK
    assert n_blk >= 2 and n_blk % 2 == 0 and _QUERY_BLOCKS_PER_STEP in (1, 2)
    n_pair = attn_dim // _LANES
    n_query = _QUERY_BLOCKS_PER_STEP * _MOBA_BLOCK
    kk = jnp.arange(_MOBA_BLOCK, dtype=jnp.int32)[:, None]
    qq = jnp.arange(_MOBA_BLOCK, dtype=jnp.int32)[None, :]
    table = rel_bias.astype(_F32)
    tiles = _bias_of_dist(table, jnp.stack([qq - kk, _MOBA_BLOCK + qq - kk]))
    far = _bias_of_dist(table, jnp.full((1,), _MOBA_BLOCK + 1, jnp.int32))[:, 0]
    return pl.pallas_call(
        functools.partial(_moba_prompt_kernel, n_blk=n_blk),
        grid=(nb, n_pair, s // n_query),
        in_specs=[pl.BlockSpec(memory_space=pltpu.SMEM),
                  pl.BlockSpec((1, n_query, _LANES), lambda b, h, c: (b, c, h)),
                  pl.BlockSpec((1, s, _LANES), lambda b, h, c: (b, 0, h)),
                  pl.BlockSpec((1, _LANES, s), lambda b, h, c: (b, h, 0)),
                  pl.BlockSpec((2, 2, _MOBA_BLOCK, _MOBA_BLOCK), lambda b, h, c: (h, 0, 0, 0))],
        out_specs=pl.BlockSpec((1, n_query, _LANES), lambda b, h, c: (b, c, h)),
        out_shape=jax.ShapeDtypeStruct((nb, s, attn_dim), _BF16),
        scratch_shapes=[pltpu.VMEM((n_blk, _LANES), _F32),
                        pltpu.VMEM((2, n_blk, n_query), _F32),
                        pltpu.VMEM((2, 2, 2 * _MOBA_BLOCK, n_query), _F32),
                        pltpu.VMEM((2, 2, 2 * _MOBA_BLOCK, n_query), _BF16),
                        pltpu.VMEM((2, 1, n_query), _F32), pltpu.VMEM((2, 1, n_query), _F32),
                        pltpu.VMEM((2, 1, n_query), _F32),
                        pltpu.VMEM((2, _HEAD_DIM, n_query), _F32)],
        compiler_params=_params(3),
        name="moba_prompt",
    )(far, q, kb, vt, tiles)


def _moba_step_kernel(pt_ref, qbd_ref, knew_ref, vnew_ref, badj_ref, bfar_ref, bown_ref, *rest, n_pages, n_q):
    del pt_ref
    kt_pages = rest[:n_pages]
    vt_pages = rest[n_pages:2 * n_pages]
    o_ref = rest[2 * n_pages]
    ppb = _MOBA_BLOCK // _PAGE_SIZE
    n_blk = n_pages // ppb
    qbd = qbd_ref[0]
    feat = qbd.shape[1]

    def block_of(pages, j):
        return jnp.concatenate([pages[p][0] for p in range(j * ppb, (j + 1) * ppb)], axis=1).astype(_BF16)

    s_blk = [_dot(qbd, block_of(kt_pages, j)) for j in range(n_blk)]
    score = [jnp.sum(s, axis=1, keepdims=True) for s in s_blk]
    logits = []
    for n in range(n_blk):
        rank = jnp.zeros(score[n].shape, jnp.int32)
        for m in range(n_blk):
            if m < n:
                rank = rank + jnp.where(score[m] >= score[n], 1, 0)
            elif m > n:
                rank = rank + jnp.where(score[m] > score[n], 1, 0)
        bias = badj_ref[...] if n == n_blk - 1 else bfar_ref[:, 0:1]
        logits.append(jnp.where(rank < _MOBA_TOPK, s_blk[n] + bias, _NEG_INF))
    s_own = _nt_dot(qbd, knew_ref[0].astype(_BF16)) + bown_ref[:, 0:_NEW_ROWS]
    m_row = jnp.max(s_own, axis=1, keepdims=True)
    for lg in logits:
        m_row = jnp.maximum(m_row, jnp.max(lg, axis=1, keepdims=True))
    p_own = jnp.exp(s_own - m_row)
    probs = [jnp.exp(lg - m_row) for lg in logits]
    denom = jnp.sum(p_own, axis=1, keepdims=True)
    for p in probs:
        denom = denom + jnp.sum(p, axis=1, keepdims=True)
    inv = 1.0 / denom
    out = _dot((p_own * inv).astype(_BF16), vnew_ref[0].astype(_BF16))
    for j in range(n_blk):
        out = out + _nt_dot((probs[j] * inv).astype(_BF16), block_of(vt_pages, j))
    r_head = lax.broadcasted_iota(jnp.int32, out.shape, 0) % _HEAD_GROUP
    c_head = lax.broadcasted_iota(jnp.int32, out.shape, 1) // _HEAD_DIM
    out = jnp.where(r_head == c_head, out, 0.0)
    o_ref[0] = jnp.sum(out.reshape(n_q, _HEAD_GROUP, feat), axis=1)


def _block_diag_queries(q, n_heads, head_dim, group):
    n, n_q, feat = q.shape
    r_head = jnp.arange(group, dtype=jnp.int32)[:, None]
    c_head = jnp.arange(feat, dtype=jnp.int32)[None, :] // head_dim
    mask = (r_head == c_head) & (r_head < n_heads)
    return jnp.where(mask[None, None], q[:, :, None, :], jnp.zeros((), q.dtype)).reshape(n, n_q * group, feat)


def _moba_step(page_table, q, k_new, v_new, cache_kt, cache_vt, rel_bias):
    n_seq, n_q, feat = q.shape
    n_pages = page_table.shape[1]
    past = n_pages * _PAGE_SIZE
    assert past % _MOBA_BLOCK == 0 and n_q <= _NEW_ROWS
    rows = n_q * _HEAD_GROUP
    qbd = _block_diag_queries(q, _N_HEADS, _HEAD_DIM, _HEAD_GROUP)
    padn = ((0, 0), (0, _NEW_ROWS - n_q), (0, 0))
    k_new = jnp.pad(k_new, padn)
    v_new = jnp.pad(v_new, padn)
    table = jnp.tile(jnp.pad(rel_bias.astype(_F32), ((0, 0), (0, _HEAD_GROUP - _N_HEADS))), (1, n_q))
    qi = jnp.arange(rows, dtype=jnp.int32)[:, None] // _HEAD_GROUP
    kk = jnp.arange(_MOBA_BLOCK, dtype=jnp.int32)[None, :]
    badj = _bias_of_dist(table, _MOBA_BLOCK + qi - kk, per_row=True)
    bfar = _bias_of_dist(table, jnp.full((rows, _LANES), _MOBA_BLOCK + 1, jnp.int32), per_row=True)
    kj = jnp.arange(_LANES, dtype=jnp.int32)[None, :]
    bown = _bias_of_dist(table, jnp.where(kj < n_q, qi - kj, -1), per_row=True)

    page = lambda p: pl.BlockSpec((1, feat, _PAGE_SIZE), lambda i, pt, p=p: (pt[i, p], 0, 0))
    seq3 = lambda r: pl.BlockSpec((1, r, feat), lambda i, pt: (i, 0, 0))
    const2 = lambda shape: pl.BlockSpec(shape, lambda i, pt: (0, 0), pipeline_mode=pl.Buffered(1))
    grid_spec = pltpu.PrefetchScalarGridSpec(
        num_scalar_prefetch=1,
        grid=(n_seq,),
        in_specs=[seq3(rows), seq3(_NEW_ROWS), seq3(_NEW_ROWS), const2((rows, _MOBA_BLOCK)),
                  const2((rows, _LANES)), const2((rows, _LANES))] + [page(p) for p in range(n_pages)] * 2,
        out_specs=pl.BlockSpec((1, n_q, feat), lambda i, pt: (i, 0, 0)))
    return pl.pallas_call(
        functools.partial(_moba_step_kernel, n_pages=n_pages, n_q=n_q),
        grid_spec=grid_spec,
        out_shape=jax.ShapeDtypeStruct((n_seq, n_q, feat), _F32),
        compiler_params=_params(1),
        name="moba_step",
    )(page_table, qbd, k_new, v_new, badj, bfar, bown, *([cache_kt] * n_pages), *([cache_vt] * n_pages))


def _softmax_rows(s):
    p = jnp.exp(s - jnp.max(s, axis=1, keepdims=True))
    return p * (1.0 / jnp.sum(p, axis=1, keepdims=True))


def _memattn_kernel(qm_ref, mk_ref, mv_ref, o_ref):
    scale = _MEM_HEAD_DIM ** -0.5
    for h in range(_MEM_HEADS):
        hs = slice(h * _MEM_HEAD_DIM, (h + 1) * _MEM_HEAD_DIM)
        p = _softmax_rows(_nt_dot(qm_ref[0, :, hs], mk_ref[0, :, hs]) * scale)
        o_ref[0, :, hs] = _dot(p.astype(_BF16), mv_ref[0, :, hs]).astype(_BF16)


def _memattn(qm, mkb, mvb, tq):
    nb, s, md = qm.shape
    m = mkb.shape[1]
    return pl.pallas_call(
        _memattn_kernel,
        grid=(nb, s // tq),
        in_specs=[pl.BlockSpec((1, tq, md), lambda b, i: (b, i, 0)),
                  pl.BlockSpec((1, m, md), lambda b, i: (b, 0, 0)),
                  pl.BlockSpec((1, m, md), lambda b, i: (b, 0, 0))],
        out_specs=pl.BlockSpec((1, tq, md), lambda b, i: (b, i, 0)),
        out_shape=jax.ShapeDtypeStruct((nb, s, md), _BF16),
        compiler_params=_params(2),
        name="memattn",
    )(qm, mkb, mvb)


def _memattn_step_kernel(q_ref, mk_ref, mv_ref, o_ref):
    scale = _MEM_HEAD_DIM ** -0.5
    for i in range(mk_ref.shape[0]):
        s = _nt_dot(q_ref[i], mk_ref[i].astype(_BF16)) * scale
        r_head = lax.broadcasted_iota(jnp.int32, s.shape, 0) % _MEM_HEADS
        c_head = lax.broadcasted_iota(jnp.int32, s.shape, 1) % _MEM_HEADS
        p = _softmax_rows(jnp.where(r_head == c_head, s, _NEG_INF))
        o_ref[i] = _dot(p.astype(_BF16), mv_ref[i].astype(_BF16))


def _memattn_step(qm, mem_k, mem_v, seqs_per_step):
    n_seq, rows, dh = qm.shape
    mh = mem_k.shape[1]
    blk = lambda i: (i, 0, 0)
    return pl.pallas_call(
        _memattn_step_kernel,
        grid=(n_seq // seqs_per_step,),
        in_specs=[pl.BlockSpec((seqs_per_step, rows, dh), blk), pl.BlockSpec((seqs_per_step, mh, dh), blk),
                  pl.BlockSpec((seqs_per_step, mh, dh), blk)],
        out_specs=pl.BlockSpec((seqs_per_step, rows, dh), blk),
        out_shape=jax.ShapeDtypeStruct((n_seq, rows, dh), _F32),
        compiler_params=_params(1),
        name="memattn_step",
    )(qm, mem_k, mem_v)


def _merge_kernel(h_ref, c_ref, o_ref, om_ref, gate_ref, wc_ref, wa_ref, wm_ref, wo_ref, gpost_ref,
                  g2pre_ref, w2in_ref, w2out_ref, g2post_ref, y_ref, *, n_chunk):
    d = h_ref.shape[1]
    merged = (gate_ref[:, 0:d].astype(_F32) * _dot(c_ref[...], wc_ref[...])
              + gate_ref[:, d:2 * d].astype(_F32) * _dot(o_ref[...], wa_ref[...])
              + gate_ref[:, 2 * d:3 * d].astype(_F32) * _dot(om_ref[...], wm_ref[...]))
    h = h_ref[...] + _rms_norm(_dot(merged.astype(_BF16), wo_ref[...]), gpost_ref[...])
    y_ref[...] = _ffn_half_step(h, g2pre_ref[...], w2in_ref, w2out_ref, g2post_ref[...], n_chunk)


def _merge(h, c_act, o, om, gates, w_conv_out, w_attn_out, w_mem_out, w_out, g_post, g2_pre, w2_in, w2_out,
           g2_post, tm):
    n, d = h.shape
    row = lambda i: (i, 0)
    tile = lambda a: pl.BlockSpec((tm, a.shape[1]), row)
    return pl.pallas_call(
        functools.partial(_merge_kernel, n_chunk=_ffn_chunks(w2_out.shape[0])),
        grid=(n // tm,),
        in_specs=[tile(h), tile(c_act), tile(o), tile(om), tile(gates),
                  _resident(w_conv_out.shape), _resident(w_attn_out.shape), _resident(w_mem_out.shape),
                  _resident(w_out.shape), _resident((1, d)), _resident((1, d)), _resident(w2_in.shape),
                  _resident(w2_out.shape), _resident((1, d))],
        out_specs=pl.BlockSpec((tm, d), row),
        out_shape=jax.ShapeDtypeStruct((n, d), _F32),
        compiler_params=_params(1),
        name="merge_ffn2",
    )(h, c_act, o, om, gates, w_conv_out, w_attn_out, w_mem_out, w_out, g_post, g2_pre, w2_in, w2_out, g2_post)


def _tile(n, want):
    t = min(n, want)
    assert n % t == 0
    return t


def _token_front(x, p):
    nb, s, d = x.shape
    n = nb * s
    h, u = _ffn1(x.reshape(n, d), p['g_ffn1_pre'], p['w_ffn1_in'], p['w_ffn1_out'], p['g_ffn1_post'],
                 p['g_mix_pre'], _tile(n, 512))
    return h, u.reshape(nb, s, d)


def _token_back(h, c_act, o, om, gates, p):
    n = h.shape[0]
    flat = lambda a: a.reshape(n, a.shape[-1])
    return _merge(h, flat(c_act), flat(o), flat(om), flat(gates), p['w_conv_out'], p['w_attn_out'],
                  p['w_mem_out'], p['w_out'], p['g_mix_post'], p['g_ffn2_pre'], p['w_ffn2_in'], p['w_ffn2_out'],
                  p['g_ffn2_post'], _tile(n, 512))


def _dims(p):
    conv_dim = p['w_conv_out'].shape[0]
    attn_dim = p['w_attn_out'].shape[0]
    mem_dim = p['w_mem_out'].shape[0]
    return conv_dim, attn_dim, mem_dim


def _prompt_layer(x, mem, rel_bias, p):
    nb, s, d = x.shape
    conv_dim, attn_dim, mem_dim = _dims(p)
    h, u = _token_front(x, p)
    glu, q, k, v, qm, gates, kb, vt, c_act = _inproj(
        u, p['w_in'], p['b_gate'], _tile(s, 512), conv_dim, attn_dim, mem_dim,
        conv_params=(p['w_dwconv'], p['b_dwconv'], p['g_conv_ln'], p['b_conv_ln']))
    mk, mv, mkb, mvb = _memkv(mem, p['g_mem'], p['w_mem_kv'])
    o = _moba_prompt(q, kb, vt, rel_bias)
    om = _memattn(qm, mkb, mvb, _tile(s, 512))
    y = _token_back(h, c_act, o, om, gates, p).reshape(nb, s, d)
    conv_state = glu[:, s - (_CONV_WIDTH - 1):]
    heads = lambda a: a.reshape(nb, s, _N_HEADS, _HEAD_DIM)
    mheads = lambda a: a.reshape(nb, a.shape[1], _MEM_HEADS, _MEM_HEAD_DIM)
    return y, heads(k), heads(v), mheads(mk), mheads(mv), conv_state


def _sample_layer(x, conv_prev, cache_k, cache_v, page_table, mem_k, mem_v, rel_bias, p):
    n_seq, n_q, d = x.shape
    conv_dim, attn_dim, mem_dim = _dims(p)
    n = n_seq * n_q
    h, u = _token_front(x.reshape(1, n, d), p)
    glu, q, k, v, qm, gates = _inproj(u, p['w_in'], p['b_gate'], _tile(n, 512), conv_dim, attn_dim, mem_dim)
    per_seq = lambda a: a.reshape(n_seq, n_q, a.shape[-1])
    glu = per_seq(glu)
    conv_in = jnp.concatenate([conv_prev, glu], axis=1)
    c_act = _conv_step(conv_in.transpose(1, 0, 2), p['w_dwconv'], p['b_dwconv'], p['g_conv_ln'], p['b_conv_ln'])
    c_act = c_act.transpose(1, 0, 2)
    page_t = lambda c: c.transpose(0, 2, 3, 1).reshape(c.shape[0], attn_dim, c.shape[1])
    o = _moba_step(page_table, per_seq(q), per_seq(k), per_seq(v), page_t(cache_k), page_t(cache_v), rel_bias)
    head_rows = lambda a: a.reshape(n_seq, -1, _MEM_HEAD_DIM)
    om = _memattn_step(head_rows(qm), head_rows(mem_k), head_rows(mem_v), 8)
    y = _token_back(h, c_act, o.astype(_BF16), om.reshape(n, mem_dim).astype(_BF16), gates, p)
    conv_state = conv_in[:, n_q:]
    heads = lambda a: a.reshape(n_seq, n_q, _N_HEADS, _HEAD_DIM)
    return y.reshape(n_seq, n_q, d), heads(k), heads(v), conv_state


def _layer_params(l, g_ffn1_pre, w_ffn1_in, w_ffn1_out, g_ffn1_post, g_mix_pre, w_in, b_gate, w_dwconv, b_dwconv,
                  g_conv_ln, b_conv_ln, w_conv_out, w_attn_out, g_mem, w_mem_kv, w_mem_out, w_out, g_mix_post,
                  g_ffn2_pre, w_ffn2_in, w_ffn2_out, g_ffn2_post):
    vec = lambda a: a[l].astype(_F32).reshape(1, -1)
    mat = lambda a: a[l].astype(_BF16)
    return dict(
        g_ffn1_pre=vec(g_ffn1_pre), w_ffn1_in=mat(w_ffn1_in), w_ffn1_out=mat(w_ffn1_out),
        g_ffn1_post=vec(g_ffn1_post), g_mix_pre=vec(g_mix_pre), w_in=mat(w_in), b_gate=vec(b_gate),
        w_dwconv=w_dwconv[l].astype(_F32), b_dwconv=vec(b_dwconv), g_conv_ln=vec(g_conv_ln),
        b_conv_ln=vec(b_conv_ln), w_conv_out=mat(w_conv_out), w_attn_out=mat(w_attn_out), g_mem=vec(g_mem),
        w_mem_kv=mat(w_mem_kv), w_mem_out=mat(w_mem_out), w_out=mat(w_out), g_mix_post=vec(g_mix_post),
        g_ffn2_pre=vec(g_ffn2_pre), w_ffn2_in=mat(w_ffn2_in), w_ffn2_out=mat(w_ffn2_out),
        g_ffn2_post=vec(g_ffn2_post))


def kernel(x_prompt, x_sample, cache_k, cache_v, cache_mem_k, cache_mem_v, state_conv, page_table, mem_prompt, g_ffn1_pre, w_ffn1_in, w_ffn1_out, g_ffn1_post, g_mix_pre, w_in, b_gate, w_dwconv, b_dwconv, g_conv_ln, b_conv_ln, w_conv_out, w_attn_out, rel_bias, g_mem, w_mem_kv, w_mem_out, w_out, g_mix_post, g_ffn2_pre, w_ffn2_in, w_ffn2_out, g_ffn2_post):
    depth = w_in.shape[0]
    hp, hs = x_prompt, x_sample
    outs = [[] for _ in range(8)]
    for l in range(depth):
        p = _layer_params(l, g_ffn1_pre, w_ffn1_in, w_ffn1_out, g_ffn1_post, g_mix_pre, w_in, b_gate, w_dwconv,
                          b_dwconv, g_conv_ln, b_conv_ln, w_conv_out, w_attn_out, g_mem, w_mem_kv, w_mem_out,
                          w_out, g_mix_post, g_ffn2_pre, w_ffn2_in, w_ffn2_out, g_ffn2_post)
        hp, kp, vp, mkp, mvp, cp = _prompt_layer(hp, mem_prompt, rel_bias, p)
        hs, ks, vs, cs = _sample_layer(hs, state_conv[l], cache_k[l], cache_v[l], page_table, cache_mem_k[l],
                                       cache_mem_v[l], rel_bias, p)
        for acc, val in zip(outs, (kp, vp, mkp, mvp, cp, ks, vs, cs)):
            acc.append(val)
    return (hp, hs) + tuple(jnp.stack(a) for a in outs)
```

```python
import functools
import math

import jax
import jax.numpy as jnp
from jax import lax
from jax.experimental import pallas as pl
from jax.experimental.pallas import tpu as pltpu

_F32 = jnp.float32
_BF16 = jnp.bfloat16

_CONV_WIDTH = 31
_N_HEADS = 12
_HEAD_DIM = 64
_MOBA_BLOCK = 256
_MOBA_TOPK = 3
_MEM_HEADS = 4
_MEM_HEAD_DIM = 128
_N_BUCKETS = 32
_MAX_DISTANCE = 128
_N_BRANCH = 3
_PAGE_SIZE = 128
_NORM_EPS = 1e-6
_NEG_INF = -1e30

_LANES = 128
_SUBLANES = 8
_MXU_COLS_V7X = 256
_VMEM_BYTES_V7X = 64 * 1024 * 1024
_VMEM_LIMIT = _VMEM_BYTES_V7X - 8 * 1024 * 1024

_HALO_ROWS = 32
_CONV_CHUNK_ROWS = 64
_HEAD_GROUP = 16
_NEW_ROWS = 16
_QUERY_BLOCKS_PER_STEP = 2


def _nt_dot(a, b):
    return lax.dot_general(a, b, (((1,), (1,)), ((), ())), preferred_element_type=_F32)


def _dot(a, b):
    return jnp.dot(a, b, preferred_element_type=_F32)


def _sigmoid(x):
    return 1.0 / (1.0 + jnp.exp(-x))


def _rms_norm(x, g):
    return x * lax.rsqrt(jnp.mean(x * x, axis=-1, keepdims=True) + _NORM_EPS) * g


def _resident(shape):
    nd = len(shape)
    return pl.BlockSpec(shape, lambda *_: (0,) * nd, pipeline_mode=pl.Buffered(1))


def _params(n_parallel_axes):
    return pltpu.CompilerParams(
        dimension_semantics=("arbitrary",) * n_parallel_axes,
        vmem_limit_bytes=_VMEM_LIMIT)


def _ffn_half_step(x, g_pre, w_in_ref, w_out_ref, g_post, n_chunk):
    d_ff = w_out_ref.shape[0]
    fc = d_ff // n_chunk
    xn = _rms_norm(x, g_pre).astype(_BF16)
    acc = None
    for c in range(n_chunk):
        a = _dot(xn, w_in_ref[:, c * fc:(c + 1) * fc])
        b = _dot(xn, w_in_ref[:, d_ff + c * fc:d_ff + (c + 1) * fc])
        hid = (a * _sigmoid(a) * b).astype(_BF16)
        part = _dot(hid, w_out_ref[c * fc:(c + 1) * fc, :])
        acc = part if acc is None else acc + part
    return x + 0.5 * _rms_norm(acc, g_post)


def _ffn1_kernel(x_ref, gpre_ref, win_ref, wout_ref, gpost_ref, gmix_ref, h_ref, u_ref, *, n_chunk):
    h = _ffn_half_step(x_ref[...], gpre_ref[...], win_ref, wout_ref, gpost_ref[...], n_chunk)
    h_ref[...] = h
    u_ref[...] = _rms_norm(h, gmix_ref[...]).astype(_BF16)


def _ffn_chunks(d_ff):
    return d_ff // _MXU_COLS_V7X if d_ff % _MXU_COLS_V7X == 0 else 1


def _ffn1(x, g_pre, w_in, w_out, g_post, g_mix, tm):
    n, d = x.shape
    d_ff = w_out.shape[0]
    row = lambda i: (i, 0)
    return pl.pallas_call(
        functools.partial(_ffn1_kernel, n_chunk=_ffn_chunks(d_ff)),
        grid=(n // tm,),
        in_specs=[pl.BlockSpec((tm, d), row), _resident((1, d)), _resident(w_in.shape),
                  _resident(w_out.shape), _resident((1, d)), _resident((1, d))],
        out_specs=[pl.BlockSpec((tm, d), row), pl.BlockSpec((tm, d), row)],
        out_shape=[jax.ShapeDtypeStruct((n, d), _F32), jax.ShapeDtypeStruct((n, d), _BF16)],
        compiler_params=_params(1),
        name="ffn1",
    )(x, g_pre, w_in, w_out, g_post, g_mix)


def _inproj_kernel(u_ref, w_ref, bg_ref, *rest, conv_dim, attn_dim, mem_dim, whole_sequences):
    if whole_sequences:
        wd_ref, bd_ref, gln_ref, bln_ref = rest[:4]
        rest = rest[4:]
    conv_ref, q_ref, k_ref, v_ref, qm_ref, gate_ref = rest[:6]
    u = u_ref[0]
    tm = u.shape[0]
    o1 = 2 * conv_dim
    o2 = o1 + attn_dim
    o3 = o2 + attn_dim
    o4 = o3 + attn_dim
    o5 = o4 + mem_dim

    def proj(lo, hi):
        return _dot(u, w_ref[:, lo:hi])

    if whole_sequences:
        kb_ref, vt_ref, cact_ref, ext_ref, cbuf_ref = rest[6:]
        @pl.when(pl.program_id(1) == 0)
        def _():
            ext_ref[0:_HALO_ROWS, :] = jnp.zeros((_HALO_ROWS, conv_dim), _F32)

        @pl.when(pl.program_id(1) > 0)
        def _():
            ext_ref[0:_HALO_ROWS, :] = ext_ref[tm:tm + _HALO_ROWS, :]

    glu = proj(0, conv_dim) * _sigmoid(proj(conv_dim, o1))
    conv_ref[0] = glu
    if whole_sequences:
        ext_ref[_HALO_ROWS:, :] = glu
        for r0 in range(0, tm, _CONV_CHUNK_ROWS):
            _conv_module_rows(ext_ref, cbuf_ref, cact_ref, r0, _CONV_CHUNK_ROWS, wd_ref, bd_ref, gln_ref, bln_ref)
    q = proj(o1, o2) * (_HEAD_DIM ** -0.5)
    q_ref[0] = (q.T if whole_sequences else q).astype(_BF16)
    k = proj(o2, o3)
    v = proj(o3, o4)
    k_ref[0] = k
    v_ref[0] = v
    qm_ref[0] = proj(o4, o5).astype(_BF16)
    gate_ref[0] = _sigmoid(proj(o5, w_ref.shape[1]) + bg_ref[...]).astype(_BF16)
    if whole_sequences:
        kb_ref[0] = k.astype(_BF16)
        vt_ref[0] = v.T.astype(_BF16)


def _inproj(u, w_in, b_gate, tm, conv_dim, attn_dim, mem_dim, conv_params=None):
    nb, s, d = u.shape
    n_gate = b_gate.shape[1]
    whole_sequences = conv_params is not None
    tok = lambda b, i: (b, i, 0)
    in_specs = [pl.BlockSpec((1, tm, d), tok), _resident(w_in.shape), _resident((1, n_gate))]
    out_specs = [pl.BlockSpec((1, tm, conv_dim), tok), pl.BlockSpec((1, tm, attn_dim), tok),
                 pl.BlockSpec((1, tm, attn_dim), tok), pl.BlockSpec((1, tm, attn_dim), tok),
                 pl.BlockSpec((1, tm, mem_dim), tok), pl.BlockSpec((1, tm, n_gate), tok)]
    out_shape = [jax.ShapeDtypeStruct((nb, s, conv_dim), _F32), jax.ShapeDtypeStruct((nb, s, attn_dim), _BF16),
                 jax.ShapeDtypeStruct((nb, s, attn_dim), _F32), jax.ShapeDtypeStruct((nb, s, attn_dim), _F32),
                 jax.ShapeDtypeStruct((nb, s, mem_dim), _BF16), jax.ShapeDtypeStruct((nb, s, n_gate), _BF16)]
    scratch_shapes = []
    operands = [u, w_in, b_gate]
    if whole_sequences:
        assert tm % _CONV_CHUNK_ROWS == 0
        out_specs[1] = pl.BlockSpec((1, attn_dim, tm), lambda b, i: (b, 0, i))
        out_shape[1] = jax.ShapeDtypeStruct((nb, attn_dim, s), _BF16)
        in_specs += [_resident(a.shape) for a in conv_params]
        operands += list(conv_params)
        out_specs += [pl.BlockSpec((1, tm, attn_dim), tok), pl.BlockSpec((1, attn_dim, tm), lambda b, i: (b, 0, i)),
                      pl.BlockSpec((1, tm, conv_dim), tok)]
        out_shape += [jax.ShapeDtypeStruct((nb, s, attn_dim), _BF16), jax.ShapeDtypeStruct((nb, attn_dim, s), _BF16),
                      jax.ShapeDtypeStruct((nb, s, conv_dim), _BF16)]
        scratch_shapes = [pltpu.VMEM((tm + _HALO_ROWS, conv_dim), _F32), pltpu.VMEM((tm, conv_dim), _F32)]
    return pl.pallas_call(
        functools.partial(_inproj_kernel, conv_dim=conv_dim, attn_dim=attn_dim, mem_dim=mem_dim,
                          whole_sequences=whole_sequences),
        grid=(nb, s // tm),
        in_specs=in_specs, out_specs=out_specs, out_shape=out_shape, scratch_shapes=scratch_shapes,
        compiler_params=_params(2),
        name="inproj",
    )(*operands)


def _memkv_kernel(mem_ref, g_ref, w_ref, mk_ref, mv_ref, mkb_ref, mvb_ref):
    kv = _dot(_rms_norm(mem_ref[0], g_ref[...]).astype(_BF16), w_ref[...])
    half = kv.shape[1] // 2
    mk_ref[0] = kv[:, :half]
    mv_ref[0] = kv[:, half:]
    mkb_ref[0] = kv[:, :half].astype(_BF16)
    mvb_ref[0] = kv[:, half:].astype(_BF16)


def _memkv(mem, g_mem, w_mem_kv):
    nb, m, d = mem.shape
    md = w_mem_kv.shape[1] // 2
    blk = lambda b: (b, 0, 0)
    return pl.pallas_call(
        _memkv_kernel,
        grid=(nb,),
        in_specs=[pl.BlockSpec((1, m, d), blk), _resident((1, d)), _resident(w_mem_kv.shape)],
        out_specs=[pl.BlockSpec((1, m, md), blk)] * 4,
        out_shape=[jax.ShapeDtypeStruct((nb, m, md), _F32)] * 2 + [jax.ShapeDtypeStruct((nb, m, md), _BF16)] * 2,
        compiler_params=_params(1),
        name="memkv",
    )(mem, g_mem, w_mem_kv)


def _ln_swish(c, g, b):
    mu = jnp.mean(c, axis=-1, keepdims=True)
    cc = c - mu
    var = jnp.mean(cc * cc, axis=-1, keepdims=True)
    y = cc * lax.rsqrt(var + _NORM_EPS) * g + b
    return y * _sigmoid(y)


def _conv_module_rows(ext_ref, cbuf_ref, out_ref, r0, rows, wd_ref, bd_ref, g_ref, b_ref):
    ch = ext_ref.shape[1]
    lead = _HALO_ROWS - (_CONV_WIDTH - 1)
    n_vreg = rows // _SUBLANES
    n_win = n_vreg + _HALO_ROWS // _SUBLANES
    sub = lax.broadcasted_iota(jnp.int32, (1, _SUBLANES, _LANES), 1)
    blocks = [slice(lb * _LANES, (lb + 1) * _LANES) for lb in range(ch // _LANES)]
    rs = slice(r0, r0 + rows)
    total = None
    for ls in blocks:
        win = ext_ref[r0:r0 + rows + _HALO_ROWS, ls].reshape(n_win, _SUBLANES, _LANES)
        acc = jnp.broadcast_to(bd_ref[:, ls], (n_vreg, _SUBLANES, _LANES))
        for r in range(_SUBLANES):
            n_a = (_CONV_WIDTH - r + _SUBLANES - 1) // _SUBLANES
            base, k = divmod(lead + r, _SUBLANES)
            need = n_vreg + n_a - 1
            if k == 0:
                sh = win[base:base + need]
            else:
                rot = pltpu.roll(win[base:base + need + 1], _SUBLANES - k, axis=1)
                sh = jnp.where(sub < _SUBLANES - k, rot[:-1], rot[1:])
            for a in range(n_a):
                w = _SUBLANES * a + r
                acc = acc + sh[a:a + n_vreg] * wd_ref[w:w + 1, ls]
        acc = acc.reshape(rows, _LANES)
        cbuf_ref[rs, ls] = acc
        total = acc if total is None else total + acc
    mu = jnp.sum(total, axis=-1, keepdims=True) * (1.0 / ch)
    sq = None
    for ls in blocks:
        d = cbuf_ref[rs, ls] - mu
        sq = d * d if sq is None else sq + d * d
    rstd = lax.rsqrt(jnp.sum(sq, axis=-1, keepdims=True) * (1.0 / ch) + _NORM_EPS)
    for ls in blocks:
        y = (cbuf_ref[rs, ls] - mu) * rstd * g_ref[:, ls] + b_ref[:, ls]
        out_ref[0, rs, ls] = (y * _sigmoid(y)).astype(_BF16)


def _conv_step_kernel(ext_ref, wd_ref, bd_ref, g_ref, b_ref, out_ref):
    n_q = out_ref.shape[0]
    for t in range(n_q):
        acc = jnp.broadcast_to(bd_ref[...], ext_ref.shape[1:])
        for w in range(_CONV_WIDTH):
            acc = acc + ext_ref[t + w] * wd_ref[w:w + 1, :]
        out_ref[t] = _ln_swish(acc, g_ref[...], b_ref[...]).astype(_BF16)


def _conv_step(ext_tm, w_dw, b_dw, g_ln, b_ln):
    n_rows, n_seq, ch = ext_tm.shape
    n_q = n_rows - (_CONV_WIDTH - 1)
    return pl.pallas_call(
        _conv_step_kernel,
        grid=(1,),
        in_specs=[_resident(ext_tm.shape), _resident(w_dw.shape), _resident((1, ch)), _resident((1, ch)),
                  _resident((1, ch))],
        out_specs=pl.BlockSpec((n_q, n_seq, ch), lambda i: (0, 0, 0)),
        out_shape=jax.ShapeDtypeStruct((n_q, n_seq, ch), _BF16),
        compiler_params=_params(1),
        name="conv_step",
    )(ext_tm, w_dw, b_dw, g_ln, b_ln)


def _rel_bucket(dist):
    n = jnp.maximum(dist, 0)
    max_exact = _N_BUCKETS // 2
    nf = jnp.maximum(n, 1).astype(_F32)
    large = max_exact + (jnp.log(nf / max_exact) / math.log(_MAX_DISTANCE / max_exact)
                         * (_N_BUCKETS - max_exact)).astype(jnp.int32)
    large = jnp.minimum(large, _N_BUCKETS - 1)
    return jnp.where(n < max_exact, n, large)


def _bias_of_dist(table, dist, per_row=False):
    n_col = table.shape[1]
    bucket = jnp.where(dist >= 0, _rel_bucket(dist), -1)
    if per_row:
        col_shape = (n_col,) + (1,) * (dist.ndim - 1)
    else:
        bucket = bucket[None]
        col_shape = (n_col,) + (1,) * dist.ndim
    out = jnp.full(jnp.broadcast_shapes(bucket.shape, col_shape), _NEG_INF, _F32)
    for b in range(_N_BUCKETS):
        out = jnp.where(bucket == b, table[b].reshape(col_shape), out)
    return out


def _topk_block_mask(score, blk, n_valid):
    rank = jnp.zeros(score.shape, jnp.int32)
    for m in range(score.shape[0]):
        sm = score[m:m + 1, :]
        counts = jnp.where(m < n_valid, 1, 0)
        rank = rank + jnp.where(sm > score, counts, 0) + jnp.where(sm == score, jnp.where(m < blk, counts, 0), 0)
    return (blk < n_valid) & (rank < _MOBA_TOPK)


def _moba_prompt_kernel(far_ref, q_ref, k_ref, vt_ref, bias_ref, o_ref, kmean_ref, pen_ref, s_buf, p_buf,
                        m_ref, l_ref, alpha_ref, acc_ref, *, n_blk):
    blk_len = _MOBA_BLOCK
    n_query = q_ref.shape[2]
    n_qb = n_query // blk_len
    hp = pl.program_id(1)
    g = pl.program_id(2)
    half = _HEAD_DIM
    owns = [n_qb * g + i for i in range(n_qb)]
    qcols = [slice(i * blk_len, (i + 1) * blk_len) for i in range(n_qb)]
    n_far = owns[0] // 2

    @pl.when(g == 0)
    def _():
        for j in range(n_blk):
            kj = k_ref[0, j * blk_len:(j + 1) * blk_len, :].astype(_F32)
            kmean_ref[j:j + 1, :] = jnp.mean(kj, axis=0, keepdims=True)

    qf = q_ref[0].astype(_F32)
    feature = lax.broadcasted_iota(jnp.int32, qf.shape, 0)
    blk = lax.broadcasted_iota(jnp.int32, (n_blk, n_query), 0)
    own_q = owns[0] + lax.broadcasted_iota(jnp.int32, (1, n_query), 1) // blk_len
    kmean = kmean_ref[...].astype(_BF16)

    qhs, adj_pens = [], []
    for hd in range(2):
        qh = jnp.where((feature >= hd * half) & (feature < (hd + 1) * half), qf, 0.0).astype(_BF16)
        sel = _topk_block_mask(_dot(kmean, qh), blk, own_q)
        pen_ref[hd] = jnp.where(sel & (blk < own_q - 1), far_ref[hp * 2 + hd], _NEG_INF)
        sel_adj = jnp.sum(jnp.where(sel & (blk == own_q - 1), 1.0, 0.0), axis=0, keepdims=True)
        adj_pens.append(jnp.where(own_q == 0, 0.0, jnp.where(sel_adj > 0.5, 0.0, _NEG_INF)))
        qhs.append(qh)

    log2e = math.log2(math.e)
    lo, hi = slice(0, blk_len), slice(blk_len, 2 * blk_len)

    def block_start(j):
        return pl.multiple_of(j * blk_len, blk_len)

    stage0_blocks = [(block_start(jnp.maximum(o - 1, 0)), block_start(jnp.maximum(o, 1))) for o in owns]

    def blocks_before_far_stage(t):
        far_a = (2 * t - 2) * blk_len
        return [(pl.multiple_of(jnp.where(t == 0, a0, far_a), blk_len),
                 pl.multiple_of(jnp.where(t == 0, b0, far_a + blk_len), blk_len)) for a0, b0 in stage0_blocks]

    def stage0_logits_into(slot):
        for hd in range(2):
            for qb in range(n_qb):
                q_cols = qhs[hd][:, qcols[qb]]
                a0, b0 = stage0_blocks[qb]
                s_buf[slot, hd, lo, qcols[qb]] = _dot(k_ref[0, pl.ds(a0, blk_len), :], q_cols)
                s_buf[slot, hd, hi, qcols[qb]] = _dot(k_ref[0, pl.ds(b0, blk_len), :], q_cols)

    def far_logits_into(slot, t):
        start = pl.multiple_of(2 * t * blk_len, 2 * blk_len)
        for hd in range(2):
            s_buf[slot, hd] = _dot(k_ref[0, pl.ds(start, 2 * blk_len), :], qhs[hd])

    def pv_from(slot, blocks):
        res = []
        for hd in range(2):
            rows = slice(hd * half, (hd + 1) * half)
            parts = [_dot(vt_ref[0, rows, pl.ds(a, blk_len)], p_buf[slot, hd, lo, qcols[qb]])
                     + _dot(vt_ref[0, rows, pl.ds(b, blk_len)], p_buf[slot, hd, hi, qcols[qb]])
                     for qb, (a, b) in enumerate(blocks)]
            res.append(jnp.concatenate(parts, axis=1))
        return res

    def softmax_stage(s_slot, p_slot, hd, cols, row_a, row_b, m_old, tiles=None):
        s_a, s_b = s_buf[s_slot, hd, lo, cols], s_buf[s_slot, hd, hi, cols]
        if tiles is not None:
            s_a, s_b = s_a + tiles[0], s_b + tiles[1]
        m_new = jnp.maximum(jnp.max(s_a, axis=0, keepdims=True) + row_a, jnp.max(s_b, axis=0, keepdims=True) + row_b)
        if m_old is not None:
            m_new = jnp.maximum(m_new, m_old)
        p_a = jnp.exp2(s_a * log2e + (row_a - m_new) * log2e)
        p_b = jnp.exp2(s_b * log2e + (row_b - m_new) * log2e)
        p_buf[p_slot, hd, lo, cols] = p_a.astype(_BF16)
        p_buf[p_slot, hd, hi, cols] = p_b.astype(_BF16)
        return m_new, jnp.sum(p_a, axis=0, keepdims=True) + jnp.sum(p_b, axis=0, keepdims=True)

    stage0_logits_into(0)
    far_logits_into(1, 0)
    for hd in range(2):
        for qb in range(n_qb):
            first = owns[qb] == 0
            tiles = (bias_ref[hd, jnp.where(first, 0, 1)], bias_ref[hd, 0])
            m0, l0 = softmax_stage(0, 0, hd, qcols[qb], adj_pens[hd][:, qcols[qb]],
                                   jnp.where(first, _NEG_INF, 0.0), None, tiles=tiles)
            m_ref[hd, :, qcols[qb]] = m0
            l_ref[hd, :, qcols[qb]] = l0
        alpha_ref[hd] = jnp.ones((1, n_query), _F32)
        acc_ref[hd] = jnp.zeros((half, n_query), _F32)

    def far_stage(t, slot):
        pv = pv_from(slot, blocks_before_far_stage(t))
        for hd in range(2):
            acc_ref[hd] = alpha_ref[hd] * acc_ref[hd] + pv[hd]
            for cols in qcols:
                m_old = m_ref[hd, :, cols]
                m_new, l_blk = softmax_stage(1 - slot, 1 - slot, hd, cols, pen_ref[hd, pl.ds(2 * t, 1), cols],
                                             pen_ref[hd, pl.ds(2 * t + 1, 1), cols], m_old)
                alpha = jnp.exp(m_old - m_new)
                l_ref[hd, :, cols] = alpha * l_ref[hd, :, cols] + l_blk
                m_ref[hd, :, cols] = m_new
                alpha_ref[hd, :, cols] = alpha
        far_logits_into(slot, jnp.minimum(t + 1, n_blk // 2 - 1))

    def two_far_stages(u, carry):
        far_stage(2 * u, 0)
        far_stage(2 * u + 1, 1)
        return carry

    n_iter = (n_far + 1) // 2
    lax.fori_loop(0, n_iter, two_far_stages, 0)
    pv = pv_from(0, blocks_before_far_stage(2 * n_iter))
    outs = [(alpha_ref[hd] * acc_ref[hd] + pv[hd]) * (1.0 / l_ref[hd]) for hd in range(2)]
    o_ref[0] = jnp.concatenate(outs, axis=0).T.astype(_BF16)


def _moba_prompt(q, kb, vt, rel_bias):
    nb, attn_dim, s = q.shape
    n_blk = s // _MOBA_BLOCK
    assert n_blk >= 2 and n_blk % 2 == 0 and _QUERY_BLOCKS_PER_STEP in (1, 2)
    n_pair = attn_dim // _LANES
    n_query = _QUERY_BLOCKS_PER_STEP * _MOBA_BLOCK
    kk = jnp.arange(_MOBA_BLOCK, dtype=jnp.int32)[:, None]
    qq = jnp.arange(_MOBA_BLOCK, dtype=jnp.int32)[None, :]
    table = rel_bias.astype(_F32)
    tiles = _bias_of_dist(table, jnp.stack([qq - kk, _MOBA_BLOCK + qq - kk]))
    far = _bias_of_dist(table, jnp.full((1,), _MOBA_BLOCK + 1, jnp.int32))[:, 0]
    return pl.pallas_call(
        functools.partial(_moba_prompt_kernel, n_blk=n_blk),
        grid=(nb, n_pair, s // n_query),
        in_specs=[pl.BlockSpec(memory_space=pltpu.SMEM),
                  pl.BlockSpec((1, _LANES, n_query), lambda b, h, c: (b, h, c)),
                  pl.BlockSpec((1, s, _LANES), lambda b, h, c: (b, 0, h)),
                  pl.BlockSpec((1, _LANES, s), lambda b, h, c: (b, h, 0)),
                  pl.BlockSpec((2, 2, _MOBA_BLOCK, _MOBA_BLOCK), lambda b, h, c: (h, 0, 0, 0))],
        out_specs=pl.BlockSpec((1, n_query, _LANES), lambda b, h, c: (b, c, h)),
        out_shape=jax.ShapeDtypeStruct((nb, s, attn_dim), _BF16),
        scratch_shapes=[pltpu.VMEM((n_blk, _LANES), _F32),
                        pltpu.VMEM((2, n_blk, n_query), _F32),
                        pltpu.VMEM((2, 2, 2 * _MOBA_BLOCK, n_query), _F32),
                        pltpu.VMEM((2, 2, 2 * _MOBA_BLOCK, n_query), _BF16),
                        pltpu.VMEM((2, 1, n_query), _F32), pltpu.VMEM((2, 1, n_query), _F32),
                        pltpu.VMEM((2, 1, n_query), _F32),
                        pltpu.VMEM((2, _HEAD_DIM, n_query), _F32)],
        compiler_params=_params(3),
        name="moba_prompt",
    )(far, q, kb, vt, tiles)


def _moba_step_kernel(pt_ref, qbd_ref, knew_ref, vnew_ref, badj_ref, bfar_ref, bown_ref, *rest, n_pages, n_q):
    del pt_ref
    kt_pages = rest[:n_pages]
    vt_pages = rest[n_pages:2 * n_pages]
    o_ref = rest[2 * n_pages]
    ppb = _MOBA_BLOCK // _PAGE_SIZE
    n_blk = n_pages // ppb
    qbd = qbd_ref[0]
    feat = qbd.shape[1]

    def block_of(pages, j):
        return jnp.concatenate([pages[p][0] for p in range(j * ppb, (j + 1) * ppb)], axis=1).astype(_BF16)

    s_blk = [_dot(qbd, block_of(kt_pages, j)) for j in range(n_blk)]
    score = [jnp.sum(s, axis=1, keepdims=True) for s in s_blk]
    logits = []
    for n in range(n_blk):
        rank = jnp.zeros(score[n].shape, jnp.int32)
        for m in range(n_blk):
            if m < n:
                rank = rank + jnp.where(score[m] >= score[n], 1, 0)
            elif m > n:
                rank = rank + jnp.where(score[m] > score[n], 1, 0)
        bias = badj_ref[...] if n == n_blk - 1 else bfar_ref[:, 0:1]
        logits.append(jnp.where(rank < _MOBA_TOPK, s_blk[n] + bias, _NEG_INF))
    s_own = _nt_dot(qbd, knew_ref[0].astype(_BF16)) + bown_ref[:, 0:_NEW_ROWS]
    m_row = jnp.max(s_own, axis=1, keepdims=True)
    for lg in logits:
        m_row = jnp.maximum(m_row, jnp.max(lg, axis=1, keepdims=True))
    p_own = jnp.exp(s_own - m_row)
    probs = [jnp.exp(lg - m_row) for lg in logits]
    denom = jnp.sum(p_own, axis=1, keepdims=True)
    for p in probs:
        denom = denom + jnp.sum(p, axis=1, keepdims=True)
    inv = 1.0 / denom
    out = _dot((p_own * inv).astype(_BF16), vnew_ref[0].astype(_BF16))
    for j in range(n_blk):
        out = out + _nt_dot((probs[j] * inv).astype(_BF16), block_of(vt_pages, j))
    r_head = lax.broadcasted_iota(jnp.int32, out.shape, 0) % _HEAD_GROUP
    c_head = lax.broadcasted_iota(jnp.int32, out.shape, 1) // _HEAD_DIM
    out = jnp.where(r_head == c_head, out, 0.0)
    o_ref[0] = jnp.sum(out.reshape(n_q, _HEAD_GROUP, feat), axis=1)


def _block_diag_queries(q, n_heads, head_dim, group):
    n, n_q, feat = q.shape
    r_head = jnp.arange(group, dtype=jnp.int32)[:, None]
    c_head = jnp.arange(feat, dtype=jnp.int32)[None, :] // head_dim
    mask = (r_head == c_head) & (r_head < n_heads)
    return jnp.where(mask[None, None], q[:, :, None, :], jnp.zeros((), q.dtype)).reshape(n, n_q * group, feat)


def _moba_step(page_table, q, k_new, v_new, cache_kt, cache_vt, rel_bias):
    n_seq, n_q, feat = q.shape
    n_pages = page_table.shape[1]
    past = n_pages * _PAGE_SIZE
    assert past % _MOBA_BLOCK == 0 and n_q <= _NEW_ROWS
    rows = n_q * _HEAD_GROUP
    qbd = _block_diag_queries(q, _N_HEADS, _HEAD_DIM, _HEAD_GROUP)
    padn = ((0, 0), (0, _NEW_ROWS - n_q), (0, 0))
    k_new = jnp.pad(k_new, padn)
    v_new = jnp.pad(v_new, padn)
    table = jnp.tile(jnp.pad(rel_bias.astype(_F32), ((0, 0), (0, _HEAD_GROUP - _N_HEADS))), (1, n_q))
    qi = jnp.arange(rows, dtype=jnp.int32)[:, None] // _HEAD_GROUP
    kk = jnp.arange(_MOBA_BLOCK, dtype=jnp.int32)[None, :]
    badj = _bias_of_dist(table, _MOBA_BLOCK + qi - kk, per_row=True)
    bfar = _bias_of_dist(table, jnp.full((rows, _LANES), _MOBA_BLOCK + 1, jnp.int32), per_row=True)
    kj = jnp.arange(_LANES, dtype=jnp.int32)[None, :]
    bown = _bias_of_dist(table, jnp.where(kj < n_q, qi - kj, -1), per_row=True)

    page = lambda p: pl.BlockSpec((1, feat, _PAGE_SIZE), lambda i, pt, p=p: (pt[i, p], 0, 0))
    seq3 = lambda r: pl.BlockSpec((1, r, feat), lambda i, pt: (i, 0, 0))
    const2 = lambda shape: pl.BlockSpec(shape, lambda i, pt: (0, 0), pipeline_mode=pl.Buffered(1))
    grid_spec = pltpu.PrefetchScalarGridSpec(
        num_scalar_prefetch=1,
        grid=(n_seq,),
        in_specs=[seq3(rows), seq3(_NEW_ROWS), seq3(_NEW_ROWS), const2((rows, _MOBA_BLOCK)),
                  const2((rows, _LANES)), const2((rows, _LANES))] + [page(p) for p in range(n_pages)] * 2,
        out_specs=pl.BlockSpec((1, n_q, feat), lambda i, pt: (i, 0, 0)))
    return pl.pallas_call(
        functools.partial(_moba_step_kernel, n_pages=n_pages, n_q=n_q),
        grid_spec=grid_spec,
        out_shape=jax.ShapeDtypeStruct((n_seq, n_q, feat), _F32),
        compiler_params=_params(1),
        name="moba_step",
    )(page_table, qbd, k_new, v_new, badj, bfar, bown, *([cache_kt] * n_pages), *([cache_vt] * n_pages))


def _softmax_rows(s):
    p = jnp.exp(s - jnp.max(s, axis=1, keepdims=True))
    return p * (1.0 / jnp.sum(p, axis=1, keepdims=True))


def _memattn_kernel(qm_ref, mk_ref, mv_ref, o_ref):
    scale = _MEM_HEAD_DIM ** -0.5
    for h in range(_MEM_HEADS):
        hs = slice(h * _MEM_HEAD_DIM, (h + 1) * _MEM_HEAD_DIM)
        p = _softmax_rows(_nt_dot(qm_ref[0, :, hs], mk_ref[0, :, hs]) * scale)
        o_ref[0, :, hs] = _dot(p.astype(_BF16), mv_ref[0, :, hs]).astype(_BF16)


def _memattn(qm, mkb, mvb, tq):
    nb, s, md = qm.shape
    m = mkb.shape[1]
    return pl.pallas_call(
        _memattn_kernel,
        grid=(nb, s // tq),
        in_specs=[pl.BlockSpec((1, tq, md), lambda b, i: (b, i, 0)),
                  pl.BlockSpec((1, m, md), lambda b, i: (b, 0, 0)),
                  pl.BlockSpec((1, m, md), lambda b, i: (b, 0, 0))],
        out_specs=pl.BlockSpec((1, tq, md), lambda b, i: (b, i, 0)),
        out_shape=jax.ShapeDtypeStruct((nb, s, md), _BF16),
        compiler_params=_params(2),
        name="memattn",
    )(qm, mkb, mvb)


def _memattn_step_kernel(q_ref, mk_ref, mv_ref, o_ref):
    scale = _MEM_HEAD_DIM ** -0.5
    for i in range(mk_ref.shape[0]):
        s = _nt_dot(q_ref[i], mk_ref[i].astype(_BF16)) * scale
        r_head = lax.broadcasted_iota(jnp.int32, s.shape, 0) % _MEM_HEADS
        c_head = lax.broadcasted_iota(jnp.int32, s.shape, 1) % _MEM_HEADS
        p = _softmax_rows(jnp.where(r_head == c_head, s, _NEG_INF))
        o_ref[i] = _dot(p.astype(_BF16), mv_ref[i].astype(_BF16))


def _memattn_step(qm, mem_k, mem_v, seqs_per_step):
    n_seq, rows, dh = qm.shape
    mh = mem_k.shape[1]
    blk = lambda i: (i, 0, 0)
    return pl.pallas_call(
        _memattn_step_kernel,
        grid=(n_seq // seqs_per_step,),
        in_specs=[pl.BlockSpec((seqs_per_step, rows, dh), blk), pl.BlockSpec((seqs_per_step, mh, dh), blk),
                  pl.BlockSpec((seqs_per_step, mh, dh), blk)],
        out_specs=pl.BlockSpec((seqs_per_step, rows, dh), blk),
        out_shape=jax.ShapeDtypeStruct((n_seq, rows, dh), _F32),
        compiler_params=_params(1),
        name="memattn_step",
    )(qm, mem_k, mem_v)


def _merge_kernel(h_ref, c_ref, o_ref, om_ref, gate_ref, wc_ref, wa_ref, wm_ref, wo_ref, gpost_ref,
                  g2pre_ref, w2in_ref, w2out_ref, g2post_ref, y_ref, *, n_chunk):
    d = h_ref.shape[1]
    merged = (gate_ref[:, 0:d].astype(_F32) * _dot(c_ref[...], wc_ref[...])
              + gate_ref[:, d:2 * d].astype(_F32) * _dot(o_ref[...], wa_ref[...])
              + gate_ref[:, 2 * d:3 * d].astype(_F32) * _dot(om_ref[...], wm_ref[...]))
    h = h_ref[...] + _rms_norm(_dot(merged.astype(_BF16), wo_ref[...]), gpost_ref[...])
    y_ref[...] = _ffn_half_step(h, g2pre_ref[...], w2in_ref, w2out_ref, g2post_ref[...], n_chunk)


def _merge(h, c_act, o, om, gates, w_conv_out, w_attn_out, w_mem_out, w_out, g_post, g2_pre, w2_in, w2_out,
           g2_post, tm):
    n, d = h.shape
    row = lambda i: (i, 0)
    tile = lambda a: pl.BlockSpec((tm, a.shape[1]), row)
    return pl.pallas_call(
        functools.partial(_merge_kernel, n_chunk=_ffn_chunks(w2_out.shape[0])),
        grid=(n // tm,),
        in_specs=[tile(h), tile(c_act), tile(o), tile(om), tile(gates),
                  _resident(w_conv_out.shape), _resident(w_attn_out.shape), _resident(w_mem_out.shape),
                  _resident(w_out.shape), _resident((1, d)), _resident((1, d)), _resident(w2_in.shape),
                  _resident(w2_out.shape), _resident((1, d))],
        out_specs=pl.BlockSpec((tm, d), row),
        out_shape=jax.ShapeDtypeStruct((n, d), _F32),
        compiler_params=_params(1),
        name="merge_ffn2",
    )(h, c_act, o, om, gates, w_conv_out, w_attn_out, w_mem_out, w_out, g_post, g2_pre, w2_in, w2_out, g2_post)


def _tile(n, want):
    t = min(n, want)
    assert n % t == 0
    return t


def _token_front(x, p):
    nb, s, d = x.shape
    n = nb * s
    h, u = _ffn1(x.reshape(n, d), p['g_ffn1_pre'], p['w_ffn1_in'], p['w_ffn1_out'], p['g_ffn1_post'],
                 p['g_mix_pre'], _tile(n, 512))
    return h, u.reshape(nb, s, d)


def _token_back(h, c_act, o, om, gates, p):
    n = h.shape[0]
    flat = lambda a: a.reshape(n, a.shape[-1])
    return _merge(h, flat(c_act), flat(o), flat(om), flat(gates), p['w_conv_out'], p['w_attn_out'],
                  p['w_mem_out'], p['w_out'], p['g_mix_post'], p['g_ffn2_pre'], p['w_ffn2_in'], p['w_ffn2_out'],
                  p['g_ffn2_post'], _tile(n, 512))


def _dims(p):
    conv_dim = p['w_conv_out'].shape[0]
    attn_dim = p['w_attn_out'].shape[0]
    mem_dim = p['w_mem_out'].shape[0]
    return conv_dim, attn_dim, mem_dim


def _prompt_layer(x, mem, rel_bias, p):
    nb, s, d = x.shape
    conv_dim, attn_dim, mem_dim = _dims(p)
    h, u = _token_front(x, p)
    glu, q, k, v, qm, gates, kb, vt, c_act = _inproj(
        u, p['w_in'], p['b_gate'], _tile(s, 512), conv_dim, attn_dim, mem_dim,
        conv_params=(p['w_dwconv'], p['b_dwconv'], p['g_conv_ln'], p['b_conv_ln']))
    mk, mv, mkb, mvb = _memkv(mem, p['g_mem'], p['w_mem_kv'])
    o = _moba_prompt(q, kb, vt, rel_bias)
    om = _memattn(qm, mkb, mvb, _tile(s, 512))
    y = _token_back(h, c_act, o, om, gates, p).reshape(nb, s, d)
    conv_state = glu[:, s - (_CONV_WIDTH - 1):]
    heads = lambda a: a.reshape(nb, s, _N_HEADS, _HEAD_DIM)
    mheads = lambda a: a.reshape(nb, a.shape[1], _MEM_HEADS, _MEM_HEAD_DIM)
    return y, heads(k), heads(v), mheads(mk), mheads(mv), conv_state


def _sample_layer(x, conv_prev, cache_k, cache_v, page_table, mem_k, mem_v, rel_bias, p):
    n_seq, n_q, d = x.shape
    conv_dim, attn_dim, mem_dim = _dims(p)
    n = n_seq * n_q
    h, u = _token_front(x.reshape(1, n, d), p)
    glu, q, k, v, qm, gates = _inproj(u, p['w_in'], p['b_gate'], _tile(n, 512), conv_dim, attn_dim, mem_dim)
    per_seq = lambda a: a.reshape(n_seq, n_q, a.shape[-1])
    glu = per_seq(glu)
    conv_in = jnp.concatenate([conv_prev, glu], axis=1)
    c_act = _conv_step(conv_in.transpose(1, 0, 2), p['w_dwconv'], p['b_dwconv'], p['g_conv_ln'], p['b_conv_ln'])
    c_act = c_act.transpose(1, 0, 2)
    page_t = lambda c: c.transpose(0, 2, 3, 1).reshape(c.shape[0], attn_dim, c.shape[1])
    o = _moba_step(page_table, per_seq(q), per_seq(k), per_seq(v), page_t(cache_k), page_t(cache_v), rel_bias)
    head_rows = lambda a: a.reshape(n_seq, -1, _MEM_HEAD_DIM)
    om = _memattn_step(head_rows(qm), head_rows(mem_k), head_rows(mem_v), 8)
    y = _token_back(h, c_act, o.astype(_BF16), om.reshape(n, mem_dim).astype(_BF16), gates, p)
    conv_state = conv_in[:, n_q:]
    heads = lambda a: a.reshape(n_seq, n_q, _N_HEADS, _HEAD_DIM)
    return y.reshape(n_seq, n_q, d), heads(k), heads(v), conv_state


def _layer_params(l, g_ffn1_pre, w_ffn1_in, w_ffn1_out, g_ffn1_post, g_mix_pre, w_in, b_gate, w_dwconv, b_dwconv,
                  g_conv_ln, b_conv_ln, w_conv_out, w_attn_out, g_mem, w_mem_kv, w_mem_out, w_out, g_mix_post,
                  g_ffn2_pre, w_ffn2_in, w_ffn2_out, g_ffn2_post):
    vec = lambda a: a[l].astype(_F32).reshape(1, -1)
    mat = lambda a: a[l].astype(_BF16)
    return dict(
        g_ffn1_pre=vec(g_ffn1_pre), w_ffn1_in=mat(w_ffn1_in), w_ffn1_out=mat(w_ffn1_out),
        g_ffn1_post=vec(g_ffn1_post), g_mix_pre=vec(g_mix_pre), w_in=mat(w_in), b_gate=vec(b_gate),
        w_dwconv=w_dwconv[l].astype(_F32), b_dwconv=vec(b_dwconv), g_conv_ln=vec(g_conv_ln),
        b_conv_ln=vec(b_conv_ln), w_conv_out=mat(w_conv_out), w_attn_out=mat(w_attn_out), g_mem=vec(g_mem),
        w_mem_kv=mat(w_mem_kv), w_mem_out=mat(w_mem_out), w_out=mat(w_out), g_mix_post=vec(g_mix_post),
        g_ffn2_pre=vec(g_ffn2_pre), w_ffn2_in=mat(w_ffn2_in), w_ffn2_out=mat(w_ffn2_out),
        g_ffn2_post=vec(g_ffn2_post))


def kernel(x_prompt, x_sample, cache_k, cache_v, cache_mem_k, cache_mem_v, state_conv, page_table, mem_prompt, g_ffn1_pre, w_ffn1_in, w_ffn1_out, g_ffn1_post, g_mix_pre, w_in, b_gate, w_dwconv, b_dwconv, g_conv_ln, b_conv_ln, w_conv_out, w_attn_out, rel_bias, g_mem, w_mem_kv, w_mem_out, w_out, g_mix_post, g_ffn2_pre, w_ffn2_in, w_ffn2_out, g_ffn2_post):
    depth = w_in.shape[0]
    hp, hs = x_prompt, x_sample
    outs = [[] for _ in range(8)]
    for l in range(depth):
        p = _layer_params(l, g_ffn1_pre, w_ffn1_in, w_ffn1_out, g_ffn1_post, g_mix_pre, w_in, b_gate, w_dwconv,
                          b_dwconv, g_conv_ln, b_conv_ln, w_conv_out, w_attn_out, g_mem, w_mem_kv, w_mem_out,
                          w_out, g_mix_post, g_ffn2_pre, w_ffn2_in, w_ffn2_out, g_ffn2_post)
        hp, kp, vp, mkp, mvp, cp = _prompt_layer(hp, mem_prompt, rel_bias, p)
        hs, ks, vs, cs = _sample_layer(hs, state_conv[l], cache_k[l], cache_v[l], page_table, cache_mem_k[l],
                                       cache_mem_v[l], rel_bias, p)
        for acc, val in zip(outs, (kp, vp, mkp, mvp, cp, ks, vs, cs)):
            acc.append(val)
    return (hp, hs) + tuple(jnp.stack(a) for a in outs)
```

```python
import functools
import math

import jax
import jax.numpy as jnp
from jax import lax
from jax.experimental import pallas as pl
from jax.experimental.pallas import tpu as pltpu

_F32 = jnp.float32
_BF16 = jnp.bfloat16

_CONV_WIDTH = 31
_N_HEADS = 12
_HEAD_DIM = 64
_MOBA_BLOCK = 256
_MOBA_TOPK = 3
_MEM_HEADS = 4
_MEM_HEAD_DIM = 128
_N_BUCKETS = 32
_MAX_DISTANCE = 128
_N_BRANCH = 3
_PAGE_SIZE = 128
_NORM_EPS = 1e-6
_NEG_INF = -1e30

_LANES = 128
_SUBLANES = 8
_MXU_COLS_V7X = 256
_VMEM_BYTES_V7X = 64 * 1024 * 1024
_VMEM_LIMIT = _VMEM_BYTES_V7X - 8 * 1024 * 1024

_HALO_ROWS = 32
_CONV_CHUNK_ROWS = 64
_HEAD_GROUP = 16
_NEW_ROWS = 16
_QUERY_BLOCKS_PER_STEP = 2


def _nt_dot(a, b):
    return lax.dot_general(a, b, (((1,), (1,)), ((), ())), preferred_element_type=_F32)


def _dot(a, b):
    return jnp.dot(a, b, preferred_element_type=_F32)


def _sigmoid(x):
    return 1.0 / (1.0 + jnp.exp(-x))


def _rms_norm(x, g):
    return x * lax.rsqrt(jnp.mean(x * x, axis=-1, keepdims=True) + _NORM_EPS) * g


def _resident(shape):
    nd = len(shape)
    return pl.BlockSpec(shape, lambda *_: (0,) * nd, pipeline_mode=pl.Buffered(1))


def _params(n_parallel_axes):
    return pltpu.CompilerParams(
        dimension_semantics=("arbitrary",) * n_parallel_axes,
        vmem_limit_bytes=_VMEM_LIMIT)


def _ffn_half_step(x, g_pre, w_in_ref, w_out_ref, g_post, n_chunk):
    d_ff = w_out_ref.shape[0]
    fc = d_ff // n_chunk
    xn = _rms_norm(x, g_pre).astype(_BF16)
    acc = None
    for c in range(n_chunk):
        a = _dot(xn, w_in_ref[:, c * fc:(c + 1) * fc])
        b = _dot(xn, w_in_ref[:, d_ff + c * fc:d_ff + (c + 1) * fc])
        hid = (a * _sigmoid(a) * b).astype(_BF16)
        part = _dot(hid, w_out_ref[c * fc:(c + 1) * fc, :])
        acc = part if acc is None else acc + part
    return x + 0.5 * _rms_norm(acc, g_post)


def _ffn1_kernel(x_ref, gpre_ref, win_ref, wout_ref, gpost_ref, gmix_ref, h_ref, u_ref, *, n_chunk):
    h = _ffn_half_step(x_ref[...], gpre_ref[...], win_ref, wout_ref, gpost_ref[...], n_chunk)
    h_ref[...] = h
    u_ref[...] = _rms_norm(h, gmix_ref[...]).astype(_BF16)


def _ffn_chunks(d_ff):
    return d_ff // _MXU_COLS_V7X if d_ff % _MXU_COLS_V7X == 0 else 1


def _ffn1(x, g_pre, w_in, w_out, g_post, g_mix, tm):
    n, d = x.shape
    d_ff = w_out.shape[0]
    row = lambda i: (i, 0)
    return pl.pallas_call(
        functools.partial(_ffn1_kernel, n_chunk=_ffn_chunks(d_ff)),
        grid=(n // tm,),
        in_specs=[pl.BlockSpec((tm, d), row), _resident((1, d)), _resident(w_in.shape),
                  _resident(w_out.shape), _resident((1, d)), _resident((1, d))],
        out_specs=[pl.BlockSpec((tm, d), row), pl.BlockSpec((tm, d), row)],
        out_shape=[jax.ShapeDtypeStruct((n, d), _F32), jax.ShapeDtypeStruct((n, d), _BF16)],
        compiler_params=_params(1),
        name="ffn1",
    )(x, g_pre, w_in, w_out, g_post, g_mix)


def _inproj_kernel(u_ref, w_ref, bg_ref, *rest, conv_dim, attn_dim, mem_dim, whole_sequences):
    if whole_sequences:
        wd_ref, bd_ref, gln_ref, bln_ref = rest[:4]
        rest = rest[4:]
    conv_ref, q_ref, k_ref, v_ref, qm_ref, gate_ref = rest[:6]
    u = u_ref[0]
    tm = u.shape[0]
    o1 = 2 * conv_dim
    o2 = o1 + attn_dim
    o3 = o2 + attn_dim
    o4 = o3 + attn_dim
    o5 = o4 + mem_dim

    def proj(lo, hi):
        return _dot(u, w_ref[:, lo:hi])

    if whole_sequences:
        kb_ref, vt_ref, cact_ref, ext_ref, cbuf_ref = rest[6:]
        @pl.when(pl.program_id(1) == 0)
        def _():
            ext_ref[0:_HALO_ROWS, :] = jnp.zeros((_HALO_ROWS, conv_dim), _F32)

        @pl.when(pl.program_id(1) > 0)
        def _():
            ext_ref[0:_HALO_ROWS, :] = ext_ref[tm:tm + _HALO_ROWS, :]

    glu = proj(0, conv_dim) * _sigmoid(proj(conv_dim, o1))
    conv_ref[0] = glu
    if whole_sequences:
        ext_ref[_HALO_ROWS:, :] = glu
        for r0 in range(0, tm, _CONV_CHUNK_ROWS):
            _conv_module_rows(ext_ref, cbuf_ref, cact_ref, r0, _CONV_CHUNK_ROWS, wd_ref, bd_ref, gln_ref, bln_ref)
    q = proj(o1, o2) * (_HEAD_DIM ** -0.5)
    q_ref[0] = (q.T if whole_sequences else q).astype(_BF16)
    k = proj(o2, o3)
    v = proj(o3, o4)
    k_ref[0] = k
    v_ref[0] = v
    qm_ref[0] = proj(o4, o5).astype(_BF16)
    gate_ref[0] = _sigmoid(proj(o5, w_ref.shape[1]) + bg_ref[...]).astype(_BF16)
    if whole_sequences:
        kb_ref[0] = k.astype(_BF16)
        vt_ref[0] = v.T.astype(_BF16)


def _inproj(u, w_in, b_gate, tm, conv_dim, attn_dim, mem_dim, conv_params=None):
    nb, s, d = u.shape
    n_gate = b_gate.shape[1]
    whole_sequences = conv_params is not None
    tok = lambda b, i: (b, i, 0)
    in_specs = [pl.BlockSpec((1, tm, d), tok), _resident(w_in.shape), _resident((1, n_gate))]
    out_specs = [pl.BlockSpec((1, tm, conv_dim), tok), pl.BlockSpec((1, tm, attn_dim), tok),
                 pl.BlockSpec((1, tm, attn_dim), tok), pl.BlockSpec((1, tm, attn_dim), tok),
                 pl.BlockSpec((1, tm, mem_dim), tok), pl.BlockSpec((1, tm, n_gate), tok)]
    out_shape = [jax.ShapeDtypeStruct((nb, s, conv_dim), _F32), jax.ShapeDtypeStruct((nb, s, attn_dim), _BF16),
                 jax.ShapeDtypeStruct((nb, s, attn_dim), _F32), jax.ShapeDtypeStruct((nb, s, attn_dim), _F32),
                 jax.ShapeDtypeStruct((nb, s, mem_dim), _BF16), jax.ShapeDtypeStruct((nb, s, n_gate), _BF16)]
    scratch_shapes = []
    operands = [u, w_in, b_gate]
    if whole_sequences:
        assert tm % _CONV_CHUNK_ROWS == 0
        out_specs[1] = pl.BlockSpec((1, attn_dim, tm), lambda b, i: (b, 0, i))
        out_shape[1] = jax.ShapeDtypeStruct((nb, attn_dim, s), _BF16)
        in_specs += [_resident(a.shape) for a in conv_params]
        operands += list(conv_params)
        out_specs += [pl.BlockSpec((1, tm, attn_dim), tok), pl.BlockSpec((1, attn_dim, tm), lambda b, i: (b, 0, i)),
                      pl.BlockSpec((1, tm, conv_dim), tok)]
        out_shape += [jax.ShapeDtypeStruct((nb, s, attn_dim), _BF16), jax.ShapeDtypeStruct((nb, attn_dim, s), _BF16),
                      jax.ShapeDtypeStruct((nb, s, conv_dim), _BF16)]
        scratch_shapes = [pltpu.VMEM((tm + _HALO_ROWS, conv_dim), _F32), pltpu.VMEM((tm, conv_dim), _F32)]
    return pl.pallas_call(
        functools.partial(_inproj_kernel, conv_dim=conv_dim, attn_dim=attn_dim, mem_dim=mem_dim,
                          whole_sequences=whole_sequences),
        grid=(nb, s // tm),
        in_specs=in_specs, out_specs=out_specs, out_shape=out_shape, scratch_shapes=scratch_shapes,
        compiler_params=_params(2),
        name="inproj",
    )(*operands)


def _memkv_kernel(mem_ref, g_ref, w_ref, mk_ref, mv_ref, mkb_ref, mvb_ref):
    kv = _dot(_rms_norm(mem_ref[0], g_ref[...]).astype(_BF16), w_ref[...])
    half = kv.shape[1] // 2
    mk_ref[0] = kv[:, :half]
    mv_ref[0] = kv[:, half:]
    mkb_ref[0] = kv[:, :half].T.astype(_BF16)
    mvb_ref[0] = kv[:, half:].astype(_BF16)


def _memkv(mem, g_mem, w_mem_kv):
    nb, m, d = mem.shape
    md = w_mem_kv.shape[1] // 2
    blk = lambda b: (b, 0, 0)
    return pl.pallas_call(
        _memkv_kernel,
        grid=(nb,),
        in_specs=[pl.BlockSpec((1, m, d), blk), _resident((1, d)), _resident(w_mem_kv.shape)],
        out_specs=[pl.BlockSpec((1, m, md), blk)] * 2 + [pl.BlockSpec((1, md, m), blk), pl.BlockSpec((1, m, md), blk)],
        out_shape=[jax.ShapeDtypeStruct((nb, m, md), _F32)] * 2
                  + [jax.ShapeDtypeStruct((nb, md, m), _BF16), jax.ShapeDtypeStruct((nb, m, md), _BF16)],
        compiler_params=_params(1),
        name="memkv",
    )(mem, g_mem, w_mem_kv)


def _ln_swish(c, g, b):
    mu = jnp.mean(c, axis=-1, keepdims=True)
    cc = c - mu
    var = jnp.mean(cc * cc, axis=-1, keepdims=True)
    y = cc * lax.rsqrt(var + _NORM_EPS) * g + b
    return y * _sigmoid(y)


def _conv_module_rows(ext_ref, cbuf_ref, out_ref, r0, rows, wd_ref, bd_ref, g_ref, b_ref):
    ch = ext_ref.shape[1]
    lead = _HALO_ROWS - (_CONV_WIDTH - 1)
    n_vreg = rows // _SUBLANES
    n_win = n_vreg + _HALO_ROWS // _SUBLANES
    sub = lax.broadcasted_iota(jnp.int32, (1, _SUBLANES, _LANES), 1)
    blocks = [slice(lb * _LANES, (lb + 1) * _LANES) for lb in range(ch // _LANES)]
    rs = slice(r0, r0 + rows)
    total = None
    for ls in blocks:
        win = ext_ref[r0:r0 + rows + _HALO_ROWS, ls].reshape(n_win, _SUBLANES, _LANES)
        acc = jnp.broadcast_to(bd_ref[:, ls], (n_vreg, _SUBLANES, _LANES))
        for r in range(_SUBLANES):
            n_a = (_CONV_WIDTH - r + _SUBLANES - 1) // _SUBLANES
            base, k = divmod(lead + r, _SUBLANES)
            need = n_vreg + n_a - 1
            if k == 0:
                sh = win[base:base + need]
            else:
                rot = pltpu.roll(win[base:base + need + 1], _SUBLANES - k, axis=1)
                sh = jnp.where(sub < _SUBLANES - k, rot[:-1], rot[1:])
            for a in range(n_a):
                w = _SUBLANES * a + r
                acc = acc + sh[a:a + n_vreg] * wd_ref[w:w + 1, ls]
        acc = acc.reshape(rows, _LANES)
        cbuf_ref[rs, ls] = acc
        total = acc if total is None else total + acc
    mu = jnp.sum(total, axis=-1, keepdims=True) * (1.0 / ch)
    sq = None
    for ls in blocks:
        d = cbuf_ref[rs, ls] - mu
        sq = d * d if sq is None else sq + d * d
    rstd = lax.rsqrt(jnp.sum(sq, axis=-1, keepdims=True) * (1.0 / ch) + _NORM_EPS)
    for ls in blocks:
        y = (cbuf_ref[rs, ls] - mu) * rstd * g_ref[:, ls] + b_ref[:, ls]
        out_ref[0, rs, ls] = (y * _sigmoid(y)).astype(_BF16)


def _conv_step_kernel(ext_ref, wd_ref, bd_ref, g_ref, b_ref, out_ref):
    n_q = out_ref.shape[0]
    for t in range(n_q):
        acc = jnp.broadcast_to(bd_ref[...], ext_ref.shape[1:])
        for w in range(_CONV_WIDTH):
            acc = acc + ext_ref[t + w] * wd_ref[w:w + 1, :]
        out_ref[t] = _ln_swish(acc, g_ref[...], b_ref[...]).astype(_BF16)


def _conv_step(ext_tm, w_dw, b_dw, g_ln, b_ln):
    n_rows, n_seq, ch = ext_tm.shape
    n_q = n_rows - (_CONV_WIDTH - 1)
    return pl.pallas_call(
        _conv_step_kernel,
        grid=(1,),
        in_specs=[_resident(ext_tm.shape), _resident(w_dw.shape), _resident((1, ch)), _resident((1, ch)),
                  _resident((1, ch))],
        out_specs=pl.BlockSpec((n_q, n_seq, ch), lambda i: (0, 0, 0)),
        out_shape=jax.ShapeDtypeStruct((n_q, n_seq, ch), _BF16),
        compiler_params=_params(1),
        name="conv_step",
    )(ext_tm, w_dw, b_dw, g_ln, b_ln)


def _rel_bucket(dist):
    n = jnp.maximum(dist, 0)
    max_exact = _N_BUCKETS // 2
    nf = jnp.maximum(n, 1).astype(_F32)
    large = max_exact + (jnp.log(nf / max_exact) / math.log(_MAX_DISTANCE / max_exact)
                         * (_N_BUCKETS - max_exact)).astype(jnp.int32)
    large = jnp.minimum(large, _N_BUCKETS - 1)
    return jnp.where(n < max_exact, n, large)


def _bias_of_dist(table, dist, per_row=False):
    n_col = table.shape[1]
    bucket = jnp.where(dist >= 0, _rel_bucket(dist), -1)
    if per_row:
        col_shape = (n_col,) + (1,) * (dist.ndim - 1)
    else:
        bucket = bucket[None]
        col_shape = (n_col,) + (1,) * dist.ndim
    out = jnp.full(jnp.broadcast_shapes(bucket.shape, col_shape), _NEG_INF, _F32)
    for b in range(_N_BUCKETS):
        out = jnp.where(bucket == b, table[b].reshape(col_shape), out)
    return out


def _topk_block_mask(score, blk, n_valid):
    rank = jnp.zeros(score.shape, jnp.int32)
    for m in range(score.shape[0]):
        sm = score[m:m + 1, :]
        counts = jnp.where(m < n_valid, 1, 0)
        rank = rank + jnp.where(sm > score, counts, 0) + jnp.where(sm == score, jnp.where(m < blk, counts, 0), 0)
    return (blk < n_valid) & (rank < _MOBA_TOPK)


def _moba_prompt_kernel(far_ref, q_ref, k_ref, vt_ref, bias_ref, o_ref, kmean_ref, pen_ref, s_buf, p_buf,
                        m_ref, l_ref, alpha_ref, acc_ref, *, n_blk):
    blk_len = _MOBA_BLOCK
    n_query = q_ref.shape[2]
    n_qb = n_query // blk_len
    hp = pl.program_id(1)
    g = pl.program_id(2)
    half = _HEAD_DIM
    owns = [n_qb * g + i for i in range(n_qb)]
    qcols = [slice(i * blk_len, (i + 1) * blk_len) for i in range(n_qb)]
    n_far = owns[0] // 2

    @pl.when(g == 0)
    def _():
        for j in range(n_blk):
            kj = k_ref[0, j * blk_len:(j + 1) * blk_len, :].astype(_F32)
            kmean_ref[j:j + 1, :] = jnp.mean(kj, axis=0, keepdims=True)

    qf = q_ref[0].astype(_F32)
    feature = lax.broadcasted_iota(jnp.int32, qf.shape, 0)
    blk = lax.broadcasted_iota(jnp.int32, (n_blk, n_query), 0)
    own_q = owns[0] + lax.broadcasted_iota(jnp.int32, (1, n_query), 1) // blk_len
    kmean = kmean_ref[...].astype(_BF16)

    qhs, adj_pens = [], []
    for hd in range(2):
        qh = jnp.where((feature >= hd * half) & (feature < (hd + 1) * half), qf, 0.0).astype(_BF16)
        sel = _topk_block_mask(_dot(kmean, qh), blk, own_q)
        pen_ref[hd] = jnp.where(sel & (blk < own_q - 1), far_ref[hp * 2 + hd], _NEG_INF)
        sel_adj = jnp.sum(jnp.where(sel & (blk == own_q - 1), 1.0, 0.0), axis=0, keepdims=True)
        adj_pens.append(jnp.where(own_q == 0, 0.0, jnp.where(sel_adj > 0.5, 0.0, _NEG_INF)))
        qhs.append(qh)

    log2e = math.log2(math.e)
    lo, hi = slice(0, blk_len), slice(blk_len, 2 * blk_len)

    def block_start(j):
        return pl.multiple_of(j * blk_len, blk_len)

    stage0_blocks = [(block_start(jnp.maximum(o - 1, 0)), block_start(jnp.maximum(o, 1))) for o in owns]

    def blocks_before_far_stage(t):
        far_a = (2 * t - 2) * blk_len
        return [(pl.multiple_of(jnp.where(t == 0, a0, far_a), blk_len),
                 pl.multiple_of(jnp.where(t == 0, b0, far_a + blk_len), blk_len)) for a0, b0 in stage0_blocks]

    def stage0_logits_into(slot):
        for hd in range(2):
            for qb in range(n_qb):
                q_cols = qhs[hd][:, qcols[qb]]
                a0, b0 = stage0_blocks[qb]
                s_buf[slot, hd, lo, qcols[qb]] = _dot(k_ref[0, pl.ds(a0, blk_len), :], q_cols)
                s_buf[slot, hd, hi, qcols[qb]] = _dot(k_ref[0, pl.ds(b0, blk_len), :], q_cols)

    def far_logits_into(slot, t):
        start = pl.multiple_of(2 * t * blk_len, 2 * blk_len)
        for hd in range(2):
            s_buf[slot, hd] = _dot(k_ref[0, pl.ds(start, 2 * blk_len), :], qhs[hd])

    def pv_from(slot, blocks):
        res = []
        for hd in range(2):
            rows = slice(hd * half, (hd + 1) * half)
            parts = [_dot(vt_ref[0, rows, pl.ds(a, blk_len)], p_buf[slot, hd, lo, qcols[qb]])
                     + _dot(vt_ref[0, rows, pl.ds(b, blk_len)], p_buf[slot, hd, hi, qcols[qb]])
                     for qb, (a, b) in enumerate(blocks)]
            res.append(jnp.concatenate(parts, axis=1))
        return res

    def softmax_stage(s_slot, p_slot, hd, cols, row_a, row_b, m_old, tiles=None):
        s_a, s_b = s_buf[s_slot, hd, lo, cols], s_buf[s_slot, hd, hi, cols]
        if tiles is not None:
            s_a, s_b = s_a + tiles[0], s_b + tiles[1]
        m_new = jnp.maximum(jnp.max(s_a, axis=0, keepdims=True) + row_a, jnp.max(s_b, axis=0, keepdims=True) + row_b)
        if m_old is not None:
            m_new = jnp.maximum(m_new, m_old)
        p_a = jnp.exp2(s_a * log2e + (row_a - m_new) * log2e)
        p_b = jnp.exp2(s_b * log2e + (row_b - m_new) * log2e)
        p_buf[p_slot, hd, lo, cols] = p_a.astype(_BF16)
        p_buf[p_slot, hd, hi, cols] = p_b.astype(_BF16)
        return m_new, jnp.sum(p_a, axis=0, keepdims=True) + jnp.sum(p_b, axis=0, keepdims=True)

    stage0_logits_into(0)
    far_logits_into(1, 0)
    for hd in range(2):
        for qb in range(n_qb):
            first = owns[qb] == 0
            tiles = (bias_ref[hd, jnp.where(first, 0, 1)], bias_ref[hd, 0])
            m0, l0 = softmax_stage(0, 0, hd, qcols[qb], adj_pens[hd][:, qcols[qb]],
                                   jnp.where(first, _NEG_INF, 0.0), None, tiles=tiles)
            m_ref[hd, :, qcols[qb]] = m0
            l_ref[hd, :, qcols[qb]] = l0
        alpha_ref[hd] = jnp.ones((1, n_query), _F32)
        acc_ref[hd] = jnp.zeros((half, n_query), _F32)

    def far_stage(t, slot):
        pv = pv_from(slot, blocks_before_far_stage(t))
        for hd in range(2):
            acc_ref[hd] = alpha_ref[hd] * acc_ref[hd] + pv[hd]
            for cols in qcols:
                m_old = m_ref[hd, :, cols]
                m_new, l_blk = softmax_stage(1 - slot, 1 - slot, hd, cols, pen_ref[hd, pl.ds(2 * t, 1), cols],
                                             pen_ref[hd, pl.ds(2 * t + 1, 1), cols], m_old)
                alpha = jnp.exp(m_old - m_new)
                l_ref[hd, :, cols] = alpha * l_ref[hd, :, cols] + l_blk
                m_ref[hd, :, cols] = m_new
                alpha_ref[hd, :, cols] = alpha
        far_logits_into(slot, jnp.minimum(t + 1, n_blk // 2 - 1))

    def two_far_stages(u, carry):
        far_stage(2 * u, 0)
        far_stage(2 * u + 1, 1)
        return carry

    n_iter = (n_far + 1) // 2
    lax.fori_loop(0, n_iter, two_far_stages, 0)
    pv = pv_from(0, blocks_before_far_stage(2 * n_iter))
    outs = [(alpha_ref[hd] * acc_ref[hd] + pv[hd]) * (1.0 / l_ref[hd]) for hd in range(2)]
    o_ref[0] = jnp.concatenate(outs, axis=0).T.astype(_BF16)


def _moba_prompt(q, kb, vt, rel_bias):
    nb, attn_dim, s = q.shape
    n_blk = s // _MOBA_BLOCK
    assert n_blk >= 2 and n_blk % 2 == 0 and _QUERY_BLOCKS_PER_STEP in (1, 2)
    n_pair = attn_dim // _LANES
    n_query = _QUERY_BLOCKS_PER_STEP * _MOBA_BLOCK
    kk = jnp.arange(_MOBA_BLOCK, dtype=jnp.int32)[:, None]
    qq = jnp.arange(_MOBA_BLOCK, dtype=jnp.int32)[None, :]
    table = rel_bias.astype(_F32)
    tiles = _bias_of_dist(table, jnp.stack([qq - kk, _MOBA_BLOCK + qq - kk]))
    far = _bias_of_dist(table, jnp.full((1,), _MOBA_BLOCK + 1, jnp.int32))[:, 0]
    return pl.pallas_call(
        functools.partial(_moba_prompt_kernel, n_blk=n_blk),
        grid=(nb, n_pair, s // n_query),
        in_specs=[pl.BlockSpec(memory_space=pltpu.SMEM),
                  pl.BlockSpec((1, _LANES, n_query), lambda b, h, c: (b, h, c)),
                  pl.BlockSpec((1, s, _LANES), lambda b, h, c: (b, 0, h)),
                  pl.BlockSpec((1, _LANES, s), lambda b, h, c: (b, h, 0)),
                  pl.BlockSpec((2, 2, _MOBA_BLOCK, _MOBA_BLOCK), lambda b, h, c: (h, 0, 0, 0))],
        out_specs=pl.BlockSpec((1, n_query, _LANES), lambda b, h, c: (b, c, h)),
        out_shape=jax.ShapeDtypeStruct((nb, s, attn_dim), _BF16),
        scratch_shapes=[pltpu.VMEM((n_blk, _LANES), _F32),
                        pltpu.VMEM((2, n_blk, n_query), _F32),
                        pltpu.VMEM((2, 2, 2 * _MOBA_BLOCK, n_query), _F32),
                        pltpu.VMEM((2, 2, 2 * _MOBA_BLOCK, n_query), _BF16),
                        pltpu.VMEM((2, 1, n_query), _F32), pltpu.VMEM((2, 1, n_query), _F32),
                        pltpu.VMEM((2, 1, n_query), _F32),
                        pltpu.VMEM((2, _HEAD_DIM, n_query), _F32)],
        compiler_params=_params(3),
        name="moba_prompt",
    )(far, q, kb, vt, tiles)


def _moba_step_kernel(pt_ref, qbd_ref, knew_ref, vnew_ref, badj_ref, bfar_ref, bown_ref, *rest, n_pages, n_q):
    del pt_ref
    kt_pages = rest[:n_pages]
    vt_pages = rest[n_pages:2 * n_pages]
    o_ref = rest[2 * n_pages]
    ppb = _MOBA_BLOCK // _PAGE_SIZE
    n_blk = n_pages // ppb
    qbd = qbd_ref[0]
    feat = qbd.shape[1]

    def block_of(pages, j):
        return jnp.concatenate([pages[p][0] for p in range(j * ppb, (j + 1) * ppb)], axis=1).astype(_BF16)

    s_blk = [_dot(qbd, block_of(kt_pages, j)) for j in range(n_blk)]
    score = [jnp.sum(s, axis=1, keepdims=True) for s in s_blk]
    logits = []
    for n in range(n_blk):
        rank = jnp.zeros(score[n].shape, jnp.int32)
        for m in range(n_blk):
            if m < n:
                rank = rank + jnp.where(score[m] >= score[n], 1, 0)
            elif m > n:
                rank = rank + jnp.where(score[m] > score[n], 1, 0)
        bias = badj_ref[...] if n == n_blk - 1 else bfar_ref[:, 0:1]
        logits.append(jnp.where(rank < _MOBA_TOPK, s_blk[n] + bias, _NEG_INF))
    s_own = _nt_dot(qbd, knew_ref[0].astype(_BF16)) + bown_ref[:, 0:_NEW_ROWS]
    m_row = jnp.max(s_own, axis=1, keepdims=True)
    for lg in logits:
        m_row = jnp.maximum(m_row, jnp.max(lg, axis=1, keepdims=True))
    p_own = jnp.exp(s_own - m_row)
    probs = [jnp.exp(lg - m_row) for lg in logits]
    denom = jnp.sum(p_own, axis=1, keepdims=True)
    for p in probs:
        denom = denom + jnp.sum(p, axis=1, keepdims=True)
    inv = 1.0 / denom
    out = _dot((p_own * inv).astype(_BF16), vnew_ref[0].astype(_BF16))
    for j in range(n_blk):
        out = out + _nt_dot((probs[j] * inv).astype(_BF16), block_of(vt_pages, j))
    r_head = lax.broadcasted_iota(jnp.int32, out.shape, 0) % _HEAD_GROUP
    c_head = lax.broadcasted_iota(jnp.int32, out.shape, 1) // _HEAD_DIM
    out = jnp.where(r_head == c_head, out, 0.0)
    o_ref[0] = jnp.sum(out.reshape(n_q, _HEAD_GROUP, feat), axis=1)


def _block_diag_queries(q, n_heads, head_dim, group):
    n, n_q, feat = q.shape
    r_head = jnp.arange(group, dtype=jnp.int32)[:, None]
    c_head = jnp.arange(feat, dtype=jnp.int32)[None, :] // head_dim
    mask = (r_head == c_head) & (r_head < n_heads)
    return jnp.where(mask[None, None], q[:, :, None, :], jnp.zeros((), q.dtype)).reshape(n, n_q * group, feat)


def _moba_step(page_table, q, k_new, v_new, cache_kt, cache_vt, rel_bias):
    n_seq, n_q, feat = q.shape
    n_pages = page_table.shape[1]
    past = n_pages * _PAGE_SIZE
    assert past % _MOBA_BLOCK == 0 and n_q <= _NEW_ROWS
    rows = n_q * _HEAD_GROUP
    qbd = _block_diag_queries(q, _N_HEADS, _HEAD_DIM, _HEAD_GROUP)
    padn = ((0, 0), (0, _NEW_ROWS - n_q), (0, 0))
    k_new = jnp.pad(k_new, padn)
    v_new = jnp.pad(v_new, padn)
    table = jnp.tile(jnp.pad(rel_bias.astype(_F32), ((0, 0), (0, _HEAD_GROUP - _N_HEADS))), (1, n_q))
    qi = jnp.arange(rows, dtype=jnp.int32)[:, None] // _HEAD_GROUP
    kk = jnp.arange(_MOBA_BLOCK, dtype=jnp.int32)[None, :]
    badj = _bias_of_dist(table, _MOBA_BLOCK + qi - kk, per_row=True)
    bfar = _bias_of_dist(table, jnp.full((rows, _LANES), _MOBA_BLOCK + 1, jnp.int32), per_row=True)
    kj = jnp.arange(_LANES, dtype=jnp.int32)[None, :]
    bown = _bias_of_dist(table, jnp.where(kj < n_q, qi - kj, -1), per_row=True)

    page = lambda p: pl.BlockSpec((1, feat, _PAGE_SIZE), lambda i, pt, p=p: (pt[i, p], 0, 0))
    seq3 = lambda r: pl.BlockSpec((1, r, feat), lambda i, pt: (i, 0, 0))
    const2 = lambda shape: pl.BlockSpec(shape, lambda i, pt: (0, 0), pipeline_mode=pl.Buffered(1))
    grid_spec = pltpu.PrefetchScalarGridSpec(
        num_scalar_prefetch=1,
        grid=(n_seq,),
        in_specs=[seq3(rows), seq3(_NEW_ROWS), seq3(_NEW_ROWS), const2((rows, _MOBA_BLOCK)),
                  const2((rows, _LANES)), const2((rows, _LANES))] + [page(p) for p in range(n_pages)] * 2,
        out_specs=pl.BlockSpec((1, n_q, feat), lambda i, pt: (i, 0, 0)))
    return pl.pallas_call(
        functools.partial(_moba_step_kernel, n_pages=n_pages, n_q=n_q),
        grid_spec=grid_spec,
        out_shape=jax.ShapeDtypeStruct((n_seq, n_q, feat), _F32),
        compiler_params=_params(1),
        name="moba_step",
    )(page_table, qbd, k_new, v_new, badj, bfar, bown, *([cache_kt] * n_pages), *([cache_vt] * n_pages))


def _softmax_rows(s):
    p = jnp.exp(s - jnp.max(s, axis=1, keepdims=True))
    return p * (1.0 / jnp.sum(p, axis=1, keepdims=True))


def _memattn_kernel(qm_ref, mk_ref, mv_ref, o_ref):
    scale = _MEM_HEAD_DIM ** -0.5
    for h in range(_MEM_HEADS):
        hs = slice(h * _MEM_HEAD_DIM, (h + 1) * _MEM_HEAD_DIM)
        p = _softmax_rows(_dot(qm_ref[0, :, hs], mk_ref[0, hs, :]) * scale)
        o_ref[0, :, hs] = _dot(p.astype(_BF16), mv_ref[0, :, hs]).astype(_BF16)


def _memattn(qm, mkb, mvb, tq):
    nb, s, md = qm.shape
    m = mvb.shape[1]
    return pl.pallas_call(
        _memattn_kernel,
        grid=(nb, s // tq),
        in_specs=[pl.BlockSpec((1, tq, md), lambda b, i: (b, i, 0)),
                  pl.BlockSpec((1, md, m), lambda b, i: (b, 0, 0)),
                  pl.BlockSpec((1, m, md), lambda b, i: (b, 0, 0))],
        out_specs=pl.BlockSpec((1, tq, md), lambda b, i: (b, i, 0)),
        out_shape=jax.ShapeDtypeStruct((nb, s, md), _BF16),
        compiler_params=_params(2),
        name="memattn",
    )(qm, mkb, mvb)


def _memattn_step_kernel(q_ref, mk_ref, mv_ref, o_ref):
    scale = _MEM_HEAD_DIM ** -0.5
    for i in range(mk_ref.shape[0]):
        s = _nt_dot(q_ref[i], mk_ref[i].astype(_BF16)) * scale
        r_head = lax.broadcasted_iota(jnp.int32, s.shape, 0) % _MEM_HEADS
        c_head = lax.broadcasted_iota(jnp.int32, s.shape, 1) % _MEM_HEADS
        p = _softmax_rows(jnp.where(r_head == c_head, s, _NEG_INF))
        o_ref[i] = _dot(p.astype(_BF16), mv_ref[i].astype(_BF16))


def _memattn_step(qm, mem_k, mem_v, seqs_per_step):
    n_seq, rows, dh = qm.shape
    mh = mem_k.shape[1]
    blk = lambda i: (i, 0, 0)
    return pl.pallas_call(
        _memattn_step_kernel,
        grid=(n_seq // seqs_per_step,),
        in_specs=[pl.BlockSpec((seqs_per_step, rows, dh), blk), pl.BlockSpec((seqs_per_step, mh, dh), blk),
                  pl.BlockSpec((seqs_per_step, mh, dh), blk)],
        out_specs=pl.BlockSpec((seqs_per_step, rows, dh), blk),
        out_shape=jax.ShapeDtypeStruct((n_seq, rows, dh), _F32),
        compiler_params=_params(1),
        name="memattn_step",
    )(qm, mem_k, mem_v)


def _merge_kernel(h_ref, c_ref, o_ref, om_ref, gate_ref, wc_ref, wa_ref, wm_ref, wo_ref, gpost_ref,
                  g2pre_ref, w2in_ref, w2out_ref, g2post_ref, y_ref, *, n_chunk):
    d = h_ref.shape[1]
    merged = (gate_ref[:, 0:d].astype(_F32) * _dot(c_ref[...], wc_ref[...])
              + gate_ref[:, d:2 * d].astype(_F32) * _dot(o_ref[...], wa_ref[...])
              + gate_ref[:, 2 * d:3 * d].astype(_F32) * _dot(om_ref[...], wm_ref[...]))
    h = h_ref[...] + _rms_norm(_dot(merged.astype(_BF16), wo_ref[...]), gpost_ref[...])
    y_ref[...] = _ffn_half_step(h, g2pre_ref[...], w2in_ref, w2out_ref, g2post_ref[...], n_chunk)


def _merge(h, c_act, o, om, gates, w_conv_out, w_attn_out, w_mem_out, w_out, g_post, g2_pre, w2_in, w2_out,
           g2_post, tm):
    n, d = h.shape
    row = lambda i: (i, 0)
    tile = lambda a: pl.BlockSpec((tm, a.shape[1]), row)
    return pl.pallas_call(
        functools.partial(_merge_kernel, n_chunk=_ffn_chunks(w2_out.shape[0])),
        grid=(n // tm,),
        in_specs=[tile(h), tile(c_act), tile(o), tile(om), tile(gates),
                  _resident(w_conv_out.shape), _resident(w_attn_out.shape), _resident(w_mem_out.shape),
                  _resident(w_out.shape), _resident((1, d)), _resident((1, d)), _resident(w2_in.shape),
                  _resident(w2_out.shape), _resident((1, d))],
        out_specs=pl.BlockSpec((tm, d), row),
        out_shape=jax.ShapeDtypeStruct((n, d), _F32),
        compiler_params=_params(1),
        name="merge_ffn2",
    )(h, c_act, o, om, gates, w_conv_out, w_attn_out, w_mem_out, w_out, g_post, g2_pre, w2_in, w2_out, g2_post)


def _tile(n, want):
    t = min(n, want)
    assert n % t == 0
    return t


def _token_front(x, p):
    nb, s, d = x.shape
    n = nb * s
    h, u = _ffn1(x.reshape(n, d), p['g_ffn1_pre'], p['w_ffn1_in'], p['w_ffn1_out'], p['g_ffn1_post'],
                 p['g_mix_pre'], _tile(n, 512))
    return h, u.reshape(nb, s, d)


def _token_back(h, c_act, o, om, gates, p):
    n = h.shape[0]
    flat = lambda a: a.reshape(n, a.shape[-1])
    return _merge(h, flat(c_act), flat(o), flat(om), flat(gates), p['w_conv_out'], p['w_attn_out'],
                  p['w_mem_out'], p['w_out'], p['g_mix_post'], p['g_ffn2_pre'], p['w_ffn2_in'], p['w_ffn2_out'],
                  p['g_ffn2_post'], _tile(n, 512))


def _dims(p):
    conv_dim = p['w_conv_out'].shape[0]
    attn_dim = p['w_attn_out'].shape[0]
    mem_dim = p['w_mem_out'].shape[0]
    return conv_dim, attn_dim, mem_dim


def _prompt_layer(x, mem, rel_bias, p):
    nb, s, d = x.shape
    conv_dim, attn_dim, mem_dim = _dims(p)
    h, u = _token_front(x, p)
    glu, q, k, v, qm, gates, kb, vt, c_act = _inproj(
        u, p['w_in'], p['b_gate'], _tile(s, 512), conv_dim, attn_dim, mem_dim,
        conv_params=(p['w_dwconv'], p['b_dwconv'], p['g_conv_ln'], p['b_conv_ln']))
    mk, mv, mkb, mvb = _memkv(mem, p['g_mem'], p['w_mem_kv'])
    o = _moba_prompt(q, kb, vt, rel_bias)
    om = _memattn(qm, mkb, mvb, _tile(s, 512))
    y = _token_back(h, c_act, o, om, gates, p).reshape(nb, s, d)
    conv_state = glu[:, s - (_CONV_WIDTH - 1):]
    heads = lambda a: a.reshape(nb, s, _N_HEADS, _HEAD_DIM)
    mheads = lambda a: a.reshape(nb, a.shape[1], _MEM_HEADS, _MEM_HEAD_DIM)
    return y, heads(k), heads(v), mheads(mk), mheads(mv), conv_state


def _sample_layer(x, conv_prev, cache_k, cache_v, page_table, mem_k, mem_v, rel_bias, p):
    n_seq, n_q, d = x.shape
    conv_dim, attn_dim, mem_dim = _dims(p)
    n = n_seq * n_q
    h, u = _token_front(x.reshape(1, n, d), p)
    glu, q, k, v, qm, gates = _inproj(u, p['w_in'], p['b_gate'], _tile(n, 512), conv_dim, attn_dim, mem_dim)
    per_seq = lambda a: a.reshape(n_seq, n_q, a.shape[-1])
    glu = per_seq(glu)
    conv_in = jnp.concatenate([conv_prev, glu], axis=1)
    c_act = _conv_step(conv_in.transpose(1, 0, 2), p['w_dwconv'], p['b_dwconv'], p['g_conv_ln'], p['b_conv_ln'])
    c_act = c_act.transpose(1, 0, 2)
    page_t = lambda c: c.transpose(0, 2, 3, 1).reshape(c.shape[0], attn_dim, c.shape[1])
    o = _moba_step(page_table, per_seq(q), per_seq(k), per_seq(v), page_t(cache_k), page_t(cache_v), rel_bias)
    head_rows = lambda a: a.reshape(n_seq, -1, _MEM_HEAD_DIM)
    om = _memattn_step(head_rows(qm), head_rows(mem_k), head_rows(mem_v), 8)
    y = _token_back(h, c_act, o.astype(_BF16), om.reshape(n, mem_dim).astype(_BF16), gates, p)
    conv_state = conv_in[:, n_q:]
    heads = lambda a: a.reshape(n_seq, n_q, _N_HEADS, _HEAD_DIM)
    return y.reshape(n_seq, n_q, d), heads(k), heads(v), conv_state


def _layer_params(l, g_ffn1_pre, w_ffn1_in, w_ffn1_out, g_ffn1_post, g_mix_pre, w_in, b_gate, w_dwconv, b_dwconv,
                  g_conv_ln, b_conv_ln, w_conv_out, w_attn_out, g_mem, w_mem_kv, w_mem_out, w_out, g_mix_post,
                  g_ffn2_pre, w_ffn2_in, w_ffn2_out, g_ffn2_post):
    vec = lambda a: a[l].astype(_F32).reshape(1, -1)
    mat = lambda a: a[l].astype(_BF16)
    return dict(
        g_ffn1_pre=vec(g_ffn1_pre), w_ffn1_in=mat(w_ffn1_in), w_ffn1_out=mat(w_ffn1_out),
        g_ffn1_post=vec(g_ffn1_post), g_mix_pre=vec(g_mix_pre), w_in=mat(w_in), b_gate=vec(b_gate),
        w_dwconv=w_dwconv[l].astype(_F32), b_dwconv=vec(b_dwconv), g_conv_ln=vec(g_conv_ln),
        b_conv_ln=vec(b_conv_ln), w_conv_out=mat(w_conv_out), w_attn_out=mat(w_attn_out), g_mem=vec(g_mem),
        w_mem_kv=mat(w_mem_kv), w_mem_out=mat(w_mem_out), w_out=mat(w_out), g_mix_post=vec(g_mix_post),
        g_ffn2_pre=vec(g_ffn2_pre), w_ffn2_in=mat(w_ffn2_in), w_ffn2_out=mat(w_ffn2_out),
        g_ffn2_post=vec(g_ffn2_post))


def kernel(x_prompt, x_sample, cache_k, cache_v, cache_mem_k, cache_mem_v, state_conv, page_table, mem_prompt, g_ffn1_pre, w_ffn1_in, w_ffn1_out, g_ffn1_post, g_mix_pre, w_in, b_gate, w_dwconv, b_dwconv, g_conv_ln, b_conv_ln, w_conv_out, w_attn_out, rel_bias, g_mem, w_mem_kv, w_mem_out, w_out, g_mix_post, g_ffn2_pre, w_ffn2_in, w_ffn2_out, g_ffn2_post):
    depth = w_in.shape[0]
    hp, hs = x_prompt, x_sample
    outs = [[] for _ in range(8)]
    for l in range(depth):
        p = _layer_params(l, g_ffn1_pre, w_ffn1_in, w_ffn1_out, g_ffn1_post, g_mix_pre, w_in, b_gate, w_dwconv,
                          b_dwconv, g_conv_ln, b_conv_ln, w_conv_out, w_attn_out, g_mem, w_mem_kv, w_mem_out,
                          w_out, g_mix_post, g_ffn2_pre, w_ffn2_in, w_ffn2_out, g_ffn2_post)
        hp, kp, vp, mkp, mvp, cp = _prompt_layer(hp, mem_prompt, rel_bias, p)
        hs, ks, vs, cs = _sample_layer(hs, state_conv[l], cache_k[l], cache_v[l], page_table, cache_mem_k[l],
                                       cache_mem_v[l], rel_bias, p)
        for acc, val in zip(outs, (kp, vp, mkp, mvp, cp, ks, vs, cs)):
            acc.append(val)
    return (hp, hs) + tuple(jnp.stack(a) for a in outs)
```
